```python
import math
import jax, jax.numpy as jnp
from jax import lax
import numpy as np

D_MODEL = 1024
BATCH = 2
SEQ = 8192
DEPTH = 2
DEC_BATCH = 32
DEC_SEQ = 16
PAST_LEN = 2048

CHUNK = 64
D_CONV = 512
CONV_W = 31
GLA_HEADS = 4
GLA_DK = D_MODEL // 2
GLA_DV = D_MODEL
HEAD_K = GLA_DK // GLA_HEADS
HEAD_V = GLA_DV // GLA_HEADS
GATE_RANK = 16
GATE_TAU = 16.0
D_FF = 2816
N_EXPERTS = 8
TOP_K = 2
N_DENSE = (DEPTH + 1) // 2
N_MOE = DEPTH // 2
ALPHA = (2.0 * DEPTH) ** 0.25
BETA = (8.0 * DEPTH) ** -0.25
LN_EPS = 1e-5
RMS_EPS = 1e-6

OFF_GLU_G = D_CONV
OFF_Q = 2 * D_CONV
OFF_K = OFF_Q + GLA_DK
OFF_V = OFF_K + GLA_DK
OFF_G = OFF_V + GLA_DV
OFF_LR = OFF_G + GLA_DV
OFF_GATES = OFF_LR + GATE_RANK
D_IN = OFF_GATES + 2 * D_MODEL

kernel_name = "hybrid_conformer_gla_deepnorm_stream_step"


def layer_norm(x, g, b):
    xf = x.astype(jnp.float32)
    mu = jnp.mean(xf, -1, keepdims=True)
    var = jnp.mean(jnp.square(xf - mu), -1, keepdims=True)
    return ((xf - mu) * lax.rsqrt(var + LN_EPS) * g.astype(jnp.float32) + b.astype(jnp.float32)).astype(x.dtype)


def gla_blocked(q, k, v, logf, s0):
    B, L = q.shape[0], q.shape[1]
    n_c = -(-L // CHUNK)
    pad = n_c * CHUNK - L

    def prep(t):
        t = jnp.pad(t.astype(jnp.float32), ((0, 0), (0, pad), (0, 0), (0, 0)))
        return t.reshape(B, n_c, CHUNK, GLA_HEADS, t.shape[-1]).transpose(1, 0, 3, 2, 4)

    qc, kc, vc, fc = prep(q), prep(k), prep(v), prep(logf)
    causal = jnp.tril(jnp.ones((CHUNK, CHUNK), dtype=bool))

    def step(S, inp):
        qb, kb, vb, fb = inp
        b = jnp.cumsum(fb, axis=2)
        b_last = b[:, :, -1:, :]
        q_t = qb * jnp.exp(b)
        k_t = kb * jnp.exp(-b)
        att = jnp.where(causal, jnp.einsum('bhik,bhjk->bhij', q_t, k_t), 0.0)
        o = jnp.einsum('bhik,bhkv->bhiv', q_t, S) + jnp.einsum('bhij,bhjv->bhiv', att, vb)
        k_s = kb * jnp.exp(b_last - b)
        S = jnp.exp(b_last[:, :, 0, :])[..., None] * S + jnp.einsum('bhjk,bhjv->bhkv', k_s, vb)
        return S, o

    S, o = lax.scan(step, s0.astype(jnp.float32), (qc, kc, vc, fc))
    o = o.transpose(1, 0, 3, 2, 4).reshape(B, n_c * CHUNK, GLA_HEADS, HEAD_V)[:, :L]
    return o, S


def token_mixer(x, conv_hist, gla_state, w_in, b_in, conv_w, conv_b, conv_ln_g, conv_ln_b,
                w_conv_out, b_conv_out, w_gate_up, b_gate, gla_norm_g, w_gla_out, w_o):
    B, L, _ = x.shape
    proj = x @ w_in + b_in
    u = proj[..., :OFF_GLU_G] * jax.nn.sigmoid(proj[..., OFF_GLU_G:OFF_Q])
    full = jnp.concatenate([conv_hist.astype(u.dtype), u], axis=1)
    new_hist = full[:, -(CONV_W - 1):]
    c = lax.conv_general_dilated(full, conv_w[:, None, :].astype(full.dtype), (1,), 'VALID',
                                 dimension_numbers=('NWC', 'WIO', 'NWC'),
                                 feature_group_count=D_CONV) + conv_b
    c = jax.nn.silu(layer_norm(c, conv_ln_g, conv_ln_b))
    y_a = c @ w_conv_out + b_conv_out
    q = proj[..., OFF_Q:OFF_K].reshape(B, L, GLA_HEADS, HEAD_K) * (HEAD_K ** -0.5)
    k = proj[..., OFF_K:OFF_V].reshape(B, L, GLA_HEADS, HEAD_K)
    v = proj[..., OFF_V:OFF_G].reshape(B, L, GLA_HEADS, HEAD_V)
    g_out = proj[..., OFF_G:OFF_LR]
    z = (proj[..., OFF_LR:OFF_GATES] @ w_gate_up + b_gate).astype(jnp.float32)
    logf = (jax.nn.log_sigmoid(z) / GATE_TAU).reshape(B, L, GLA_HEADS, HEAD_K)
    o, new_state = gla_blocked(q, k, v, logf, gla_state)
    o = o * lax.rsqrt(jnp.mean(o * o, -1, keepdims=True) + RMS_EPS) * gla_norm_g.astype(jnp.float32)
    o = o.reshape(B, L, GLA_DV).astype(x.dtype) * jax.nn.silu(g_out)
    y_b = o @ w_gla_out
    gates = jax.nn.sigmoid(proj[..., OFF_GATES:])
    merged = gates[..., :D_MODEL] * y_a + gates[..., D_MODEL:] * y_b
    return merged @ w_o, new_hist, new_state.astype(gla_state.dtype)


def swiglu(x, wg, wu, wd):
    return (jax.nn.silu(x @ wg) * (x @ wu)) @ wd


def moe_swiglu(x, w_router, wg, wu, wd):
    B, L, D = x.shape
    t = x.reshape(B * L, D)
    probs = jax.nn.softmax((t @ w_router).astype(jnp.float32), axis=-1)
    top_p, top_i = lax.top_k(probs, TOP_K)
    top_p = top_p / jnp.sum(top_p, -1, keepdims=True)
    comb = jnp.sum(jax.nn.one_hot(top_i, N_EXPERTS, dtype=jnp.float32) * top_p[..., None], axis=1)
    out = jnp.zeros(t.shape, jnp.float32)
    for e in range(N_EXPERTS):
        out = out + comb[:, e:e + 1] * swiglu(t, wg[e], wu[e], wd[e]).astype(jnp.float32)
    return out.astype(x.dtype).reshape(B, L, D)


def trunk(x, conv_hist, gla_state, p):
    (w_in, b_in, conv_w, conv_b, conv_ln_g, conv_ln_b, w_conv_out, b_conv_out, w_gate_up, b_gate,
     gla_norm_g, w_gla_out, w_o, ln1_g, ln1_b, ln2_g, ln2_b, ff_w_gate, ff_w_up, ff_w_down,
     w_router, moe_w_gate, moe_w_up, moe_w_down) = p
    hists, states = [], []
    for l in range(DEPTH):
        m, h, s = token_mixer(x, conv_hist[l], gla_state[l], w_in[l], b_in[l], conv_w[l], conv_b[l],
                              conv_ln_g[l], conv_ln_b[l], w_conv_out[l], b_conv_out[l], w_gate_up[l],
                              b_gate[l], gla_norm_g[l], w_gla_out[l], w_o[l])
        x = layer_norm(ALPHA * x + m, ln1_g[l], ln1_b[l])
        if l % 2 == 0:
            f = swiglu(x, ff_w_gate[l // 2], ff_w_up[l // 2], ff_w_down[l // 2])
        else:
            f = moe_swiglu(x, w_router[l // 2], moe_w_gate[l // 2], moe_w_up[l // 2], moe_w_down[l // 2])
        x = layer_norm(ALPHA * x + f, ln2_g[l], ln2_b[l])
        hists.append(h)
        states.append(s)
    return x, jnp.stack(hists), jnp.stack(states)


def setup_inputs(seed: int = 0) -> dict:
    key = jax.random.key(seed)
    ks = jax.random.split(key, 32)
    n = lambda i, shape, s: jax.random.normal(ks[i], shape, jnp.float32) * s
    f32 = jnp.float32
    return {
        "x_prompt": n(0, (BATCH, SEQ, D_MODEL), 1.0),
        "x_sample": n(1, (DEC_BATCH, DEC_SEQ, D_MODEL), 1.0),
        "cache_conv": n(2, (DEPTH, DEC_BATCH, CONV_W - 1, D_CONV), 0.5),
        "state_gla": n(3, (DEPTH, DEC_BATCH, GLA_HEADS, HEAD_K, HEAD_V), 0.5),
        "w_in": n(4, (DEPTH, D_MODEL, D_IN), D_MODEL ** -0.5),
        "b_in": n(5, (DEPTH, D_IN), 0.02),
        "conv_w": n(6, (DEPTH, CONV_W, D_CONV), CONV_W ** -0.5),
        "conv_b": n(7, (DEPTH, D_CONV), 0.02),
        "conv_ln_g": jnp.ones((DEPTH, D_CONV), f32) + n(8, (DEPTH, D_CONV), 0.02),
        "conv_ln_b": n(9, (DEPTH, D_CONV), 0.02),
        "w_conv_out": n(10, (DEPTH, D_CONV, D_MODEL), D_CONV ** -0.5),
        "b_conv_out": n(11, (DEPTH, D_MODEL), 0.02),
        "w_gate_up": n(12, (DEPTH, GATE_RANK, GLA_DK), GATE_RANK ** -0.5),
        "b_gate": n(13, (DEPTH, GLA_DK), 0.1),
        "gla_norm_g": jnp.ones((DEPTH, HEAD_V), f32) + n(14, (DEPTH, HEAD_V), 0.02),
        "w_gla_out": n(15, (DEPTH, GLA_DV, D_MODEL), GLA_DV ** -0.5),
        "w_o": n(16, (DEPTH, D_MODEL, D_MODEL), BETA * D_MODEL ** -0.5),
        "ln1_g": jnp.ones((DEPTH, D_MODEL), f32) + n(17, (DEPTH, D_MODEL), 0.02),
        "ln1_b": n(18, (DEPTH, D_MODEL), 0.02),
        "ln2_g": jnp.ones((DEPTH, D_MODEL), f32) + n(19, (DEPTH, D_MODEL), 0.02),
        "ln2_b": n(20, (DEPTH, D_MODEL), 0.02),
        "ff_w_gate": n(21, (N_DENSE, D_MODEL, D_FF), D_MODEL ** -0.5),
        "ff_w_up": n(22, (N_DENSE, D_MODEL, D_FF), D_MODEL ** -0.5),
        "ff_w_down": n(23, (N_DENSE, D_FF, D_MODEL), BETA * D_FF ** -0.5),
        "w_router": n(24, (N_MOE, D_MODEL, N_EXPERTS), D_MODEL ** -0.5),
        "moe_w_gate": n(25, (N_MOE, N_EXPERTS, D_MODEL, D_FF), D_MODEL ** -0.5),
        "moe_w_up": n(26, (N_MOE, N_EXPERTS, D_MODEL, D_FF), D_MODEL ** -0.5),
        "moe_w_down": n(27, (N_MOE, N_EXPERTS, D_FF, D_MODEL), BETA * D_FF ** -0.5),
    }


def reference(x_prompt, x_sample, cache_conv, state_gla, w_in, b_in, conv_w, conv_b, conv_ln_g,
              conv_ln_b, w_conv_out, b_conv_out, w_gate_up, b_gate, gla_norm_g, w_gla_out, w_o,
              ln1_g, ln1_b, ln2_g, ln2_b, ff_w_gate, ff_w_up, ff_w_down, w_router, moe_w_gate,
              moe_w_up, moe_w_down):
    p = (w_in, b_in, conv_w, conv_b, conv_ln_g, conv_ln_b, w_conv_out, b_conv_out, w_gate_up, b_gate,
         gla_norm_g, w_gla_out, w_o, ln1_g, ln1_b, ln2_g, ln2_b, ff_w_gate, ff_w_up, ff_w_down,
         w_router, moe_w_gate, moe_w_up, moe_w_down)
    b_p = x_prompt.shape[0]
    hist0 = jnp.zeros((DEPTH, b_p, CONV_W - 1, D_CONV), x_prompt.dtype)
    state0 = jnp.zeros((DEPTH, b_p, GLA_HEADS, HEAD_K, HEAD_V), state_gla.dtype)
    y_prompt, conv_prompt, gla_prompt = trunk(x_prompt, hist0, state0, p)
    y_sample, conv_sample, gla_sample = trunk(x_sample, cache_conv, state_gla, p)
    return (y_prompt, y_sample, conv_prompt, gla_prompt, conv_sample, gla_sample)
```

```python
import functools

import jax
import jax.numpy as jnp
from jax import lax
from jax.experimental import pallas as pl
from jax.experimental.pallas import tpu as pltpu

CHUNK = 64
CONV_W = 31
GLA_HEADS = 4
GATE_TAU = 16.0
LN_EPS = 1e-5
RMS_EPS = 1e-6
TOP_K = 2

HIST_PAD = 32
CONV_ROWS = 32
VMEM_LIMIT = 56 * 1024 * 1024

BF16 = jnp.bfloat16
F32 = jnp.float32


def _mm(a, b):
    return jnp.dot(a, b, preferred_element_type=F32)


def _sigmoid(x):
    return 1.0 / (1.0 + jnp.exp(-x))


def _silu(x):
    return x * _sigmoid(x)


def _layer_norm(x, g, b):
    mu = jnp.mean(x, axis=-1, keepdims=True)
    xc = x - mu
    var = jnp.mean(xc * xc, axis=-1, keepdims=True)
    return xc * lax.rsqrt(var + LN_EPS) * g + b


def _split3(x):
    hi = x.astype(BF16)
    r1 = x - hi.astype(F32)
    mid = r1.astype(BF16)
    lo = (r1 - mid.astype(F32)).astype(BF16)
    return hi, mid, lo


def _full_spec(shape):
    zeros = (0,) * len(shape)
    return pl.BlockSpec(shape, lambda *_: zeros, pipeline_mode=pl.Buffered(1))


def _project(xb, w, dims, u_store, q_s, k_s, v_s, lf_s):
    dc, dk, dv = dims["dc"], dims["dk"], dims["dv"]
    off_q, off_k, off_v = 2 * dc, 2 * dc + dk, 2 * dc + 2 * dk
    off_g = off_v + dv
    glu = _mm(xb, w["wmain"][:, 0:off_q]) + w["bmain"][:, 0:off_q]
    u_store(glu[:, 0:dc] * _sigmoid(glu[:, dc:off_q]))
    q_s[...] = (_mm(xb, w["wmain"][:, off_q:off_k]) + w["bmain"][:, off_q:off_k]) * (dims["hk"] ** -0.5)
    k_s[...] = _mm(xb, w["wmain"][:, off_k:off_v]) + w["bmain"][:, off_k:off_v]
    v_s[...] = _mm(xb, w["wmain"][:, off_v:off_g]) + w["bmain"][:, off_v:off_g]
    lr = _mm(xb, w["wlr"][...]) + w["blr"][...]
    z = _mm(lr.astype(BF16), w["wgu"][...]) + w["bgu"][...]
    lf_s[...] = -(jnp.maximum(-z, 0.0) + jnp.log1p(jnp.exp(-jnp.abs(z)))) * (1.0 / GATE_TAU)


def _gla_chunk(q_s, k_s, v_s, lf_s, o_s, rows, c, get_state, set_state, dims):
    hk, hv, dk = dims["hk"], dims["hv"], dims["dk"]
    lf = lf_s[rows, :]
    ri = lax.broadcasted_iota(jnp.int32, (c, c), 0)
    ci = lax.broadcasted_iota(jnp.int32, (c, c), 1)
    causal = ri >= ci
    tri = causal.astype(BF16)
    hi, mid, lo = _split3(lf)
    b = _mm(tri, hi) + _mm(tri, mid) + _mm(tri, lo)
    b_last = b[c - 1:c, :]
    q_t = q_s[rows, :] * jnp.exp(b)
    kk = k_s[rows, :]
    k_t = kk * jnp.exp(-b)
    k_e = kk * jnp.exp(b_last - b)
    d_t = jnp.transpose(jnp.broadcast_to(jnp.exp(b_last), (128, dk)))
    vv = v_s[rows, :]
    for h in range(GLA_HEADS):
        ks = slice(h * hk, (h + 1) * hk)
        vs = slice(h * hv, (h + 1) * hv)
        qh = q_t[:, ks].astype(BF16)
        kh = k_t[:, ks].astype(BF16)
        keh = k_e[:, ks].astype(BF16)
        vh = vv[:, vs].astype(BF16)
        att = lax.dot_general(qh, kh, (((1,), (1,)), ((), ())), preferred_element_type=F32)
        att = jnp.where(causal, att, 0.0)
        s_old = get_state(h)
        o_s[rows, vs] = _mm(qh, s_old.astype(BF16)) + _mm(att.astype(BF16), vh)
        upd = lax.dot_general(keh, vh, (((0,), (0,)), ((), ())), preferred_element_type=F32)
        dcol = d_t[ks, :]
        decay = jnp.concatenate([dcol] * (hv // 128), axis=1)
        set_state(h, decay * s_old + upd)


def _conv_block(win, convw_ref, n):
    wn = n + HIST_PAD
    base = HIST_PAD - (CONV_W - 1)
    acc = None
    for b in range(8):
        wb = win if b == 0 else pltpu.roll(win, wn - b, axis=0)
        for a in range((base + CONV_W + 7) // 8):
            j = 8 * a + b - base
            if 0 <= j < CONV_W:
                term = convw_ref[j:j + 1, :] * wb[8 * a:8 * a + n, :]
                acc = term if acc is None else acc + term
    return acc


def _tail(x, xb, c, w, dims, o_s):
    d, dc, dk, dv, hv = dims["d"], dims["dc"], dims["dk"], dims["dv"], dims["hv"]
    off_g = 2 * dc + 2 * dk + dv
    off_lr = off_g + dv
    c = _silu(_layer_norm(c + w["convb"][...], w["clng"][...], w["clnb"][...]))
    y_a = _mm(c.astype(BF16), w["wco"][...]) + w["bco"][...]
    g_out = _mm(xb, w["wmain"][:, off_g:off_lr]) + w["bmain"][:, off_g:off_lr]
    heads = []
    for h in range(GLA_HEADS):
        oh = o_s[:, h * hv:(h + 1) * hv]
        ms = jnp.mean(oh * oh, axis=-1, keepdims=True)
        heads.append(oh * lax.rsqrt(ms + RMS_EPS) * w["gng"][...])
    o = jnp.concatenate(heads, axis=1) * _silu(g_out)
    y_b = _mm(o.astype(BF16), w["wgo"][...])
    gates = _sigmoid(_mm(xb, w["wgates"][...]) + w["bgates"][...])
    merged = gates[:, 0:d] * y_a + gates[:, d:2 * d] * y_b
    m = _mm(merged.astype(BF16), w["wo"][...])
    return _layer_norm(dims["alpha"] * x + m, w["ln1g"][...], w["ln1b"][...])


_W_NAMES = ("wmain", "bmain", "wlr", "blr", "wgates", "bgates", "convw", "convb", "clng", "clnb",
            "wco", "bco", "wgu", "bgu", "gng", "wgo", "wo", "ln1g", "ln1b")


def _mixer_prompt_kernel(*refs, dims, tl):
    nw = len(_W_NAMES)
    x_ref = refs[0]
    w = dict(zip(_W_NAMES, refs[1:1 + nw]))
    x1_ref, hist_ref, state_ref = refs[1 + nw:4 + nw]
    ubuf, q_s, k_s, v_s, lf_s, o_s, c_s, s_s = refs[4 + nw:]
    j = pl.program_id(1)
    dc = dims["dc"]

    @pl.when(j == 0)
    def _():
        ubuf[0:HIST_PAD, :] = jnp.zeros((HIST_PAD, dc), F32)
        s_s[...] = jnp.zeros(s_s.shape, F32)

    x = x_ref[0]
    xb = x.astype(BF16)

    def u_store(u):
        ubuf[HIST_PAD:HIST_PAD + tl, :] = u

    _project(xb, w, dims, u_store, q_s, k_s, v_s, lf_s)

    def chunk_body(ci, carry):
        rows = pl.ds(pl.multiple_of(ci * CHUNK, CHUNK), CHUNK)

        def set_state(h, val):
            s_s[h] = val

        _gla_chunk(q_s, k_s, v_s, lf_s, o_s, rows, CHUNK, lambda h: s_s[h], set_state, dims)
        return carry

    lax.fori_loop(0, tl // CHUNK, chunk_body, 0)

    def conv_body(bi, carry):
        r0 = pl.multiple_of(bi * CONV_ROWS, CONV_ROWS)
        win = ubuf[pl.ds(r0, CONV_ROWS + HIST_PAD), :]
        c_s[pl.ds(r0, CONV_ROWS), :] = _conv_block(win, w["convw"], CONV_ROWS)
        return carry

    lax.fori_loop(0, tl // CONV_ROWS, conv_body, 0)

    x1_ref[0] = _tail(x, xb, c_s[...], w, dims, o_s)

    @pl.when(j == pl.num_programs(1) - 1)
    def _():
        hist_ref[0] = ubuf[tl + HIST_PAD - (CONV_W - 1):tl + HIST_PAD, :]
        state_ref[0] = s_s[...]

    ubuf[0:HIST_PAD, :] = ubuf[tl:tl + HIST_PAD, :]


def _mixer_sample_kernel(*refs, dims, ns, ls):
    nw = len(_W_NAMES)
    x_ref, hist_in_ref, state_in_ref = refs[0:3]
    w = dict(zip(_W_NAMES, refs[3:3 + nw]))
    x1_ref, hist_ref, state_ref = refs[3 + nw:6 + nw]
    ubuf, q_s, k_s, v_s, lf_s, o_s, c_s = refs[6 + nw:]
    i = pl.program_id(0)
    dc = dims["dc"]
    hl = CONV_W - 1

    @pl.when(i == 0)
    def _():
        xb = x_ref[...].astype(BF16)
        ubuf[:, 0:8, :] = jnp.zeros((ns, 8, dc), F32)
        ubuf[:, HIST_PAD - hl:HIST_PAD, :] = hist_in_ref[...]

        def u_store(u):
            ubuf[:, HIST_PAD:HIST_PAD + ls, :] = u.reshape(ns, ls, dc)

        _project(xb, w, dims, u_store, q_s, k_s, v_s, lf_s)

    rows = pl.ds(pl.multiple_of(i * ls, ls), ls)

    def set_state(h, val):
        state_ref[0, h] = val

    _gla_chunk(q_s, k_s, v_s, lf_s, o_s, rows, ls, lambda h: state_in_ref[0, h], set_state, dims)
    win = ubuf[i]
    c_s[rows, :] = _conv_block(win, w["convw"], ls)
    hist_ref[i] = win[HIST_PAD + ls - hl:HIST_PAD + ls, :]

    @pl.when(i == ns - 1)
    def _():
        x = x_ref[...]
        x1_ref[...] = _tail(x, x.astype(BF16), c_s[...], w, dims, o_s)


def _mixer_weights(l, p, dims):
    dc, dk, dv, rank = dims["dc"], dims["dk"], dims["dv"], dims["rank"]
    off_lr = 2 * dc + 2 * dk + 2 * dv
    off_gates = off_lr + rank
    w_in, b_in = p["w_in"][l], p["b_in"][l]
    row = lambda v: v.reshape(1, -1).astype(F32)
    return (
        w_in[:, :off_lr].astype(BF16), row(b_in[:off_lr]),
        w_in[:, off_lr:off_gates].astype(BF16), row(b_in[off_lr:off_gates]),
        w_in[:, off_gates:].astype(BF16), row(b_in[off_gates:]),
        p["conv_w"][l].astype(F32), row(p["conv_b"][l]), row(p["conv_ln_g"][l]), row(p["conv_ln_b"][l]),
        p["w_conv_out"][l].astype(BF16), row(p["b_conv_out"][l]),
        p["w_gate_up"][l].astype(BF16), row(p["b_gate"][l]),
        row(p["gla_norm_g"][l]),
        p["w_gla_out"][l].astype(BF16), p["w_o"][l].astype(BF16),
        row(p["ln1_g"][l]), row(p["ln1_b"][l]),
    )


def _mixer_prompt(x, wts, dims, tl):
    bsz, seq, d = x.shape
    dc, dk, dv, hk, hv = dims["dc"], dims["dk"], dims["dv"], dims["hk"], dims["hv"]
    hl = CONV_W - 1
    kern = functools.partial(_mixer_prompt_kernel, dims=dims, tl=tl)
    return pl.pallas_call(
        kern,
        grid=(bsz, seq // tl),
        in_specs=[pl.BlockSpec((1, tl, d), lambda b, j: (b, j, 0))] + [_full_spec(a.shape) for a in wts],
        out_specs=[
            pl.BlockSpec((1, tl, d), lambda b, j: (b, j, 0)),
            pl.BlockSpec((1, hl, dc), lambda b, j: (b, 0, 0)),
            pl.BlockSpec((1, GLA_HEADS, hk, hv), lambda b, j: (b, 0, 0, 0)),
        ],
        out_shape=[
            jax.ShapeDtypeStruct((bsz, seq, d), F32),
            jax.ShapeDtypeStruct((bsz, hl, dc), F32),
            jax.ShapeDtypeStruct((bsz, GLA_HEADS, hk, hv), F32),
        ],
        scratch_shapes=[
            pltpu.VMEM((HIST_PAD + tl, dc), F32),
            pltpu.VMEM((tl, dk), F32), pltpu.VMEM((tl, dk), F32), pltpu.VMEM((tl, dv), F32),
            pltpu.VMEM((tl, dk), F32), pltpu.VMEM((tl, dv), F32), pltpu.VMEM((tl, dc), F32),
            pltpu.VMEM((GLA_HEADS, hk, hv), F32),
        ],
        compiler_params=pltpu.CompilerParams(
            dimension_semantics=("arbitrary", "arbitrary"), vmem_limit_bytes=VMEM_LIMIT),
        name="mixer_prompt",
    )(x, *wts)


def _mixer_sample(x, hist, state, wts, dims):
    ns, ls, d = x.shape
    dc, dk, dv, hk, hv = dims["dc"], dims["dk"], dims["dv"], dims["hk"], dims["hv"]
    hl = CONV_W - 1
    t = ns * ls
    kern = functools.partial(_mixer_sample_kernel, dims=dims, ns=ns, ls=ls)
    x1, hist_o, state_o = pl.pallas_call(
        kern,
        grid=(ns,),
        in_specs=[
            pl.BlockSpec((t, d), lambda i: (0, 0)),
            pl.BlockSpec((ns, hl, dc), lambda i: (0, 0, 0)),
            pl.BlockSpec((1, GLA_HEADS, hk, hv), lambda i: (i, 0, 0, 0)),
        ] + [_full_spec(a.shape) for a in wts],
        out_specs=[
            pl.BlockSpec((t, d), lambda i: (0, 0)),
            pl.BlockSpec((ns, hl, dc), lambda i: (0, 0, 0)),
            pl.BlockSpec((1, GLA_HEADS, hk, hv), lambda i: (i, 0, 0, 0)),
        ],
        out_shape=[
            jax.ShapeDtypeStruct((t, d), F32),
            jax.ShapeDtypeStruct((ns, hl, dc), F32),
            jax.ShapeDtypeStruct((ns, GLA_HEADS, hk, hv), F32),
        ],
        scratch_shapes=[
            pltpu.VMEM((ns, HIST_PAD + ls, dc), F32),
            pltpu.VMEM((t, dk), F32), pltpu.VMEM((t, dk), F32), pltpu.VMEM((t, dv), F32),
            pltpu.VMEM((t, dk), F32), pltpu.VMEM((t, dv), F32), pltpu.VMEM((t, dc), F32),
        ],
        compiler_params=pltpu.CompilerParams(
            dimension_semantics=("arbitrary",), vmem_limit_bytes=VMEM_LIMIT),
        name="mixer_sample",
    )(x.reshape(t, d), hist, state, *wts)
    return x1, hist_o, state_o


def _ffn_dense_kernel(x_ref, wg_ref, wu_ref, wd_ref, g_ref, b_ref, o_ref, *, alpha):
    x = x_ref[...]
    xb = x.astype(BF16)
    h = _silu(_mm(xb, wg_ref[...])) * _mm(xb, wu_ref[...])
    f = _mm(h.astype(BF16), wd_ref[...])
    o_ref[...] = _layer_norm(alpha * x + f, g_ref[...], b_ref[...])


def _ffn_dense(x, wg, wu, wd, g, b, alpha, tm):
    t, d = x.shape
    ops = (wg.astype(BF16), wu.astype(BF16), wd.astype(BF16), g.reshape(1, d), b.reshape(1, d))
    return pl.pallas_call(
        functools.partial(_ffn_dense_kernel, alpha=alpha),
        grid=(t // tm,),
        in_specs=[pl.BlockSpec((tm, d), lambda i: (i, 0))] + [_full_spec(a.shape) for a in ops],
        out_specs=pl.BlockSpec((tm, d), lambda i: (i, 0)),
        out_shape=jax.ShapeDtypeStruct((t, d), F32),
        compiler_params=pltpu.CompilerParams(
            dimension_semantics=("arbitrary",), vmem_limit_bytes=VMEM_LIMIT),
        name="ffn_dense",
    )(x, *ops)


def _router_kernel(x_ref, wr_ref, route_ref, cnt_ref, carry, *, ne, tr):
    i = pl.program_id(0)

    @pl.when(i == 0)
    def _():
        carry[...] = jnp.zeros(carry.shape, F32)

    xh = x_ref[...]
    x_hi = xh.astype(BF16)
    x_lo = (xh - x_hi.astype(F32)).astype(BF16)
    wr = wr_ref[...]
    w_hi = wr.astype(BF16)
    w_lo = (wr - w_hi.astype(F32)).astype(BF16)
    nt = (((1,), (1,)), ((), ()))
    logits = (lax.dot_general(w_hi, x_hi, nt, preferred_element_type=F32)
              + lax.dot_general(w_hi, x_lo, nt, preferred_element_type=F32)
              + lax.dot_general(w_lo, x_hi, nt, preferred_element_type=F32))
    mx = jnp.max(logits, axis=0, keepdims=True)
    ex = jnp.exp(logits - mx)
    probs = ex / jnp.sum(ex, axis=0, keepdims=True)
    eid = lax.broadcasted_iota(jnp.int32, (ne, tr), 0)
    p1 = jnp.max(probs, axis=0, keepdims=True)
    i1 = jnp.min(jnp.where(probs == p1, eid, ne), axis=0, keepdims=True)
    rest = jnp.where(eid == i1, -1.0, probs)
    p2 = jnp.max(rest, axis=0, keepdims=True)
    i2 = jnp.min(jnp.where(rest == p2, eid, ne), axis=0, keepdims=True)
    den = p1 + p2
    oh1 = (eid == i1).astype(F32)
    oh2 = (eid == i2).astype(F32)
    oh = oh1 + oh2
    ri = lax.broadcasted_iota(jnp.int32, (tr, tr), 0)
    ci = lax.broadcasted_iota(jnp.int32, (tr, tr), 1)
    upper = (ri <= ci).astype(BF16)
    incl = _mm(oh.astype(BF16), upper)
    before = carry[:, 0:1] + incl - oh
    r1 = jnp.sum(oh1 * before, axis=0, keepdims=True)
    r2 = jnp.sum(oh2 * before, axis=0, keepdims=True)
    zero = jnp.zeros((1, tr), F32)
    route_ref[...] = jnp.concatenate(
        [i1.astype(F32), i2.astype(F32), p1 / den, p2 / den, r1, r2, zero, zero], axis=0)
    total = carry[:, 0:1] + incl[:, tr - 1:tr]
    carry[...] = jnp.broadcast_to(total, carry.shape)
    cnt_ref[...] = jnp.broadcast_to(total, cnt_ref.shape)


def _router(x, w_router, tr):
    t, d = x.shape
    ne = w_router.shape[1]
    return pl.pallas_call(
        functools.partial(_router_kernel, ne=ne, tr=tr),
        grid=(t // tr,),
        in_specs=[pl.BlockSpec((tr, d), lambda i: (i, 0)), _full_spec((ne, d))],
        out_specs=[pl.BlockSpec((8, tr), lambda i: (0, i)), pl.BlockSpec((ne, 128), lambda i: (0, 0))],
        out_shape=[jax.ShapeDtypeStruct((8, t), F32), jax.ShapeDtypeStruct((ne, 128), F32)],
        scratch_shapes=[pltpu.VMEM((ne, 128), F32)],
        compiler_params=pltpu.CompilerParams(dimension_semantics=("arbitrary",)),
        name="moe_router",
    )(x, w_router.T.astype(F32))


def _dispatch_kernel(pos_ref, x_ref, zeros_ref, xs_ref, sem, *, tm):
    del zeros_ref

    def row_copy(r, k):
        return pltpu.make_async_copy(x_ref.at[pl.ds(r, 1), :], xs_ref.at[pl.ds(pos_ref[0, 0, 2 * r + k], 1), :], sem)

    def start(r, carry):
        row_copy(r, 0).start()
        row_copy(r, 1).start()
        return carry

    def wait(r, carry):
        row_copy(r, 0).wait()
        row_copy(r, 1).wait()
        return carry

    lax.fori_loop(0, tm, start, 0)
    lax.fori_loop(0, tm, wait, 0)


def _dispatch(x, pos, n_rows, tm):
    t, d = x.shape
    zeros = jnp.zeros((n_rows, d), F32)
    return pl.pallas_call(
        functools.partial(_dispatch_kernel, tm=tm),
        grid=(t // tm,),
        in_specs=[
            pl.BlockSpec((1, 1, 2 * tm), lambda i: (i, 0, 0), memory_space=pltpu.SMEM),
            pl.BlockSpec((tm, d), lambda i: (i, 0)),
            pl.BlockSpec(memory_space=pl.ANY),
        ],
        out_specs=pl.BlockSpec(memory_space=pl.ANY),
        out_shape=jax.ShapeDtypeStruct((n_rows, d), F32),
        scratch_shapes=[pltpu.SemaphoreType.DMA],
        input_output_aliases={2: 0},
        compiler_params=pltpu.CompilerParams(dimension_semantics=("arbitrary",), has_side_effects=True),
        name="moe_dispatch",
    )(pos.reshape(t // tm, 1, 2 * tm), x, zeros)


def _experts_kernel(te_ref, nv_ref, xs_ref, wg_ref, wu_ref, wd_ref, ys_ref):
    i = pl.program_id(0)

    @pl.when(i < nv_ref[0])
    def _():
        xb = xs_ref[...].astype(BF16)
        h = _silu(_mm(xb, wg_ref[0])) * _mm(xb, wu_ref[0])
        ys_ref[...] = _mm(h.astype(BF16), wd_ref[0])

    @pl.when(i >= nv_ref[0])
    def _():
        ys_ref[...] = jnp.zeros(ys_ref.shape, F32)


def _experts(xs, tile_expert, n_valid, wg, wu, wd, tm):
    n_rows, d = xs.shape
    ne, _, ff = wg.shape
    grid_spec = pltpu.PrefetchScalarGridSpec(
        num_scalar_prefetch=2,
        grid=(n_rows // tm,),
        in_specs=[
            pl.BlockSpec((tm, d), lambda i, te, nv: (i, 0)),
            pl.BlockSpec((1, d, ff), lambda i, te, nv: (te[i], 0, 0)),
            pl.BlockSpec((1, d, ff), lambda i, te, nv: (te[i], 0, 0)),
            pl.BlockSpec((1, ff, d), lambda i, te, nv: (te[i], 0, 0)),
        ],
        out_specs=pl.BlockSpec((tm, d), lambda i, te, nv: (i, 0)),
    )
    return pl.pallas_call(
        _experts_kernel,
        grid_spec=grid_spec,
        out_shape=jax.ShapeDtypeStruct((n_rows, d), F32),
        compiler_params=pltpu.CompilerParams(
            dimension_semantics=("arbitrary",), vmem_limit_bytes=VMEM_LIMIT),
        name="moe_experts",
    )(tile_expert, n_valid, xs, wg, wu, wd)


def _combine_kernel(pos_ref, x_ref, rt_ref, ys_ref, g_ref, b_ref, o_ref, buf, sem, *, tm, alpha):
    def row_copy(r, k):
        return pltpu.make_async_copy(ys_ref.at[pl.ds(pos_ref[0, 0, 2 * r + k], 1), :], buf.at[k, pl.ds(r, 1), :], sem)

    def start(r, carry):
        row_copy(r, 0).start()
        row_copy(r, 1).start()
        return carry

    def wait(r, carry):
        row_copy(r, 0).wait()
        row_copy(r, 1).wait()
        return carry

    lax.fori_loop(0, tm, start, 0)
    lax.fori_loop(0, tm, wait, 0)
    rt = rt_ref[...]
    f = rt[:, 2:3] * buf[0] + rt[:, 3:4] * buf[1]
    o_ref[...] = _layer_norm(alpha * x_ref[...] + f, g_ref[...], b_ref[...])


def _combine(x, pos, route_t, ys, g, b, alpha, tm):
    t, d = x.shape
    return pl.pallas_call(
        functools.partial(_combine_kernel, tm=tm, alpha=alpha),
        grid=(t // tm,),
        in_specs=[
            pl.BlockSpec((1, 1, 2 * tm), lambda i: (i, 0, 0), memory_space=pltpu.SMEM),
            pl.BlockSpec((tm, d), lambda i: (i, 0)),
            pl.BlockSpec((tm, 8), lambda i: (i, 0)),
            pl.BlockSpec(memory_space=pl.ANY),
            _full_spec((1, d)), _full_spec((1, d)),
        ],
        out_specs=pl.BlockSpec((tm, d), lambda i: (i, 0)),
        out_shape=jax.ShapeDtypeStruct((t, d), F32),
        scratch_shapes=[pltpu.VMEM((2, tm, d), F32), pltpu.SemaphoreType.DMA],
        compiler_params=pltpu.CompilerParams(dimension_semantics=("arbitrary",)),
        name="moe_combine",
    )(pos.reshape(t // tm, 1, 2 * tm), x, route_t, ys, g.reshape(1, d), b.reshape(1, d))


def _ffn_moe(x, w_router, wg, wu, wd, g, b, alpha, tr, tm):
    t, d = x.shape
    ne = w_router.shape[1]
    route, counts = _router(x, w_router, tr)
    cnt = counts[:, 0].astype(jnp.int32)
    gsz = ((cnt + tm - 1) // tm) * tm
    ends = jnp.cumsum(gsz)
    offs = ends - gsz
    n_tiles = (TOP_K * t) // tm + ne
    n_rows = n_tiles * tm
    tile_start = jnp.arange(n_tiles, dtype=jnp.int32) * tm
    tile_e = jnp.sum((tile_start[:, None] >= ends[None, :]).astype(jnp.int32), axis=1)
    n_valid = (ends[ne - 1] // tm).astype(jnp.int32).reshape(1)
    last_e = jnp.sum((ends[ne - 1] - 1 >= ends).astype(jnp.int32))
    tile_e = jnp.minimum(tile_e, last_e).astype(jnp.int32)
    i12 = route[0:2].astype(jnp.int32)
    pos = (jnp.take(offs, i12) + route[4:6].astype(jnp.int32)).T
    pos = pos.reshape(-1)
    xs = _dispatch(x, pos, n_rows, tr)
    ys = _experts(xs, tile_e, n_valid, wg.astype(BF16), wu.astype(BF16), wd.astype(BF16), tm)
    return _combine(x, pos, route.T, ys, g, b, alpha, tr)


def kernel(x_prompt, x_sample, cache_conv, state_gla, w_in, b_in, conv_w, conv_b, conv_ln_g, conv_ln_b, w_conv_out, b_conv_out, w_gate_up, b_gate, gla_norm_g, w_gla_out, w_o, ln1_g, ln1_b, ln2_g, ln2_b, ff_w_gate, ff_w_up, ff_w_down, w_router, moe_w_gate, moe_w_up, moe_w_down):
    p = dict(w_in=w_in, b_in=b_in, conv_w=conv_w, conv_b=conv_b, conv_ln_g=conv_ln_g, conv_ln_b=conv_ln_b,
             w_conv_out=w_conv_out, b_conv_out=b_conv_out, w_gate_up=w_gate_up, b_gate=b_gate,
             gla_norm_g=gla_norm_g, w_gla_out=w_gla_out, w_o=w_o, ln1_g=ln1_g, ln1_b=ln1_b)
    depth = w_in.shape[0]
    bsz, seq, d = x_prompt.shape
    ns, ls, _ = x_sample.shape
    dc = conv_w.shape[-1]
    rank, dk = w_gate_up.shape[1], w_gate_up.shape[2]
    dv = w_gla_out.shape[1]
    dims = dict(d=d, dc=dc, dk=dk, dv=dv, rank=rank, hk=dk // GLA_HEADS, hv=dv // GLA_HEADS,
                alpha=(2.0 * depth) ** 0.25)
    alpha = dims["alpha"]
    tl = min(512, seq)
    tp = bsz * seq
    ts = ns * ls
    tm = min(512, ts)

    xp = x_prompt
    xs = x_sample
    hist_p, state_p, hist_s, state_s = [], [], [], []
    for l in range(depth):
        wts = _mixer_weights(l, p, dims)
        x1p, hp, sp = _mixer_prompt(xp, wts, dims, tl)
        x1s, hs, ss = _mixer_sample(xs, cache_conv[l], state_gla[l], wts, dims)
        hist_p.append(hp), state_p.append(sp), hist_s.append(hs), state_s.append(ss)
        x1 = jnp.concatenate([x1p.reshape(tp, d), x1s], axis=0)
        if l % 2 == 0:
            x2 = _ffn_dense(x1, ff_w_gate[l // 2], ff_w_up[l // 2], ff_w_down[l // 2], ln2_g[l], ln2_b[l], alpha, tm)
        else:
            x2 = _ffn_moe(x1, w_router[l // 2], moe_w_gate[l // 2], moe_w_up[l // 2], moe_w_down[l // 2],
                          ln2_g[l], ln2_b[l], alpha, tm, tm)
        xp = x2[:tp].reshape(bsz, seq, d)
        xs = x2[tp:].reshape(ns, ls, d)
    return (xp, xs, jnp.stack(hist_p), jnp.stack(state_p).astype(state_gla.dtype),
            jnp.stack(hist_s), jnp.stack(state_s).astype(state_gla.dtype))
```

```python
import functools

import jax
import jax.numpy as jnp
from jax import lax
from jax.experimental import pallas as pl
from jax.experimental.pallas import tpu as pltpu

CHUNK = 64
CONV_W = 31
GLA_HEADS = 4
GATE_TAU = 16.0
LN_EPS = 1e-5
RMS_EPS = 1e-6
TOP_K = 2

HIST_PAD = 32
CONV_ROWS = 32
VMEM_LIMIT = 56 * 1024 * 1024

BF16 = jnp.bfloat16
F32 = jnp.float32


def _mm(a, b):
    return jnp.dot(a, b, preferred_element_type=F32)


def _sigmoid(x):
    return 0.5 * jnp.tanh(0.5 * x) + 0.5


def _silu(x):
    return x * _sigmoid(x)


def _layer_norm(x, g, b):
    mu = jnp.mean(x, axis=-1, keepdims=True)
    xc = x - mu
    var = jnp.mean(xc * xc, axis=-1, keepdims=True)
    return xc * lax.rsqrt(var + LN_EPS) * g + b


def _split3(x):
    hi = x.astype(BF16)
    r1 = x - hi.astype(F32)
    mid = r1.astype(BF16)
    lo = (r1 - mid.astype(F32)).astype(BF16)
    return hi, mid, lo


def _full_spec(shape):
    zeros = (0,) * len(shape)
    return pl.BlockSpec(shape, lambda *_: zeros, pipeline_mode=pl.Buffered(1))


def _project(xb, w, dims, u_store, q_s, k_s, v_s, lf_s):
    dc, dk, dv = dims["dc"], dims["dk"], dims["dv"]
    off_q, off_k, off_v = 2 * dc, 2 * dc + dk, 2 * dc + 2 * dk
    off_g = off_v + dv
    glu = _mm(xb, w["wmain"][:, 0:off_q]) + w["bmain"][:, 0:off_q]
    u_store(glu[:, 0:dc] * _sigmoid(glu[:, dc:off_q]))
    q_s[...] = (_mm(xb, w["wmain"][:, off_q:off_k]) + w["bmain"][:, off_q:off_k]) * (dims["hk"] ** -0.5)
    k_s[...] = _mm(xb, w["wmain"][:, off_k:off_v]) + w["bmain"][:, off_k:off_v]
    v_s[...] = _mm(xb, w["wmain"][:, off_v:off_g]) + w["bmain"][:, off_v:off_g]
    lr = _mm(xb, w["wlr"][...]) + w["blr"][...]
    z = _mm(lr.astype(BF16), w["wgu"][...]) + w["bgu"][...]
    lf_s[...] = -(jnp.maximum(-z, 0.0) + jnp.log1p(jnp.exp(-jnp.abs(z)))) * (1.0 / GATE_TAU)


def _gla_chunk(q_s, k_s, v_s, lf_s, o_s, rows, c, get_state, set_state, dims):
    hk, hv, dk = dims["hk"], dims["hv"], dims["dk"]
    lf = lf_s[rows, :]
    ri = lax.broadcasted_iota(jnp.int32, (c, c), 0)
    ci = lax.broadcasted_iota(jnp.int32, (c, c), 1)
    causal = ri >= ci
    tri = causal.astype(BF16)
    hi, mid, lo = _split3(lf)
    b = _mm(tri, hi) + _mm(tri, mid) + _mm(tri, lo)
    b_last = b[c - 1:c, :]
    q_t = q_s[rows, :] * jnp.exp(b)
    kk = k_s[rows, :]
    k_t = kk * jnp.exp(-b)
    k_e = kk * jnp.exp(b_last - b)
    d_t = jnp.transpose(jnp.broadcast_to(jnp.exp(b_last), (128, dk)))
    vv = v_s[rows, :]
    for h in range(GLA_HEADS):
        ks = slice(h * hk, (h + 1) * hk)
        vs = slice(h * hv, (h + 1) * hv)
        qh = q_t[:, ks].astype(BF16)
        kh = k_t[:, ks].astype(BF16)
        keh = k_e[:, ks].astype(BF16)
        vh = vv[:, vs].astype(BF16)
        att = lax.dot_general(qh, kh, (((1,), (1,)), ((), ())), preferred_element_type=F32)
        att = jnp.where(causal, att, 0.0)
        s_old = get_state(h)
        o_s[rows, vs] = _mm(qh, s_old.astype(BF16)) + _mm(att.astype(BF16), vh)
        upd = lax.dot_general(keh, vh, (((0,), (0,)), ((), ())), preferred_element_type=F32)
        dcol = d_t[ks, :]
        decay = jnp.concatenate([dcol] * (hv // 128), axis=1)
        set_state(h, decay * s_old + upd)


def _conv_block(win, convw_ref, n):
    wn = n + HIST_PAD
    base = HIST_PAD - (CONV_W - 1)
    acc = None
    for b in range(8):
        wb = win if b == 0 else pltpu.roll(win, wn - b, axis=0)
        for a in range((base + CONV_W + 7) // 8):
            j = 8 * a + b - base
            if 0 <= j < CONV_W:
                term = convw_ref[j:j + 1, :] * wb[8 * a:8 * a + n, :]
                acc = term if acc is None else acc + term
    return acc


def _tail(x, xb, c, w, dims, o_s):
    d, dc, dk, dv, hv = dims["d"], dims["dc"], dims["dk"], dims["dv"], dims["hv"]
    off_g = 2 * dc + 2 * dk + dv
    off_lr = off_g + dv
    c = _silu(_layer_norm(c + w["convb"][...], w["clng"][...], w["clnb"][...]))
    y_a = _mm(c.astype(BF16), w["wco"][...]) + w["bco"][...]
    g_out = _mm(xb, w["wmain"][:, off_g:off_lr]) + w["bmain"][:, off_g:off_lr]
    heads = []
    for h in range(GLA_HEADS):
        oh = o_s[:, h * hv:(h + 1) * hv]
        ms = jnp.mean(oh * oh, axis=-1, keepdims=True)
        heads.append(oh * lax.rsqrt(ms + RMS_EPS) * w["gng"][...])
    o = jnp.concatenate(heads, axis=1) * _silu(g_out)
    y_b = _mm(o.astype(BF16), w["wgo"][...])
    gates = _sigmoid(_mm(xb, w["wgates"][...]) + w["bgates"][...])
    merged = gates[:, 0:d] * y_a + gates[:, d:2 * d] * y_b
    m = _mm(merged.astype(BF16), w["wo"][...])
    return _layer_norm(dims["alpha"] * x + m, w["ln1g"][...], w["ln1b"][...])


_W_NAMES = ("wmain", "bmain", "wlr", "blr", "wgates", "bgates", "convw", "convb", "clng", "clnb",
            "wco", "bco", "wgu", "bgu", "gng", "wgo", "wo", "ln1g", "ln1b")


def _mixer_prompt_kernel(*refs, dims, tl, nt, n_prompt):
    nw = len(_W_NAMES)
    x_ref, tail_ref = refs[0:2]
    w = dict(zip(_W_NAMES, refs[2:2 + nw]))
    x1_ref, hist_ref, state_ref = refs[2 + nw:5 + nw]
    scratch = refs[5 + nw:]
    i = pl.program_id(0)

    @pl.when(i < n_prompt)
    def _():
        _mixer_prompt_tile(i % nt, x_ref, w, x1_ref, hist_ref, state_ref, scratch, dims, tl, nt)

    @pl.when(i >= n_prompt)
    def _():
        x1_ref[...] = tail_ref[...]


def _mixer_prompt_tile(j, x_ref, w, x1_ref, hist_ref, state_ref, scratch, dims, tl, nt):
    ubuf, q_s, k_s, v_s, lf_s, o_s, c_s, s_s = scratch
    dc = dims["dc"]

    @pl.when(j == 0)
    def _():
        ubuf[0:HIST_PAD, :] = jnp.zeros((HIST_PAD, dc), F32)
        s_s[...] = jnp.zeros(s_s.shape, F32)

    x = x_ref[...]
    xb = x.astype(BF16)

    def u_store(u):
        ubuf[HIST_PAD:HIST_PAD + tl, :] = u

    _project(xb, w, dims, u_store, q_s, k_s, v_s, lf_s)

    def chunk_body(ci, carry):
        rows = pl.ds(pl.multiple_of(ci * CHUNK, CHUNK), CHUNK)

        def set_state(h, val):
            s_s[h] = val

        _gla_chunk(q_s, k_s, v_s, lf_s, o_s, rows, CHUNK, lambda h: s_s[h], set_state, dims)
        return carry

    lax.fori_loop(0, tl // CHUNK, chunk_body, 0)

    def conv_body(bi, carry):
        r0 = pl.multiple_of(bi * CONV_ROWS, CONV_ROWS)
        win = ubuf[pl.ds(r0, CONV_ROWS + HIST_PAD), :]
        c_s[pl.ds(r0, CONV_ROWS), :] = _conv_block(win, w["convw"], CONV_ROWS)
        return carry

    lax.fori_loop(0, tl // CONV_ROWS, conv_body, 0)

    x1_ref[...] = _tail(x, xb, c_s[...], w, dims, o_s)

    @pl.when(j == nt - 1)
    def _():
        hist_ref[0] = ubuf[tl + HIST_PAD - (CONV_W - 1):tl + HIST_PAD, :]
        state_ref[0] = s_s[...]

    ubuf[0:HIST_PAD, :] = ubuf[tl:tl + HIST_PAD, :]


def _mixer_sample_kernel(*refs, dims, ns, ls):
    nw = len(_W_NAMES)
    x_ref, hist_in_ref, state_in_ref = refs[0:3]
    w = dict(zip(_W_NAMES, refs[3:3 + nw]))
    x1_ref, hist_ref, state_ref = refs[3 + nw:6 + nw]
    ubuf, q_s, k_s, v_s, lf_s, o_s, c_s = refs[6 + nw:]
    i = pl.program_id(0)
    dc = dims["dc"]
    hl = CONV_W - 1

    @pl.when(i == 0)
    def _():
        xb = x_ref[...].astype(BF16)
        ubuf[:, 0:8, :] = jnp.zeros((ns, 8, dc), F32)
        ubuf[:, HIST_PAD - hl:HIST_PAD, :] = hist_in_ref[...]

        def u_store(u):
            ubuf[:, HIST_PAD:HIST_PAD + ls, :] = u.reshape(ns, ls, dc)

        _project(xb, w, dims, u_store, q_s, k_s, v_s, lf_s)

    rows = pl.ds(pl.multiple_of(i * ls, ls), ls)

    def set_state(h, val):
        state_ref[0, h] = val

    _gla_chunk(q_s, k_s, v_s, lf_s, o_s, rows, ls, lambda h: state_in_ref[0, h], set_state, dims)
    win = ubuf[i]
    c_s[rows, :] = _conv_block(win, w["convw"], ls)
    hist_ref[i] = win[HIST_PAD + ls - hl:HIST_PAD + ls, :]

    @pl.when(i == ns - 1)
    def _():
        x = x_ref[...]
        x1_ref[...] = _tail(x, x.astype(BF16), c_s[...], w, dims, o_s)


def _mixer_weights(l, p, dims):
    dc, dk, dv, rank = dims["dc"], dims["dk"], dims["dv"], dims["rank"]
    off_lr = 2 * dc + 2 * dk + 2 * dv
    off_gates = off_lr + rank
    w_in, b_in = p["w_in"][l], p["b_in"][l]
    row = lambda v: v.reshape(1, -1).astype(F32)
    return (
        w_in[:, :off_lr].astype(BF16), row(b_in[:off_lr]),
        w_in[:, off_lr:off_gates].astype(BF16), row(b_in[off_lr:off_gates]),
        w_in[:, off_gates:].astype(BF16), row(b_in[off_gates:]),
        p["conv_w"][l].astype(F32), row(p["conv_b"][l]), row(p["conv_ln_g"][l]), row(p["conv_ln_b"][l]),
        p["w_conv_out"][l].astype(BF16), row(p["b_conv_out"][l]),
        p["w_gate_up"][l].astype(BF16), row(p["b_gate"][l]),
        row(p["gla_norm_g"][l]),
        p["w_gla_out"][l].astype(BF16), p["w_o"][l].astype(BF16),
        row(p["ln1_g"][l]), row(p["ln1_b"][l]),
    )


def _mixer_prompt(x, x1_tail, wts, dims, tl, bsz, seq):
    d = x.shape[1]
    dc, dk, dv, hk, hv = dims["dc"], dims["dk"], dims["dv"], dims["hk"], dims["hv"]
    hl = CONV_W - 1
    nt = seq // tl
    n_prompt = bsz * nt
    n_tail = x1_tail.shape[0] // tl
    kern = functools.partial(_mixer_prompt_kernel, dims=dims, tl=tl, nt=nt, n_prompt=n_prompt)
    seq_of = lambda i: jnp.minimum(i // nt, bsz - 1)
    return pl.pallas_call(
        kern,
        grid=(n_prompt + n_tail,),
        in_specs=[pl.BlockSpec((tl, d), lambda i: (jnp.minimum(i, n_prompt - 1), 0)),
                  pl.BlockSpec((tl, d), lambda i: (jnp.maximum(i - n_prompt, 0), 0))]
        + [_full_spec(a.shape) for a in wts],
        out_specs=[
            pl.BlockSpec((tl, d), lambda i: (i, 0)),
            pl.BlockSpec((1, hl, dc), lambda i: (seq_of(i), 0, 0)),
            pl.BlockSpec((1, GLA_HEADS, hk, hv), lambda i: (seq_of(i), 0, 0, 0)),
        ],
        out_shape=[
            jax.ShapeDtypeStruct(((n_prompt + n_tail) * tl, d), F32),
            jax.ShapeDtypeStruct((bsz, hl, dc), F32),
            jax.ShapeDtypeStruct((bsz, GLA_HEADS, hk, hv), F32),
        ],
        scratch_shapes=[
            pltpu.VMEM((HIST_PAD + tl, dc), F32),
            pltpu.VMEM((tl, dk), F32), pltpu.VMEM((tl, dk), F32), pltpu.VMEM((tl, dv), F32),
            pltpu.VMEM((tl, dk), F32), pltpu.VMEM((tl, dv), F32), pltpu.VMEM((tl, dc), F32),
            pltpu.VMEM((GLA_HEADS, hk, hv), F32),
        ],
        compiler_params=pltpu.CompilerParams(
            dimension_semantics=("arbitrary",), vmem_limit_bytes=VMEM_LIMIT),
        name="mixer_prompt",
    )(x, x1_tail, *wts)


def _mixer_sample(x, in_blk, hist, state, wts, dims, ns, ls):
    d = x.shape[1]
    dc, dk, dv, hk, hv = dims["dc"], dims["dk"], dims["dv"], dims["hk"], dims["hv"]
    hl = CONV_W - 1
    t = ns * ls
    kern = functools.partial(_mixer_sample_kernel, dims=dims, ns=ns, ls=ls)
    x1, hist_o, state_o = pl.pallas_call(
        kern,
        grid=(ns,),
        in_specs=[
            pl.BlockSpec((t, d), lambda i: (in_blk, 0)),
            pl.BlockSpec((ns, hl, dc), lambda i: (0, 0, 0)),
            pl.BlockSpec((1, GLA_HEADS, hk, hv), lambda i: (i, 0, 0, 0)),
        ] + [_full_spec(a.shape) for a in wts],
        out_specs=[
            pl.BlockSpec((t, d), lambda i: (0, 0)),
            pl.BlockSpec((ns, hl, dc), lambda i: (0, 0, 0)),
            pl.BlockSpec((1, GLA_HEADS, hk, hv), lambda i: (i, 0, 0, 0)),
        ],
        out_shape=[
            jax.ShapeDtypeStruct((t, d), F32),
            jax.ShapeDtypeStruct((ns, hl, dc), F32),
            jax.ShapeDtypeStruct((ns, GLA_HEADS, hk, hv), F32),
        ],
        scratch_shapes=[
            pltpu.VMEM((ns, HIST_PAD + ls, dc), F32),
            pltpu.VMEM((t, dk), F32), pltpu.VMEM((t, dk), F32), pltpu.VMEM((t, dv), F32),
            pltpu.VMEM((t, dk), F32), pltpu.VMEM((t, dv), F32), pltpu.VMEM((t, dc), F32),
        ],
        compiler_params=pltpu.CompilerParams(
            dimension_semantics=("arbitrary",), vmem_limit_bytes=VMEM_LIMIT),
        name="mixer_sample",
    )(x, hist, state, *wts)
    return x1, hist_o, state_o


def _token_out(t, d, tm, split):
    if split is None:
        return [pl.BlockSpec((tm, d), lambda i: (i, 0))], [jax.ShapeDtypeStruct((t, d), F32)]
    tp, ts = split
    npt = tp // tm
    specs = [pl.BlockSpec((tm, d), lambda i: (jnp.minimum(i, npt - 1), 0)),
             pl.BlockSpec((tm, d), lambda i: (jnp.maximum(i - npt, 0), 0))]
    return specs, [jax.ShapeDtypeStruct((tp, d), F32), jax.ShapeDtypeStruct((ts, d), F32)]


def _token_store(o_refs, val, npt):
    if len(o_refs) == 1:
        o_refs[0][...] = val
        return
    i = pl.program_id(0)

    @pl.when(i < npt)
    def _():
        o_refs[0][...] = val

    @pl.when(i >= npt)
    def _():
        o_refs[1][...] = val


def _ffn_dense_kernel(x_ref, wg_ref, wu_ref, wd_ref, g_ref, b_ref, *o_refs, alpha, npt):
    x = x_ref[...]
    xb = x.astype(BF16)
    h = _silu(_mm(xb, wg_ref[...])) * _mm(xb, wu_ref[...])
    f = _mm(h.astype(BF16), wd_ref[...])
    _token_store(o_refs, _layer_norm(alpha * x + f, g_ref[...], b_ref[...]), npt)


def _ffn_dense(x, wg, wu, wd, g, b, alpha, tm, split):
    t, d = x.shape
    ops = (wg.astype(BF16), wu.astype(BF16), wd.astype(BF16), g.reshape(1, d), b.reshape(1, d))
    out_specs, out_shape = _token_out(t, d, tm, split)
    npt = None if split is None else split[0] // tm
    return pl.pallas_call(
        functools.partial(_ffn_dense_kernel, alpha=alpha, npt=npt),
        grid=(t // tm,),
        in_specs=[pl.BlockSpec((tm, d), lambda i: (i, 0))] + [_full_spec(a.shape) for a in ops],
        out_specs=out_specs,
        out_shape=out_shape,
        compiler_params=pltpu.CompilerParams(
            dimension_semantics=("arbitrary",), vmem_limit_bytes=VMEM_LIMIT),
        name="ffn_dense",
    )(x, *ops)


def _router_kernel(x_ref, wr_ref, route_ref, cnt_ref, carry, *, ne, tr):
    i = pl.program_id(0)

    @pl.when(i == 0)
    def _():
        carry[...] = jnp.zeros(carry.shape, F32)

    xh = x_ref[...]
    x_hi = xh.astype(BF16)
    x_lo = (xh - x_hi.astype(F32)).astype(BF16)
    wr = wr_ref[...]
    w_hi = wr.astype(BF16)
    w_lo = (wr - w_hi.astype(F32)).astype(BF16)
    nt = (((1,), (1,)), ((), ()))
    logits = (lax.dot_general(w_hi, x_hi, nt, preferred_element_type=F32)
              + lax.dot_general(w_hi, x_lo, nt, preferred_element_type=F32)
              + lax.dot_general(w_lo, x_hi, nt, preferred_element_type=F32))
    mx = jnp.max(logits, axis=0, keepdims=True)
    ex = jnp.exp(logits - mx)
    probs = ex / jnp.sum(ex, axis=0, keepdims=True)
    eid = lax.broadcasted_iota(jnp.int32, (ne, tr), 0)
    p1 = jnp.max(probs, axis=0, keepdims=True)
    i1 = jnp.min(jnp.where(probs == p1, eid, ne), axis=0, keepdims=True)
    rest = jnp.where(eid == i1, -1.0, probs)
    p2 = jnp.max(rest, axis=0, keepdims=True)
    i2 = jnp.min(jnp.where(rest == p2, eid, ne), axis=0, keepdims=True)
    den = p1 + p2
    oh1 = (eid == i1).astype(F32)
    oh2 = (eid == i2).astype(F32)
    oh = oh1 + oh2
    ri = lax.broadcasted_iota(jnp.int32, (tr, tr), 0)
    ci = lax.broadcasted_iota(jnp.int32, (tr, tr), 1)
    upper = (ri <= ci).astype(BF16)
    incl = _mm(oh.astype(BF16), upper)
    before = carry[:, 0:1] + incl - oh
    r1 = jnp.sum(oh1 * before, axis=0, keepdims=True)
    r2 = jnp.sum(oh2 * before, axis=0, keepdims=True)
    zero = jnp.zeros((1, tr), F32)
    route_ref[...] = jnp.concatenate(
        [i1.astype(F32), i2.astype(F32), p1 / den, p2 / den, r1, r2, zero, zero], axis=0)
    total = carry[:, 0:1] + incl[:, tr - 1:tr]
    carry[...] = jnp.broadcast_to(total, carry.shape)
    cnt_ref[...] = jnp.broadcast_to(total, cnt_ref.shape)


def _router(x, w_router, tr):
    t, d = x.shape
    ne = w_router.shape[1]
    return pl.pallas_call(
        functools.partial(_router_kernel, ne=ne, tr=tr),
        grid=(t // tr,),
        in_specs=[pl.BlockSpec((tr, d), lambda i: (i, 0)), _full_spec((ne, d))],
        out_specs=[pl.BlockSpec((8, tr), lambda i: (0, i)), pl.BlockSpec((ne, 128), lambda i: (0, 0))],
        out_shape=[jax.ShapeDtypeStruct((8, t), F32), jax.ShapeDtypeStruct((ne, 128), F32)],
        scratch_shapes=[pltpu.VMEM((ne, 128), F32)],
        compiler_params=pltpu.CompilerParams(dimension_semantics=("arbitrary",)),
        name="moe_router",
    )(x, w_router.T.astype(F32))


def _dispatch_kernel(pos_ref, x_ref, zeros_ref, xs_ref, sem, *, tm):
    del zeros_ref

    def row_copy(r, k):
        return pltpu.make_async_copy(x_ref.at[pl.ds(r, 1), :], xs_ref.at[pl.ds(pos_ref[0, 0, 2 * r + k], 1), :], sem)

    def start(r, carry):
        row_copy(r, 0).start()
        row_copy(r, 1).start()
        return carry

    def wait(r, carry):
        row_copy(r, 0).wait()
        row_copy(r, 1).wait()
        return carry

    lax.fori_loop(0, tm, start, 0)
    lax.fori_loop(0, tm, wait, 0)


def _dispatch(x, pos, n_rows, tm):
    t, d = x.shape
    zeros = jnp.zeros((n_rows, d), F32)
    return pl.pallas_call(
        functools.partial(_dispatch_kernel, tm=tm),
        grid=(t // tm,),
        in_specs=[
            pl.BlockSpec((1, 1, 2 * tm), lambda i: (i, 0, 0), memory_space=pltpu.SMEM),
            pl.BlockSpec((tm, d), lambda i: (i, 0)),
            pl.BlockSpec(memory_space=pl.ANY),
        ],
        out_specs=pl.BlockSpec(memory_space=pl.ANY),
        out_shape=jax.ShapeDtypeStruct((n_rows, d), F32),
        scratch_shapes=[pltpu.SemaphoreType.DMA],
        input_output_aliases={2: 0},
        compiler_params=pltpu.CompilerParams(dimension_semantics=("arbitrary",), has_side_effects=True),
        name="moe_dispatch",
    )(pos.reshape(t // tm, 1, 2 * tm), x, zeros)


def _experts_kernel(te_ref, nv_ref, xs_ref, wg_ref, wu_ref, wd_ref, ys_ref):
    i = pl.program_id(0)

    @pl.when(i < nv_ref[0])
    def _():
        xb = xs_ref[...].astype(BF16)
        h = _silu(_mm(xb, wg_ref[0])) * _mm(xb, wu_ref[0])
        ys_ref[...] = _mm(h.astype(BF16), wd_ref[0])

    @pl.when(i >= nv_ref[0])
    def _():
        ys_ref[...] = jnp.zeros(ys_ref.shape, F32)


def _experts(xs, tile_expert, n_valid, wg, wu, wd, tm):
    n_rows, d = xs.shape
    ne, _, ff = wg.shape
    grid_spec = pltpu.PrefetchScalarGridSpec(
        num_scalar_prefetch=2,
        grid=(n_rows // tm,),
        in_specs=[
            pl.BlockSpec((tm, d), lambda i, te, nv: (i, 0)),
            pl.BlockSpec((1, d, ff), lambda i, te, nv: (te[i], 0, 0)),
            pl.BlockSpec((1, d, ff), lambda i, te, nv: (te[i], 0, 0)),
            pl.BlockSpec((1, ff, d), lambda i, te, nv: (te[i], 0, 0)),
        ],
        out_specs=pl.BlockSpec((tm, d), lambda i, te, nv: (i, 0)),
    )
    return pl.pallas_call(
        _experts_kernel,
        grid_spec=grid_spec,
        out_shape=jax.ShapeDtypeStruct((n_rows, d), F32),
        compiler_params=pltpu.CompilerParams(
            dimension_semantics=("arbitrary",), vmem_limit_bytes=VMEM_LIMIT),
        name="moe_experts",
    )(tile_expert, n_valid, xs, wg, wu, wd)


def _combine_kernel(pos_ref, x_ref, rt_ref, ys_ref, g_ref, b_ref, *rest, tm, alpha, npt):
    o_refs, (buf, sem) = rest[:-2], rest[-2:]

    def row_copy(r, k):
        return pltpu.make_async_copy(ys_ref.at[pl.ds(pos_ref[0, 0, 2 * r + k], 1), :], buf.at[k, pl.ds(r, 1), :], sem)

    def start(r, carry):
        row_copy(r, 0).start()
        row_copy(r, 1).start()
        return carry

    def wait(r, carry):
        row_copy(r, 0).wait()
        row_copy(r, 1).wait()
        return carry

    lax.fori_loop(0, tm, start, 0)
    lax.fori_loop(0, tm, wait, 0)
    rt = rt_ref[...]
    f = rt[:, 2:3] * buf[0] + rt[:, 3:4] * buf[1]
    _token_store(o_refs, _layer_norm(alpha * x_ref[...] + f, g_ref[...], b_ref[...]), npt)


def _combine(x, pos, route_t, ys, g, b, alpha, tm, split):
    t, d = x.shape
    out_specs, out_shape = _token_out(t, d, tm, split)
    npt = None if split is None else split[0] // tm
    return pl.pallas_call(
        functools.partial(_combine_kernel, tm=tm, alpha=alpha, npt=npt),
        grid=(t // tm,),
        in_specs=[
            pl.BlockSpec((1, 1, 2 * tm), lambda i: (i, 0, 0), memory_space=pltpu.SMEM),
            pl.BlockSpec((tm, d), lambda i: (i, 0)),
            pl.BlockSpec((tm, 8), lambda i: (i, 0)),
            pl.BlockSpec(memory_space=pl.ANY),
            _full_spec((1, d)), _full_spec((1, d)),
        ],
        out_specs=out_specs,
        out_shape=out_shape,
        scratch_shapes=[pltpu.VMEM((2, tm, d), F32), pltpu.SemaphoreType.DMA],
        compiler_params=pltpu.CompilerParams(dimension_semantics=("arbitrary",)),
        name="moe_combine",
    )(pos.reshape(t // tm, 1, 2 * tm), x, route_t, ys, g.reshape(1, d), b.reshape(1, d))


def _ffn_moe(x, w_router, wg, wu, wd, g, b, alpha, tr, tm, split):
    t, d = x.shape
    ne = w_router.shape[1]
    route, counts = _router(x, w_router, tr)
    cnt = counts[:, 0].astype(jnp.int32)
    gsz = ((cnt + tm - 1) // tm) * tm
    ends = jnp.cumsum(gsz)
    offs = ends - gsz
    n_tiles = (TOP_K * t) // tm + ne
    n_rows = n_tiles * tm
    tile_start = jnp.arange(n_tiles, dtype=jnp.int32) * tm
    tile_e = jnp.sum((tile_start[:, None] >= ends[None, :]).astype(jnp.int32), axis=1)
    n_valid = (ends[ne - 1] // tm).astype(jnp.int32).reshape(1)
    last_e = jnp.sum((ends[ne - 1] - 1 >= ends).astype(jnp.int32))
    tile_e = jnp.minimum(tile_e, last_e).astype(jnp.int32)
    i12 = route[0:2].astype(jnp.int32)
    base = sum(jnp.where(i12 == e, offs[e], 0) for e in range(ne))
    pos = (base + route[4:6].astype(jnp.int32)).T.reshape(-1)
    xs = _dispatch(x, pos, n_rows, tr)
    ys = _experts(xs, tile_e, n_valid, wg.astype(BF16), wu.astype(BF16), wd.astype(BF16), tm)
    return _combine(x, pos, route.T, ys, g, b, alpha, tr, split)


def kernel(x_prompt, x_sample, cache_conv, state_gla, w_in, b_in, conv_w, conv_b, conv_ln_g, conv_ln_b, w_conv_out, b_conv_out, w_gate_up, b_gate, gla_norm_g, w_gla_out, w_o, ln1_g, ln1_b, ln2_g, ln2_b, ff_w_gate, ff_w_up, ff_w_down, w_router, moe_w_gate, moe_w_up, moe_w_down):
    p = dict(w_in=w_in, b_in=b_in, conv_w=conv_w, conv_b=conv_b, conv_ln_g=conv_ln_g, conv_ln_b=conv_ln_b,
             w_conv_out=w_conv_out, b_conv_out=b_conv_out, w_gate_up=w_gate_up, b_gate=b_gate,
             gla_norm_g=gla_norm_g, w_gla_out=w_gla_out, w_o=w_o, ln1_g=ln1_g, ln1_b=ln1_b)
    depth = w_in.shape[0]
    bsz, seq, d = x_prompt.shape
    ns, ls, _ = x_sample.shape
    dc = conv_w.shape[-1]
    rank, dk = w_gate_up.shape[1], w_gate_up.shape[2]
    dv = w_gla_out.shape[1]
    dims = dict(d=d, dc=dc, dk=dk, dv=dv, rank=rank, hk=dk // GLA_HEADS, hv=dv // GLA_HEADS,
                alpha=(2.0 * depth) ** 0.25)
    alpha = dims["alpha"]
    tl = min(512, seq)
    tp = bsz * seq
    ts = ns * ls
    tm = min(512, ts)
    assert seq % tl == 0 and tp % ts == 0 and ts % tm == 0 and ts % tl == 0

    x_p, x_s, s_blk = x_prompt.reshape(tp, d), x_sample.reshape(ts, d), 0
    hist_p, state_p, hist_s, state_s = [], [], [], []
    for l in range(depth):
        wts = _mixer_weights(l, p, dims)
        x1s, hs, ss = _mixer_sample(x_s, s_blk, cache_conv[l], state_gla[l], wts, dims, ns, ls)
        x1, hp, sp = _mixer_prompt(x_p, x1s, wts, dims, tl, bsz, seq)
        hist_p.append(hp), state_p.append(sp), hist_s.append(hs), state_s.append(ss)
        split = (tp, ts) if l == depth - 1 else None
        if l % 2 == 0:
            x2 = _ffn_dense(x1, ff_w_gate[l // 2], ff_w_up[l // 2], ff_w_down[l // 2], ln2_g[l], ln2_b[l],
                            alpha, tm, split)
        else:
            x2 = _ffn_moe(x1, w_router[l // 2], moe_w_gate[l // 2], moe_w_up[l // 2], moe_w_down[l // 2],
                          ln2_g[l], ln2_b[l], alpha, tm, tm, split)
        if split is None:
            x_p, x_s, s_blk = x2[0], x2[0], tp // ts
    y_p, y_s = x2
    return (y_p.reshape(bsz, seq, d), y_s.reshape(ns, ls, d), jnp.stack(hist_p),
            jnp.stack(state_p).astype(state_gla.dtype), jnp.stack(hist_s), jnp.stack(state_s).astype(state_gla.dtype))
```

```python
import functools

import jax
import jax.numpy as jnp
from jax import lax
from jax.experimental import pallas as pl
from jax.experimental.pallas import tpu as pltpu

CHUNK = 64
CONV_W = 31
GLA_HEADS = 4
GATE_TAU = 16.0
LN_EPS = 1e-5
RMS_EPS = 1e-6
TOP_K = 2

HIST_PAD = 32
CONV_ROWS = 32
ROW_DMA_UNROLL = 8
VMEM_LIMIT = 56 * 1024 * 1024

BF16 = jnp.bfloat16
F32 = jnp.float32


def _mm(a, b):
    return jnp.dot(a, b, preferred_element_type=F32)


def _sigmoid(x):
    return 0.5 * jnp.tanh(0.5 * x) + 0.5


def _silu(x):
    return x * _sigmoid(x)


def _layer_norm(x, g, b):
    mu = jnp.mean(x, axis=-1, keepdims=True)
    xc = x - mu
    var = jnp.mean(xc * xc, axis=-1, keepdims=True)
    return xc * lax.rsqrt(var + LN_EPS) * g + b


def _split3(x):
    hi = x.astype(BF16)
    r1 = x - hi.astype(F32)
    mid = r1.astype(BF16)
    lo = (r1 - mid.astype(F32)).astype(BF16)
    return hi, mid, lo


def _full_spec(shape):
    zeros = (0,) * len(shape)
    return pl.BlockSpec(shape, lambda *_: zeros, pipeline_mode=pl.Buffered(1))


def _project(xb, w, dims, u_store, q_s, k_s, v_s, lf_s):
    dc, dk, dv = dims["dc"], dims["dk"], dims["dv"]
    off_q, off_k, off_v = 2 * dc, 2 * dc + dk, 2 * dc + 2 * dk
    off_g = off_v + dv
    glu = _mm(xb, w["wmain"][:, 0:off_q]) + w["bmain"][:, 0:off_q]
    u_store(glu[:, 0:dc] * _sigmoid(glu[:, dc:off_q]))
    q_s[...] = (_mm(xb, w["wmain"][:, off_q:off_k]) + w["bmain"][:, off_q:off_k]) * (dims["hk"] ** -0.5)
    k_s[...] = _mm(xb, w["wmain"][:, off_k:off_v]) + w["bmain"][:, off_k:off_v]
    v_s[...] = _mm(xb, w["wmain"][:, off_v:off_g]) + w["bmain"][:, off_v:off_g]
    lr = _mm(xb, w["wlr"][...]) + w["blr"][...]
    z = _mm(lr.astype(BF16), w["wgu"][...]) + w["bgu"][...]
    lf_s[...] = -(jnp.maximum(-z, 0.0) + jnp.log1p(jnp.exp(-jnp.abs(z)))) * (1.0 / GATE_TAU)


def _gla_chunk(q_s, k_s, v_s, lf_s, o_s, rows, c, get_state, set_state, dims):
    hk, hv, dk = dims["hk"], dims["hv"], dims["dk"]
    lf = lf_s[rows, :]
    ri = lax.broadcasted_iota(jnp.int32, (c, c), 0)
    ci = lax.broadcasted_iota(jnp.int32, (c, c), 1)
    causal = ri >= ci
    tri = causal.astype(BF16)
    hi, mid, lo = _split3(lf)
    b = _mm(tri, hi) + _mm(tri, mid) + _mm(tri, lo)
    b_last = b[c - 1:c, :]
    q_t = q_s[rows, :] * jnp.exp(b)
    kk = k_s[rows, :]
    k_t = kk * jnp.exp(-b)
    k_e = kk * jnp.exp(b_last - b)
    d_t = jnp.transpose(jnp.broadcast_to(jnp.exp(b_last), (128, dk)))
    vv = v_s[rows, :]
    for h in range(GLA_HEADS):
        ks = slice(h * hk, (h + 1) * hk)
        vs = slice(h * hv, (h + 1) * hv)
        qh = q_t[:, ks].astype(BF16)
        kh = k_t[:, ks].astype(BF16)
        keh = k_e[:, ks].astype(BF16)
        vh = vv[:, vs].astype(BF16)
        att = lax.dot_general(qh, kh, (((1,), (1,)), ((), ())), preferred_element_type=F32)
        att = jnp.where(causal, att, 0.0)
        s_old = get_state(h)
        o_s[rows, vs] = _mm(qh, s_old.astype(BF16)) + _mm(att.astype(BF16), vh)
        upd = lax.dot_general(keh, vh, (((0,), (0,)), ((), ())), preferred_element_type=F32)
        dcol = d_t[ks, :]
        decay = jnp.concatenate([dcol] * (hv // 128), axis=1)
        set_state(h, decay * s_old + upd)


def _conv_rows(load, w8_ref, n):
    base = HIST_PAD - (CONV_W - 1)
    acc = None
    for j in range(CONV_W):
        xw = load(base + j)
        term = w8_ref[8 * j:8 * j + 8, :][None] * xw.reshape(n // 8, 8, xw.shape[-1])
        acc = term if acc is None else acc + term
    return acc.reshape(n, acc.shape[-1])


def _tail(x, xb, c, w, dims, o_s):
    d, dc, dk, dv, hv = dims["d"], dims["dc"], dims["dk"], dims["dv"], dims["hv"]
    off_g = 2 * dc + 2 * dk + dv
    off_lr = off_g + dv
    c = _silu(_layer_norm(c + w["convb"][...], w["clng"][...], w["clnb"][...]))
    y_a = _mm(c.astype(BF16), w["wco"][...]) + w["bco"][...]
    g_out = _mm(xb, w["wmain"][:, off_g:off_lr]) + w["bmain"][:, off_g:off_lr]
    heads = []
    for h in range(GLA_HEADS):
        oh = o_s[:, h * hv:(h + 1) * hv]
        ms = jnp.mean(oh * oh, axis=-1, keepdims=True)
        heads.append(oh * lax.rsqrt(ms + RMS_EPS) * w["gng"][...])
    o = jnp.concatenate(heads, axis=1) * _silu(g_out)
    y_b = _mm(o.astype(BF16), w["wgo"][...])
    gates = _sigmoid(_mm(xb, w["wgates"][...]) + w["bgates"][...])
    merged = gates[:, 0:d] * y_a + gates[:, d:2 * d] * y_b
    m = _mm(merged.astype(BF16), w["wo"][...])
    return _layer_norm(dims["alpha"] * x + m, w["ln1g"][...], w["ln1b"][...])


_W_NAMES = ("wmain", "bmain", "wlr", "blr", "wgates", "bgates", "convw", "convb", "clng", "clnb",
            "wco", "bco", "wgu", "bgu", "gng", "wgo", "wo", "ln1g", "ln1b")


def _mixer_prompt_kernel(*refs, dims, tl, nt, n_prompt):
    nw = len(_W_NAMES)
    x_ref, tail_ref = refs[0:2]
    w = dict(zip(_W_NAMES, refs[2:2 + nw]))
    x1_ref, hist_ref, state_ref = refs[2 + nw:5 + nw]
    scratch = refs[5 + nw:]
    i = pl.program_id(0)

    @pl.when(i < n_prompt)
    def _():
        _mixer_prompt_tile(i % nt, x_ref, w, x1_ref, hist_ref, state_ref, scratch, dims, tl, nt)

    @pl.when(i >= n_prompt)
    def _():
        x1_ref[...] = tail_ref[...]


def _mixer_prompt_tile(j, x_ref, w, x1_ref, hist_ref, state_ref, scratch, dims, tl, nt):
    ubuf, qt_s, kt_s, ke_s, vb_s, g_s, gt_s, o_s, c_s, s_s, dt_s = scratch
    d, dc, dk, dv, hk, hv = dims["d"], dims["dc"], dims["dk"], dims["dv"], dims["hk"], dims["hv"]
    off_q, off_k, off_v = 2 * dc, 2 * dc + dk, 2 * dc + 2 * dk
    off_g = off_v + dv
    off_lr = off_g + dv
    nch = tl // CHUNK
    rep = 128 // nch
    wm, bm = w["wmain"], w["bmain"]

    @pl.when(j == 0)
    def _():
        ubuf[0:HIST_PAD, :] = jnp.zeros((HIST_PAD, dc), F32)
        s_s[...] = jnp.zeros(s_s.shape, F32)

    x = x_ref[...]
    xb = x.astype(BF16)

    glu = _mm(xb, wm[:, 0:off_q]) + bm[:, 0:off_q]
    ubuf[HIST_PAD:HIST_PAD + tl, :] = glu[:, 0:dc] * _sigmoid(glu[:, dc:off_q])
    lr = _mm(xb, w["wlr"][...]) + w["blr"][...]
    z = _mm(lr.astype(BF16), w["wgu"][...]) + w["bgu"][...]
    lf = -(jnp.maximum(-z, 0.0) + jnp.log1p(jnp.exp(-jnp.abs(z)))) * (1.0 / GATE_TAU)
    ri = lax.broadcasted_iota(jnp.int32, (CHUNK, CHUNK), 0)
    ci = lax.broadcasted_iota(jnp.int32, (CHUNK, CHUNK), 1)
    causal = ri >= ci
    tri = causal.astype(BF16)
    parts = _split3(lf)
    b = jnp.concatenate(
        [sum(_mm(tri, p[c * CHUNK:(c + 1) * CHUNK, :]) for p in parts) for c in range(nch)], axis=0)
    b3 = b.reshape(nch, CHUNK, dk)
    b_last = b3[:, CHUNK - 1:CHUNK, :]
    q = (_mm(xb, wm[:, off_q:off_k]) + bm[:, off_q:off_k]) * (hk ** -0.5)
    qt_s[...] = (q * jnp.exp(b)).astype(BF16)
    kk = _mm(xb, wm[:, off_k:off_v]) + bm[:, off_k:off_v]
    kt_s[...] = (kk * jnp.exp(-b)).astype(BF16)
    ke_s[...] = (kk.reshape(nch, CHUNK, dk) * jnp.exp(b_last - b3)).reshape(tl, dk).astype(BF16)
    vb_s[...] = (_mm(xb, wm[:, off_v:off_g]) + bm[:, off_v:off_g]).astype(BF16)
    d_rows = jnp.broadcast_to(jnp.exp(b_last), (nch, rep, dk)).reshape(nch * rep, dk)
    if nch * rep < 128:
        d_rows = jnp.concatenate([d_rows, jnp.zeros((128 - nch * rep, dk), F32)], axis=0)
    dt_s[...] = jnp.transpose(d_rows)
    g_s[...] = _silu(_mm(xb, wm[:, off_g:off_lr]) + bm[:, off_g:off_lr])

    states = [s_s[h] for h in range(GLA_HEADS)]
    conv_per = tl // CONV_ROWS // nch
    gw = 2 * d // nch
    for c in range(nch):
        rows = slice(c * CHUNK, (c + 1) * CHUNK)
        for h in range(GLA_HEADS):
            ks = slice(h * hk, (h + 1) * hk)
            vs = slice(h * hv, (h + 1) * hv)
            qh, kh, keh, vh = qt_s[rows, ks], kt_s[rows, ks], ke_s[rows, ks], vb_s[rows, vs]
            att = lax.dot_general(qh, kh, (((1,), (1,)), ((), ())), preferred_element_type=F32)
            att = jnp.where(causal, att, 0.0)
            s_old = states[h]
            o_s[rows, vs] = _mm(qh, s_old.astype(BF16)) + _mm(att.astype(BF16), vh)
            upd = lax.dot_general(keh, vh, (((0,), (0,)), ((), ())), preferred_element_type=F32)
            decay = jnp.broadcast_to(dt_s[ks, c * rep:c * rep + 1], (hk, hv))
            states[h] = decay * s_old + upd
        for bi in range(c * conv_per, (c + 1) * conv_per):
            r0 = bi * CONV_ROWS
            c_s[r0:r0 + CONV_ROWS, :] = _conv_rows(
                lambda off: ubuf[r0 + off:r0 + off + CONV_ROWS, :], w["convw"], CONV_ROWS)
        cols = slice(c * gw, (c + 1) * gw)
        gt_s[:, cols] = _sigmoid(_mm(xb, w["wgates"][:, cols]) + w["bgates"][:, cols])
    for h in range(GLA_HEADS):
        s_s[h] = states[h]

    cv = _silu(_layer_norm(c_s[...] + w["convb"][...], w["clng"][...], w["clnb"][...]))
    y_a = _mm(cv.astype(BF16), w["wco"][...]) + w["bco"][...]
    heads = []
    for h in range(GLA_HEADS):
        oh = o_s[:, h * hv:(h + 1) * hv]
        ms = jnp.mean(oh * oh, axis=-1, keepdims=True)
        heads.append(oh * lax.rsqrt(ms + RMS_EPS) * w["gng"][...])
    o = jnp.concatenate(heads, axis=1) * g_s[...]
    y_b = _mm(o.astype(BF16), w["wgo"][...])
    merged = gt_s[:, 0:d] * y_a + gt_s[:, d:2 * d] * y_b
    m = _mm(merged.astype(BF16), w["wo"][...])
    x1_ref[...] = _layer_norm(dims["alpha"] * x + m, w["ln1g"][...], w["ln1b"][...])

    @pl.when(j == nt - 1)
    def _():
        hist_ref[0] = ubuf[tl + HIST_PAD - (CONV_W - 1):tl + HIST_PAD, :]
        state_ref[0] = s_s[...]

    ubuf[0:HIST_PAD, :] = ubuf[tl:tl + HIST_PAD, :]


def _mixer_sample_kernel(*refs, dims, ns, ls):
    nw = len(_W_NAMES)
    x_ref, hist_in_ref, state_in_ref = refs[0:3]
    w = dict(zip(_W_NAMES, refs[3:3 + nw]))
    x1_ref, hist_ref, state_ref = refs[3 + nw:6 + nw]
    ubuf, q_s, k_s, v_s, lf_s, o_s, c_s = refs[6 + nw:]
    i = pl.program_id(0)
    dc = dims["dc"]
    hl = CONV_W - 1

    @pl.when(i == 0)
    def _():
        xb = x_ref[...].astype(BF16)
        ubuf[:, 0:8, :] = jnp.zeros((ns, 8, dc), F32)
        ubuf[:, HIST_PAD - hl:HIST_PAD, :] = hist_in_ref[...]

        def u_store(u):
            ubuf[:, HIST_PAD:HIST_PAD + ls, :] = u.reshape(ns, ls, dc)

        _project(xb, w, dims, u_store, q_s, k_s, v_s, lf_s)

    rows = pl.ds(pl.multiple_of(i * ls, ls), ls)

    def set_state(h, val):
        state_ref[0, h] = val

    _gla_chunk(q_s, k_s, v_s, lf_s, o_s, rows, ls, lambda h: state_in_ref[0, h], set_state, dims)
    c_s[rows, :] = _conv_rows(lambda off: ubuf[i, pl.ds(off, ls), :], w["convw"], ls)
    hist_ref[i] = ubuf[i, pl.ds(HIST_PAD + ls - hl, hl), :]

    @pl.when(i == ns - 1)
    def _():
        x = x_ref[...]
        x1_ref[...] = _tail(x, x.astype(BF16), c_s[...], w, dims, o_s)


def _mixer_weights(l, p, dims):
    dc, dk, dv, rank = dims["dc"], dims["dk"], dims["dv"], dims["rank"]
    off_lr = 2 * dc + 2 * dk + 2 * dv
    off_gates = off_lr + rank
    w_in, b_in = p["w_in"][l], p["b_in"][l]
    row = lambda v: v.reshape(1, -1).astype(F32)
    return (
        w_in[:, :off_lr].astype(BF16), row(b_in[:off_lr]),
        w_in[:, off_lr:off_gates].astype(BF16), row(b_in[off_lr:off_gates]),
        w_in[:, off_gates:].astype(BF16), row(b_in[off_gates:]),
        jnp.repeat(p["conv_w"][l].astype(F32), 8, axis=0),
        row(p["conv_b"][l]), row(p["conv_ln_g"][l]), row(p["conv_ln_b"][l]),
        p["w_conv_out"][l].astype(BF16), row(p["b_conv_out"][l]),
        p["w_gate_up"][l].astype(BF16), row(p["b_gate"][l]),
        row(p["gla_norm_g"][l]),
        p["w_gla_out"][l].astype(BF16), p["w_o"][l].astype(BF16),
        row(p["ln1_g"][l]), row(p["ln1_b"][l]),
    )


def _mixer_prompt(x, x1_tail, wts, dims, tl, bsz, seq):
    d = x.shape[1]
    dc, dk, dv, hk, hv = dims["dc"], dims["dk"], dims["dv"], dims["hk"], dims["hv"]
    hl = CONV_W - 1
    nt = seq // tl
    n_prompt = bsz * nt
    n_tail = x1_tail.shape[0] // tl
    kern = functools.partial(_mixer_prompt_kernel, dims=dims, tl=tl, nt=nt, n_prompt=n_prompt)
    seq_of = lambda i: jnp.minimum(i // nt, bsz - 1)
    return pl.pallas_call(
        kern,
        grid=(n_prompt + n_tail,),
        in_specs=[pl.BlockSpec((tl, d), lambda i: (jnp.minimum(i, n_prompt - 1), 0)),
                  pl.BlockSpec((tl, d), lambda i: (jnp.maximum(i - n_prompt, 0), 0))]
        + [_full_spec(a.shape) for a in wts],
        out_specs=[
            pl.BlockSpec((tl, d), lambda i: (i, 0)),
            pl.BlockSpec((1, hl, dc), lambda i: (seq_of(i), 0, 0)),
            pl.BlockSpec((1, GLA_HEADS, hk, hv), lambda i: (seq_of(i), 0, 0, 0)),
        ],
        out_shape=[
            jax.ShapeDtypeStruct(((n_prompt + n_tail) * tl, d), F32),
            jax.ShapeDtypeStruct((bsz, hl, dc), F32),
            jax.ShapeDtypeStruct((bsz, GLA_HEADS, hk, hv), F32),
        ],
        scratch_shapes=[
            pltpu.VMEM((HIST_PAD + tl, dc), F32),
            pltpu.VMEM((tl, dk), BF16), pltpu.VMEM((tl, dk), BF16), pltpu.VMEM((tl, dk), BF16),
            pltpu.VMEM((tl, dv), BF16),
            pltpu.VMEM((tl, dv), F32), pltpu.VMEM((tl, 2 * d), F32),
            pltpu.VMEM((tl, dv), F32), pltpu.VMEM((tl, dc), F32),
            pltpu.VMEM((GLA_HEADS, hk, hv), F32), pltpu.VMEM((dk, 128), F32),
        ],
        compiler_params=pltpu.CompilerParams(
            dimension_semantics=("arbitrary",), vmem_limit_bytes=VMEM_LIMIT),
        name="mixer_prompt",
    )(x, x1_tail, *wts)


def _mixer_sample(x, in_blk, hist, state, wts, dims, ns, ls):
    d = x.shape[1]
    dc, dk, dv, hk, hv = dims["dc"], dims["dk"], dims["dv"], dims["hk"], dims["hv"]
    hl = CONV_W - 1
    t = ns * ls
    kern = functools.partial(_mixer_sample_kernel, dims=dims, ns=ns, ls=ls)
    x1, hist_o, state_o = pl.pallas_call(
        kern,
        grid=(ns,),
        in_specs=[
            pl.BlockSpec((t, d), lambda i: (in_blk, 0)),
            pl.BlockSpec((ns, hl, dc), lambda i: (0, 0, 0)),
            pl.BlockSpec((1, GLA_HEADS, hk, hv), lambda i: (i, 0, 0, 0)),
        ] + [_full_spec(a.shape) for a in wts],
        out_specs=[
            pl.BlockSpec((t, d), lambda i: (0, 0)),
            pl.BlockSpec((ns, hl, dc), lambda i: (0, 0, 0)),
            pl.BlockSpec((1, GLA_HEADS, hk, hv), lambda i: (i, 0, 0, 0)),
        ],
        out_shape=[
            jax.ShapeDtypeStruct((t, d), F32),
            jax.ShapeDtypeStruct((ns, hl, dc), F32),
            jax.ShapeDtypeStruct((ns, GLA_HEADS, hk, hv), F32),
        ],
        scratch_shapes=[
            pltpu.VMEM((ns, HIST_PAD + ls, dc), F32),
            pltpu.VMEM((t, dk), F32), pltpu.VMEM((t, dk), F32), pltpu.VMEM((t, dv), F32),
            pltpu.VMEM((t, dk), F32), pltpu.VMEM((t, dv), F32), pltpu.VMEM((t, dc), F32),
        ],
        compiler_params=pltpu.CompilerParams(
            dimension_semantics=("arbitrary",), vmem_limit_bytes=VMEM_LIMIT),
        name="mixer_sample",
    )(x, hist, state, *wts)
    return x1, hist_o, state_o


def _token_out(t, d, tm, split):
    if split is None:
        return [pl.BlockSpec((tm, d), lambda i: (i, 0))], [jax.ShapeDtypeStruct((t, d), F32)]
    tp, ts = split
    npt = tp // tm
    specs = [pl.BlockSpec((tm, d), lambda i: (jnp.minimum(i, npt - 1), 0)),
             pl.BlockSpec((tm, d), lambda i: (jnp.maximum(i - npt, 0), 0))]
    return specs, [jax.ShapeDtypeStruct((tp, d), F32), jax.ShapeDtypeStruct((ts, d), F32)]


def _token_store(o_refs, val, npt):
    if len(o_refs) == 1:
        o_refs[0][...] = val
        return
    i = pl.program_id(0)

    @pl.when(i < npt)
    def _():
        o_refs[0][...] = val

    @pl.when(i >= npt)
    def _():
        o_refs[1][...] = val


def _ffn_dense_kernel(x_ref, wg_ref, wu_ref, wd_ref, g_ref, b_ref, *o_refs, alpha, npt):
    x = x_ref[...]
    xb = x.astype(BF16)
    h = _silu(_mm(xb, wg_ref[...])) * _mm(xb, wu_ref[...])
    f = _mm(h.astype(BF16), wd_ref[...])
    _token_store(o_refs, _layer_norm(alpha * x + f, g_ref[...], b_ref[...]), npt)


def _ffn_dense(x, wg, wu, wd, g, b, alpha, tm, split):
    t, d = x.shape
    ops = (wg.astype(BF16), wu.astype(BF16), wd.astype(BF16), g.reshape(1, d), b.reshape(1, d))
    out_specs, out_shape = _token_out(t, d, tm, split)
    npt = None if split is None else split[0] // tm
    return pl.pallas_call(
        functools.partial(_ffn_dense_kernel, alpha=alpha, npt=npt),
        grid=(t // tm,),
        in_specs=[pl.BlockSpec((tm, d), lambda i: (i, 0))] + [_full_spec(a.shape) for a in ops],
        out_specs=out_specs,
        out_shape=out_shape,
        compiler_params=pltpu.CompilerParams(
            dimension_semantics=("arbitrary",), vmem_limit_bytes=VMEM_LIMIT),
        name="ffn_dense",
    )(x, *ops)


def _router_kernel(x_ref, wr_ref, route_ref, cnt_ref, carry, *, ne, tr):
    i = pl.program_id(0)

    @pl.when(i == 0)
    def _():
        carry[...] = jnp.zeros(carry.shape, F32)

    xh = x_ref[...]
    x_hi = xh.astype(BF16)
    x_lo = (xh - x_hi.astype(F32)).astype(BF16)
    wr = wr_ref[...]
    w_hi = wr.astype(BF16)
    w_lo = (wr - w_hi.astype(F32)).astype(BF16)
    nt = (((1,), (1,)), ((), ()))
    logits = (lax.dot_general(w_hi, x_hi, nt, preferred_element_type=F32)
              + lax.dot_general(w_hi, x_lo, nt, preferred_element_type=F32)
              + lax.dot_general(w_lo, x_hi, nt, preferred_element_type=F32))
    mx = jnp.max(logits, axis=0, keepdims=True)
    ex = jnp.exp(logits - mx)
    probs = ex / jnp.sum(ex, axis=0, keepdims=True)
    eid = lax.broadcasted_iota(jnp.int32, (ne, tr), 0)
    p1 = jnp.max(probs, axis=0, keepdims=True)
    i1 = jnp.min(jnp.where(probs == p1, eid, ne), axis=0, keepdims=True)
    rest = jnp.where(eid == i1, -1.0, probs)
    p2 = jnp.max(rest, axis=0, keepdims=True)
    i2 = jnp.min(jnp.where(rest == p2, eid, ne), axis=0, keepdims=True)
    den = p1 + p2
    oh1 = (eid == i1).astype(F32)
    oh2 = (eid == i2).astype(F32)
    oh = oh1 + oh2
    ri = lax.broadcasted_iota(jnp.int32, (tr, tr), 0)
    ci = lax.broadcasted_iota(jnp.int32, (tr, tr), 1)
    upper = (ri <= ci).astype(BF16)
    incl = _mm(oh.astype(BF16), upper)
    before = carry[:, 0:1] + incl - oh
    r1 = jnp.sum(oh1 * before, axis=0, keepdims=True)
    r2 = jnp.sum(oh2 * before, axis=0, keepdims=True)
    zero = jnp.zeros((1, tr), F32)
    route_ref[...] = jnp.concatenate(
        [i1.astype(F32), i2.astype(F32), p1 / den, p2 / den, r1, r2, zero, zero], axis=0)
    total = carry[:, 0:1] + incl[:, tr - 1:tr]
    carry[...] = jnp.broadcast_to(total, carry.shape)
    cnt_ref[...] = jnp.broadcast_to(total, cnt_ref.shape)


def _router(x, w_router, tr):
    t, d = x.shape
    ne = w_router.shape[1]
    return pl.pallas_call(
        functools.partial(_router_kernel, ne=ne, tr=tr),
        grid=(t // tr,),
        in_specs=[pl.BlockSpec((tr, d), lambda i: (i, 0)), _full_spec((ne, d))],
        out_specs=[pl.BlockSpec((8, tr), lambda i: (0, i)), pl.BlockSpec((ne, 128), lambda i: (0, 0))],
        out_shape=[jax.ShapeDtypeStruct((8, t), F32), jax.ShapeDtypeStruct((ne, 128), F32)],
        scratch_shapes=[pltpu.VMEM((ne, 128), F32)],
        compiler_params=pltpu.CompilerParams(dimension_semantics=("arbitrary",)),
        name="moe_router",
    )(x, w_router.T.astype(F32))


def _dispatch_kernel(pos_ref, x_ref, zeros_ref, xs_ref, sem, *, tm):
    del zeros_ref

    def row_copy(r, k):
        return pltpu.make_async_copy(x_ref.at[pl.ds(r, 1), :], xs_ref.at[pl.ds(pos_ref[0, 0, 2 * r + k], 1), :], sem)

    def start(g, carry):
        for u in range(ROW_DMA_UNROLL):
            row_copy(g * ROW_DMA_UNROLL + u, 0).start()
            row_copy(g * ROW_DMA_UNROLL + u, 1).start()
        return carry

    lax.fori_loop(0, tm // ROW_DMA_UNROLL, start, 0)
    for _ in range(TOP_K):
        pltpu.make_async_copy(x_ref, xs_ref.at[pl.ds(0, tm), :], sem).wait()


def _dispatch(x, pos, n_rows, tm):
    t, d = x.shape
    zeros = jnp.zeros((n_rows, d), F32)
    return pl.pallas_call(
        functools.partial(_dispatch_kernel, tm=tm),
        grid=(t // tm,),
        in_specs=[
            pl.BlockSpec((1, 1, 2 * tm), lambda i: (i, 0, 0), memory_space=pltpu.SMEM),
            pl.BlockSpec((tm, d), lambda i: (i, 0)),
            pl.BlockSpec(memory_space=pl.ANY),
        ],
        out_specs=pl.BlockSpec(memory_space=pl.ANY),
        out_shape=jax.ShapeDtypeStruct((n_rows, d), F32),
        scratch_shapes=[pltpu.SemaphoreType.DMA],
        input_output_aliases={2: 0},
        compiler_params=pltpu.CompilerParams(dimension_semantics=("arbitrary",), has_side_effects=True),
        name="moe_dispatch",
    )(pos.reshape(t // tm, 1, 2 * tm), x, zeros)


def _experts_kernel(te_ref, nv_ref, xs_ref, wg_ref, wu_ref, wd_ref, ys_ref):
    i = pl.program_id(0)

    @pl.when(i < nv_ref[0])
    def _():
        xb = xs_ref[...].astype(BF16)
        h = _silu(_mm(xb, wg_ref[0])) * _mm(xb, wu_ref[0])
        ys_ref[...] = _mm(h.astype(BF16), wd_ref[0])

    @pl.when(i >= nv_ref[0])
    def _():
        ys_ref[...] = jnp.zeros(ys_ref.shape, F32)


def _experts(xs, tile_expert, n_valid, wg, wu, wd, tm):
    n_rows, d = xs.shape
    ne, _, ff = wg.shape
    grid_spec = pltpu.PrefetchScalarGridSpec(
        num_scalar_prefetch=2,
        grid=(n_rows // tm,),
        in_specs=[
            pl.BlockSpec((tm, d), lambda i, te, nv: (i, 0)),
            pl.BlockSpec((1, d, ff), lambda i, te, nv: (te[i], 0, 0)),
            pl.BlockSpec((1, d, ff), lambda i, te, nv: (te[i], 0, 0)),
            pl.BlockSpec((1, ff, d), lambda i, te, nv: (te[i], 0, 0)),
        ],
        out_specs=pl.BlockSpec((tm, d), lambda i, te, nv: (i, 0)),
    )
    return pl.pallas_call(
        _experts_kernel,
        grid_spec=grid_spec,
        out_shape=jax.ShapeDtypeStruct((n_rows, d), F32),
        compiler_params=pltpu.CompilerParams(
            dimension_semantics=("arbitrary",), vmem_limit_bytes=VMEM_LIMIT),
        name="moe_experts",
    )(tile_expert, n_valid, xs, wg, wu, wd)


def _combine_kernel(pos_ref, x_ref, rt_ref, ys_ref, g_ref, b_ref, *rest, tm, alpha, npt):
    o_refs, (buf, sem) = rest[:-2], rest[-2:]

    def row_copy(r, k):
        return pltpu.make_async_copy(ys_ref.at[pl.ds(pos_ref[0, 0, 2 * r + k], 1), :], buf.at[k, pl.ds(r, 1), :], sem)

    def start(g, carry):
        for u in range(ROW_DMA_UNROLL):
            row_copy(g * ROW_DMA_UNROLL + u, 0).start()
            row_copy(g * ROW_DMA_UNROLL + u, 1).start()
        return carry

    lax.fori_loop(0, tm // ROW_DMA_UNROLL, start, 0)
    for k in range(TOP_K):
        pltpu.make_async_copy(ys_ref.at[pl.ds(0, tm), :], buf.at[k], sem).wait()
    rt = rt_ref[...]
    f = rt[:, 2:3] * buf[0] + rt[:, 3:4] * buf[1]
    _token_store(o_refs, _layer_norm(alpha * x_ref[...] + f, g_ref[...], b_ref[...]), npt)


def _combine(x, pos, route_t, ys, g, b, alpha, tm, split):
    t, d = x.shape
    out_specs, out_shape = _token_out(t, d, tm, split)
    npt = None if split is None else split[0] // tm
    return pl.pallas_call(
        functools.partial(_combine_kernel, tm=tm, alpha=alpha, npt=npt),
        grid=(t // tm,),
        in_specs=[
            pl.BlockSpec((1, 1, 2 * tm), lambda i: (i, 0, 0), memory_space=pltpu.SMEM),
            pl.BlockSpec((tm, d), lambda i: (i, 0)),
            pl.BlockSpec((tm, 8), lambda i: (i, 0)),
            pl.BlockSpec(memory_space=pl.ANY),
            _full_spec((1, d)), _full_spec((1, d)),
        ],
        out_specs=out_specs,
        out_shape=out_shape,
        scratch_shapes=[pltpu.VMEM((2, tm, d), F32), pltpu.SemaphoreType.DMA],
        compiler_params=pltpu.CompilerParams(dimension_semantics=("arbitrary",)),
        name="moe_combine",
    )(pos.reshape(t // tm, 1, 2 * tm), x, route_t, ys, g.reshape(1, d), b.reshape(1, d))


def _ffn_moe(x, w_router, wg, wu, wd, g, b, alpha, tr, tm, split):
    t, d = x.shape
    ne = w_router.shape[1]
    route, counts = _router(x, w_router, tr)
    cnt = counts[:, 0].astype(jnp.int32)
    gsz = ((cnt + tm - 1) // tm) * tm
    ends = jnp.cumsum(gsz)
    offs = ends - gsz
    n_tiles = (TOP_K * t) // tm + ne
    n_rows = n_tiles * tm
    tile_start = jnp.arange(n_tiles, dtype=jnp.int32) * tm
    tile_e = jnp.sum((tile_start[:, None] >= ends[None, :]).astype(jnp.int32), axis=1)
    n_valid = (ends[ne - 1] // tm).astype(jnp.int32).reshape(1)
    last_e = jnp.sum((ends[ne - 1] - 1 >= ends).astype(jnp.int32))
    tile_e = jnp.minimum(tile_e, last_e).astype(jnp.int32)
    i12 = route[0:2].astype(jnp.int32)
    base = sum(jnp.where(i12 == e, offs[e], 0) for e in range(ne))
    pos = (base + route[4:6].astype(jnp.int32)).T.reshape(-1)
    xs = _dispatch(x, pos, n_rows, tr)
    ys = _experts(xs, tile_e, n_valid, wg.astype(BF16), wu.astype(BF16), wd.astype(BF16), tm)
    return _combine(x, pos, route.T, ys, g, b, alpha, tr, split)


def kernel(x_prompt, x_sample, cache_conv, state_gla, w_in, b_in, conv_w, conv_b, conv_ln_g, conv_ln_b, w_conv_out, b_conv_out, w_gate_up, b_gate, gla_norm_g, w_gla_out, w_o, ln1_g, ln1_b, ln2_g, ln2_b, ff_w_gate, ff_w_up, ff_w_down, w_router, moe_w_gate, moe_w_up, moe_w_down):
    p = dict(w_in=w_in, b_in=b_in, conv_w=conv_w, conv_b=conv_b, conv_ln_g=conv_ln_g, conv_ln_b=conv_ln_b,
             w_conv_out=w_conv_out, b_conv_out=b_conv_out, w_gate_up=w_gate_up, b_gate=b_gate,
             gla_norm_g=gla_norm_g, w_gla_out=w_gla_out, w_o=w_o, ln1_g=ln1_g, ln1_b=ln1_b)
    depth = w_in.shape[0]
    bsz, seq, d = x_prompt.shape
    ns, ls, _ = x_sample.shape
    dc = conv_w.shape[-1]
    rank, dk = w_gate_up.shape[1], w_gate_up.shape[2]
    dv = w_gla_out.shape[1]
    dims = dict(d=d, dc=dc, dk=dk, dv=dv, rank=rank, hk=dk // GLA_HEADS, hv=dv // GLA_HEADS,
                alpha=(2.0 * depth) ** 0.25)
    alpha = dims["alpha"]
    tl = min(512, seq)
    tp = bsz * seq
    ts = ns * ls
    tm = min(512, ts)
    assert seq % tl == 0 and tp % ts == 0 and ts % tm == 0 and ts % tl == 0

    x_p, x_s, s_blk = x_prompt.reshape(tp, d), x_sample.reshape(ts, d), 0
    hist_p, state_p, hist_s, state_s = [], [], [], []
    for l in range(depth):
        wts = _mixer_weights(l, p, dims)
        x1s, hs, ss = _mixer_sample(x_s, s_blk, cache_conv[l], state_gla[l], wts, dims, ns, ls)
        x1, hp, sp = _mixer_prompt(x_p, x1s, wts, dims, tl, bsz, seq)
        hist_p.append(hp), state_p.append(sp), hist_s.append(hs), state_s.append(ss)
        split = (tp, ts) if l == depth - 1 else None
        if l % 2 == 0:
            x2 = _ffn_dense(x1, ff_w_gate[l // 2], ff_w_up[l // 2], ff_w_down[l // 2], ln2_g[l], ln2_b[l],
                            alpha, tm, split)
        else:
            x2 = _ffn_moe(x1, w_router[l // 2], moe_w_gate[l // 2], moe_w_up[l // 2], moe_w_down[l // 2],
                          ln2_g[l], ln2_b[l], alpha, tm, tm, split)
        if split is None:
            x_p, x_s, s_blk = x2[0], x2[0], tp // ts
    y_p, y_s = x2
    return (y_p.reshape(bsz, seq, d), y_s.reshape(ns, ls, d), jnp.stack(hist_p),
            jnp.stack(state_p).astype(state_gla.dtype), jnp.stack(hist_s), jnp.stack(state_s).astype(state_gla.dtype))
```

```python
import functools

import jax
import jax.numpy as jnp
from jax import lax
from jax.experimental import pallas as pl
from jax.experimental.pallas import tpu as pltpu

CHUNK = 64
CONV_W = 31
GLA_HEADS = 4
GATE_TAU = 16.0
LN_EPS = 1e-5
RMS_EPS = 1e-6
TOP_K = 2

HIST_PAD = 32
CONV_ROWS = 32
ROW_DMA_UNROLL = 8
SEQ_PER_STEP = 4
VMEM_LIMIT = 56 * 1024 * 1024

BF16 = jnp.bfloat16
F32 = jnp.float32


def _mm(a, b):
    return jnp.dot(a, b, preferred_element_type=F32)


def _sigmoid(x):
    return 0.5 * jnp.tanh(0.5 * x) + 0.5


def _silu(x):
    return x * _sigmoid(x)


def _log_sigmoid(z):
    return -(jnp.maximum(-z, 0.0) + jnp.log(1.0 + jnp.exp(-jnp.abs(z))))


def _layer_norm(x, g, b):
    mu = jnp.mean(x, axis=-1, keepdims=True)
    xc = x - mu
    var = jnp.mean(xc * xc, axis=-1, keepdims=True)
    return xc * lax.rsqrt(var + LN_EPS) * g + b


def _split3(x):
    hi = x.astype(BF16)
    r1 = x - hi.astype(F32)
    mid = r1.astype(BF16)
    lo = (r1 - mid.astype(F32)).astype(BF16)
    return hi, mid, lo


def _full_spec(shape):
    zeros = (0,) * len(shape)
    return pl.BlockSpec(shape, lambda *_: zeros, pipeline_mode=pl.Buffered(1))


def _project(xb, w, dims, u_store, q_s, k_s, v_s, lf_s):
    dc, dk, dv = dims["dc"], dims["dk"], dims["dv"]
    off_q, off_k, off_v = 2 * dc, 2 * dc + dk, 2 * dc + 2 * dk
    off_g = off_v + dv
    glu = _mm(xb, w["wmain"][:, 0:off_q]) + w["bmain"][:, 0:off_q]
    u_store(glu[:, 0:dc] * _sigmoid(glu[:, dc:off_q]))
    q_s[...] = (_mm(xb, w["wmain"][:, off_q:off_k]) + w["bmain"][:, off_q:off_k]) * (dims["hk"] ** -0.5)
    k_s[...] = _mm(xb, w["wmain"][:, off_k:off_v]) + w["bmain"][:, off_k:off_v]
    v_s[...] = _mm(xb, w["wmain"][:, off_v:off_g]) + w["bmain"][:, off_v:off_g]
    lr = _mm(xb, w["wlr"][...]) + w["blr"][...]
    z = _mm(lr.astype(BF16), w["wgu"][...]) + w["bgu"][...]
    lf_s[...] = _log_sigmoid(z) * (1.0 / GATE_TAU)


def _gla_chunk(q_s, k_s, v_s, lf_s, o_s, rows, c, get_state, set_state, dims):
    hk, hv, dk = dims["hk"], dims["hv"], dims["dk"]
    lf = lf_s[rows, :]
    ri = lax.broadcasted_iota(jnp.int32, (c, c), 0)
    ci = lax.broadcasted_iota(jnp.int32, (c, c), 1)
    causal = ri >= ci
    tri = causal.astype(BF16)
    hi, mid, lo = _split3(lf)
    b = _mm(tri, hi) + _mm(tri, mid) + _mm(tri, lo)
    b_last = b[c - 1:c, :]
    q_t = q_s[rows, :] * jnp.exp(b)
    kk = k_s[rows, :]
    k_t = kk * jnp.exp(-b)
    k_e = kk * jnp.exp(b_last - b)
    d_t = jnp.transpose(jnp.broadcast_to(jnp.exp(b_last), (128, dk)))
    vv = v_s[rows, :]
    for h in range(GLA_HEADS):
        ks = slice(h * hk, (h + 1) * hk)
        vs = slice(h * hv, (h + 1) * hv)
        qh = q_t[:, ks].astype(BF16)
        kh = k_t[:, ks].astype(BF16)
        keh = k_e[:, ks].astype(BF16)
        vh = vv[:, vs].astype(BF16)
        att = lax.dot_general(qh, kh, (((1,), (1,)), ((), ())), preferred_element_type=F32)
        att = jnp.where(causal, att, 0.0)
        s_old = get_state(h)
        o_s[rows, vs] = _mm(qh, s_old.astype(BF16)) + _mm(att.astype(BF16), vh)
        upd = lax.dot_general(keh, vh, (((0,), (0,)), ((), ())), preferred_element_type=F32)
        dcol = d_t[ks, :]
        decay = jnp.concatenate([dcol] * (hv // 128), axis=1)
        set_state(h, decay * s_old + upd)


def _conv_rows(load, w8_ref, n):
    base = HIST_PAD - (CONV_W - 1)
    acc = None
    for j in range(CONV_W):
        xw = load(base + j)
        term = w8_ref[8 * j:8 * j + 8, :][None] * xw.reshape(n // 8, 8, xw.shape[-1])
        acc = term if acc is None else acc + term
    return acc.reshape(n, acc.shape[-1])


def _tail(x, xb, c, w, dims, o_s):
    d, dc, dk, dv, hv = dims["d"], dims["dc"], dims["dk"], dims["dv"], dims["hv"]
    off_g = 2 * dc + 2 * dk + dv
    off_lr = off_g + dv
    c = _silu(_layer_norm(c + w["convb"][...], w["clng"][...], w["clnb"][...]))
    y_a = _mm(c.astype(BF16), w["wco"][...]) + w["bco"][...]
    g_out = _mm(xb, w["wmain"][:, off_g:off_lr]) + w["bmain"][:, off_g:off_lr]
    heads = []
    for h in range(GLA_HEADS):
        oh = o_s[:, h * hv:(h + 1) * hv]
        ms = jnp.mean(oh * oh, axis=-1, keepdims=True)
        heads.append(oh * lax.rsqrt(ms + RMS_EPS) * w["gng"][...])
    o = jnp.concatenate(heads, axis=1) * _silu(g_out)
    y_b = _mm(o.astype(BF16), w["wgo"][...])
    gates = _sigmoid(_mm(xb, w["wgates"][...]) + w["bgates"][...])
    merged = gates[:, 0:d] * y_a + gates[:, d:2 * d] * y_b
    m = _mm(merged.astype(BF16), w["wo"][...])
    return _layer_norm(dims["alpha"] * x + m, w["ln1g"][...], w["ln1b"][...])


_W_NAMES = ("wmain", "bmain", "wlr", "blr", "wgates", "bgates", "convw", "convb", "clng", "clnb",
            "wco", "bco", "wgu", "bgu", "gng", "wgo", "wo", "ln1g", "ln1b")


def _mixer_prompt_kernel(*refs, dims, tl, nt, n_prompt):
    nw = len(_W_NAMES)
    x_ref, tail_ref = refs[0:2]
    w = dict(zip(_W_NAMES, refs[2:2 + nw]))
    x1_ref, hist_ref, state_ref = refs[2 + nw:5 + nw]
    scratch = refs[5 + nw:]
    i = pl.program_id(0)

    @pl.when(i < n_prompt)
    def _():
        _mixer_prompt_tile(i % nt, x_ref, w, x1_ref, hist_ref, state_ref, scratch, dims, tl, nt)

    @pl.when(i >= n_prompt)
    def _():
        x1_ref[...] = tail_ref[...]


def _mixer_prompt_tile(j, x_ref, w, x1_ref, hist_ref, state_ref, scratch, dims, tl, nt):
    ubuf, qt_s, kt_s, ke_s, vb_s, g_s, gt_s, o_s, c_s, s_s, dt_s = scratch
    d, dc, dk, dv, hk, hv = dims["d"], dims["dc"], dims["dk"], dims["dv"], dims["hk"], dims["hv"]
    off_q, off_k, off_v = 2 * dc, 2 * dc + dk, 2 * dc + 2 * dk
    off_g = off_v + dv
    off_lr = off_g + dv
    nch = tl // CHUNK
    rep = 128 // nch
    wm, bm = w["wmain"], w["bmain"]

    @pl.when(j == 0)
    def _():
        ubuf[0:HIST_PAD, :] = jnp.zeros((HIST_PAD, dc), F32)
        s_s[...] = jnp.zeros(s_s.shape, F32)

    x = x_ref[...]
    xb = x.astype(BF16)

    glu = _mm(xb, wm[:, 0:off_q]) + bm[:, 0:off_q]
    ubuf[HIST_PAD:HIST_PAD + tl, :] = glu[:, 0:dc] * _sigmoid(glu[:, dc:off_q])
    lr = _mm(xb, w["wlr"][...]) + w["blr"][...]
    z = _mm(lr.astype(BF16), w["wgu"][...]) + w["bgu"][...]
    lf = _log_sigmoid(z) * (1.0 / GATE_TAU)
    ri = lax.broadcasted_iota(jnp.int32, (CHUNK, CHUNK), 0)
    ci = lax.broadcasted_iota(jnp.int32, (CHUNK, CHUNK), 1)
    causal = ri >= ci
    tri = causal.astype(BF16)
    parts = _split3(lf)
    b = jnp.concatenate(
        [sum(_mm(tri, p[c * CHUNK:(c + 1) * CHUNK, :]) for p in parts) for c in range(nch)], axis=0)
    b3 = b.reshape(nch, CHUNK, dk)
    b_last = b3[:, CHUNK - 1:CHUNK, :]
    q = (_mm(xb, wm[:, off_q:off_k]) + bm[:, off_q:off_k]) * (hk ** -0.5)
    qt_s[...] = (q * jnp.exp(b)).astype(BF16)
    kk = _mm(xb, wm[:, off_k:off_v]) + bm[:, off_k:off_v]
    kt_s[...] = (kk * jnp.exp(-b)).astype(BF16)
    ke_s[...] = (kk.reshape(nch, CHUNK, dk) * jnp.exp(b_last - b3)).reshape(tl, dk).astype(BF16)
    vb_s[...] = (_mm(xb, wm[:, off_v:off_g]) + bm[:, off_v:off_g]).astype(BF16)
    d_rows = jnp.broadcast_to(jnp.exp(b_last), (nch, rep, dk)).reshape(nch * rep, dk)
    if nch * rep < 128:
        d_rows = jnp.concatenate([d_rows, jnp.zeros((128 - nch * rep, dk), F32)], axis=0)
    dt_s[...] = jnp.transpose(d_rows)
    g_s[...] = _silu(_mm(xb, wm[:, off_g:off_lr]) + bm[:, off_g:off_lr])

    states = [s_s[h] for h in range(GLA_HEADS)]
    conv_per = tl // CONV_ROWS // nch
    gw = 2 * d // nch
    for c in range(nch):
        rows = slice(c * CHUNK, (c + 1) * CHUNK)
        for h in range(GLA_HEADS):
            ks = slice(h * hk, (h + 1) * hk)
            vs = slice(h * hv, (h + 1) * hv)
            qh, kh, keh, vh = qt_s[rows, ks], kt_s[rows, ks], ke_s[rows, ks], vb_s[rows, vs]
            att = lax.dot_general(qh, kh, (((1,), (1,)), ((), ())), preferred_element_type=F32)
            att = jnp.where(causal, att, 0.0)
            s_old = states[h]
            o_s[rows, vs] = _mm(qh, s_old.astype(BF16)) + _mm(att.astype(BF16), vh)
            upd = lax.dot_general(keh, vh, (((0,), (0,)), ((), ())), preferred_element_type=F32)
            decay = jnp.broadcast_to(dt_s[ks, c * rep:c * rep + 1], (hk, hv))
            states[h] = decay * s_old + upd
        for bi in range(c * conv_per, (c + 1) * conv_per):
            r0 = bi * CONV_ROWS
            c_s[r0:r0 + CONV_ROWS, :] = _conv_rows(
                lambda off: ubuf[r0 + off:r0 + off + CONV_ROWS, :], w["convw"], CONV_ROWS)
        cols = slice(c * gw, (c + 1) * gw)
        gt_s[:, cols] = _sigmoid(_mm(xb, w["wgates"][:, cols]) + w["bgates"][:, cols])
    for h in range(GLA_HEADS):
        s_s[h] = states[h]

    cv = _silu(_layer_norm(c_s[...] + w["convb"][...], w["clng"][...], w["clnb"][...]))
    y_a = _mm(cv.astype(BF16), w["wco"][...]) + w["bco"][...]
    heads = []
    for h in range(GLA_HEADS):
        oh = o_s[:, h * hv:(h + 1) * hv]
        ms = jnp.mean(oh * oh, axis=-1, keepdims=True)
        heads.append(oh * lax.rsqrt(ms + RMS_EPS) * w["gng"][...])
    o = jnp.concatenate(heads, axis=1) * g_s[...]
    y_b = _mm(o.astype(BF16), w["wgo"][...])
    merged = gt_s[:, 0:d] * y_a + gt_s[:, d:2 * d] * y_b
    m = _mm(merged.astype(BF16), w["wo"][...])
    x1_ref[...] = _layer_norm(dims["alpha"] * x + m, w["ln1g"][...], w["ln1b"][...])

    @pl.when(j == nt - 1)
    def _():
        hist_ref[0] = ubuf[tl + HIST_PAD - (CONV_W - 1):tl + HIST_PAD, :]
        state_ref[0] = s_s[...]

    ubuf[0:HIST_PAD, :] = ubuf[tl:tl + HIST_PAD, :]


def _mixer_sample_kernel(*refs, dims, ns, ls):
    nw = len(_W_NAMES)
    x_ref, hist_in_ref, state_in_ref = refs[0:3]
    w = dict(zip(_W_NAMES, refs[3:3 + nw]))
    x1_ref, hist_ref, state_ref = refs[3 + nw:6 + nw]
    ubuf, q_s, k_s, v_s, lf_s, o_s, c_s = refs[6 + nw:]
    i = pl.program_id(0)
    dc = dims["dc"]
    hl = CONV_W - 1

    @pl.when(i == 0)
    def _():
        xb = x_ref[...].astype(BF16)
        ubuf[:, 0:8, :] = jnp.zeros((ns, 8, dc), F32)
        ubuf[:, HIST_PAD - hl:HIST_PAD, :] = hist_in_ref[...]

        def u_store(u):
            ubuf[:, HIST_PAD:HIST_PAD + ls, :] = u.reshape(ns, ls, dc)

        _project(xb, w, dims, u_store, q_s, k_s, v_s, lf_s)

    for s in range(SEQ_PER_STEP):
        seq = i * SEQ_PER_STEP + s
        rows = pl.ds(pl.multiple_of(seq * ls, ls), ls)

        def set_state(h, val, s=s):
            state_ref[s, h] = val

        _gla_chunk(q_s, k_s, v_s, lf_s, o_s, rows, ls, lambda h, s=s: state_in_ref[s, h], set_state, dims)
        c_s[rows, :] = _conv_rows(lambda off, seq=seq: ubuf[seq, pl.ds(off, ls), :], w["convw"], ls)
        hist_ref[seq] = ubuf[seq, pl.ds(HIST_PAD + ls - hl, hl), :]

    @pl.when(i == ns // SEQ_PER_STEP - 1)
    def _():
        x = x_ref[...]
        x1_ref[...] = _tail(x, x.astype(BF16), c_s[...], w, dims, o_s)


def _mixer_weights(l, p, dims):
    dc, dk, dv, rank = dims["dc"], dims["dk"], dims["dv"], dims["rank"]
    off_lr = 2 * dc + 2 * dk + 2 * dv
    off_gates = off_lr + rank
    w_in, b_in = p["w_in"][l], p["b_in"][l]
    row = lambda v: v.reshape(1, -1).astype(F32)
    return (
        w_in[:, :off_lr].astype(BF16), row(b_in[:off_lr]),
        w_in[:, off_lr:off_gates].astype(BF16), row(b_in[off_lr:off_gates]),
        w_in[:, off_gates:].astype(BF16), row(b_in[off_gates:]),
        jnp.repeat(p["conv_w"][l].astype(F32), 8, axis=0),
        row(p["conv_b"][l]), row(p["conv_ln_g"][l]), row(p["conv_ln_b"][l]),
        p["w_conv_out"][l].astype(BF16), row(p["b_conv_out"][l]),
        p["w_gate_up"][l].astype(BF16), row(p["b_gate"][l]),
        row(p["gla_norm_g"][l]),
        p["w_gla_out"][l].astype(BF16), p["w_o"][l].astype(BF16),
        row(p["ln1_g"][l]), row(p["ln1_b"][l]),
    )


def _mixer_prompt(x, x1_tail, wts, dims, tl, bsz, seq):
    d = x.shape[1]
    dc, dk, dv, hk, hv = dims["dc"], dims["dk"], dims["dv"], dims["hk"], dims["hv"]
    hl = CONV_W - 1
    nt = seq // tl
    n_prompt = bsz * nt
    n_tail = x1_tail.shape[0] // tl
    kern = functools.partial(_mixer_prompt_kernel, dims=dims, tl=tl, nt=nt, n_prompt=n_prompt)
    seq_of = lambda i: jnp.minimum(i // nt, bsz - 1)
    return pl.pallas_call(
        kern,
        grid=(n_prompt + n_tail,),
        in_specs=[pl.BlockSpec((tl, d), lambda i: (jnp.minimum(i, n_prompt - 1), 0)),
                  pl.BlockSpec((tl, d), lambda i: (jnp.maximum(i - n_prompt, 0), 0))]
        + [_full_spec(a.shape) for a in wts],
        out_specs=[
            pl.BlockSpec((tl, d), lambda i: (i, 0)),
            pl.BlockSpec((1, hl, dc), lambda i: (seq_of(i), 0, 0)),
            pl.BlockSpec((1, GLA_HEADS, hk, hv), lambda i: (seq_of(i), 0, 0, 0)),
        ],
        out_shape=[
            jax.ShapeDtypeStruct(((n_prompt + n_tail) * tl, d), F32),
            jax.ShapeDtypeStruct((bsz, hl, dc), F32),
            jax.ShapeDtypeStruct((bsz, GLA_HEADS, hk, hv), F32),
        ],
        scratch_shapes=[
            pltpu.VMEM((HIST_PAD + tl, dc), F32),
            pltpu.VMEM((tl, dk), BF16), pltpu.VMEM((tl, dk), BF16), pltpu.VMEM((tl, dk), BF16),
            pltpu.VMEM((tl, dv), BF16),
            pltpu.VMEM((tl, dv), F32), pltpu.VMEM((tl, 2 * d), F32),
            pltpu.VMEM((tl, dv), F32), pltpu.VMEM((tl, dc), F32),
            pltpu.VMEM((GLA_HEADS, hk, hv), F32), pltpu.VMEM((dk, 128), F32),
        ],
        compiler_params=pltpu.CompilerParams(
            dimension_semantics=("arbitrary",), vmem_limit_bytes=VMEM_LIMIT),
        name="mixer_prompt",
    )(x, x1_tail, *wts)


def _mixer_sample(x, in_blk, hist, state, wts, dims, ns, ls):
    d = x.shape[1]
    dc, dk, dv, hk, hv = dims["dc"], dims["dk"], dims["dv"], dims["hk"], dims["hv"]
    hl = CONV_W - 1
    t = ns * ls
    kern = functools.partial(_mixer_sample_kernel, dims=dims, ns=ns, ls=ls)
    x1, hist_o, state_o = pl.pallas_call(
        kern,
        grid=(ns // SEQ_PER_STEP,),
        in_specs=[
            pl.BlockSpec((t, d), lambda i: (in_blk, 0)),
            pl.BlockSpec((ns, hl, dc), lambda i: (0, 0, 0)),
            pl.BlockSpec((SEQ_PER_STEP, GLA_HEADS, hk, hv), lambda i: (i, 0, 0, 0)),
        ] + [_full_spec(a.shape) for a in wts],
        out_specs=[
            pl.BlockSpec((t, d), lambda i: (0, 0)),
            pl.BlockSpec((ns, hl, dc), lambda i: (0, 0, 0)),
            pl.BlockSpec((SEQ_PER_STEP, GLA_HEADS, hk, hv), lambda i: (i, 0, 0, 0)),
        ],
        out_shape=[
            jax.ShapeDtypeStruct((t, d), F32),
            jax.ShapeDtypeStruct((ns, hl, dc), F32),
            jax.ShapeDtypeStruct((ns, GLA_HEADS, hk, hv), F32),
        ],
        scratch_shapes=[
            pltpu.VMEM((ns, HIST_PAD + ls, dc), F32),
            pltpu.VMEM((t, dk), F32), pltpu.VMEM((t, dk), F32), pltpu.VMEM((t, dv), F32),
            pltpu.VMEM((t, dk), F32), pltpu.VMEM((t, dv), F32), pltpu.VMEM((t, dc), F32),
        ],
        compiler_params=pltpu.CompilerParams(
            dimension_semantics=("arbitrary",), vmem_limit_bytes=VMEM_LIMIT),
        name="mixer_sample",
    )(x, hist, state, *wts)
    return x1, hist_o, state_o


def _token_out(t, d, tm, split):
    if split is None:
        return [pl.BlockSpec((tm, d), lambda i: (i, 0))], [jax.ShapeDtypeStruct((t, d), F32)]
    tp, ts = split
    npt = tp // tm
    specs = [pl.BlockSpec((tm, d), lambda i: (jnp.minimum(i, npt - 1), 0)),
             pl.BlockSpec((tm, d), lambda i: (jnp.maximum(i - npt, 0), 0))]
    return specs, [jax.ShapeDtypeStruct((tp, d), F32), jax.ShapeDtypeStruct((ts, d), F32)]


def _token_store(o_refs, val, npt):
    if len(o_refs) == 1:
        o_refs[0][...] = val
        return
    i = pl.program_id(0)

    @pl.when(i < npt)
    def _():
        o_refs[0][...] = val

    @pl.when(i >= npt)
    def _():
        o_refs[1][...] = val


def _ffn_dense_kernel(x_ref, wg_ref, wu_ref, wd_ref, g_ref, b_ref, *o_refs, alpha, npt):
    x = x_ref[...]
    xb = x.astype(BF16)
    h = _silu(_mm(xb, wg_ref[...])) * _mm(xb, wu_ref[...])
    f = _mm(h.astype(BF16), wd_ref[...])
    _token_store(o_refs, _layer_norm(alpha * x + f, g_ref[...], b_ref[...]), npt)


def _ffn_dense(x, wg, wu, wd, g, b, alpha, tm, split):
    t, d = x.shape
    ops = (wg.astype(BF16), wu.astype(BF16), wd.astype(BF16), g.reshape(1, d), b.reshape(1, d))
    out_specs, out_shape = _token_out(t, d, tm, split)
    npt = None if split is None else split[0] // tm
    return pl.pallas_call(
        functools.partial(_ffn_dense_kernel, alpha=alpha, npt=npt),
        grid=(t // tm,),
        in_specs=[pl.BlockSpec((tm, d), lambda i: (i, 0))] + [_full_spec(a.shape) for a in ops],
        out_specs=out_specs,
        out_shape=out_shape,
        compiler_params=pltpu.CompilerParams(
            dimension_semantics=("arbitrary",), vmem_limit_bytes=VMEM_LIMIT),
        name="ffn_dense",
    )(x, *ops)


def _router_kernel(x_ref, wr_ref, route_ref, cnt_ref, carry, *, ne, tr):
    i = pl.program_id(0)

    @pl.when(i == 0)
    def _():
        carry[...] = jnp.zeros(carry.shape, F32)

    xh = x_ref[...]
    x_hi = xh.astype(BF16)
    x_lo = (xh - x_hi.astype(F32)).astype(BF16)
    wr = wr_ref[...]
    w_hi = wr.astype(BF16)
    w_lo = (wr - w_hi.astype(F32)).astype(BF16)
    nt = (((1,), (1,)), ((), ()))
    logits = (lax.dot_general(w_hi, x_hi, nt, preferred_element_type=F32)
              + lax.dot_general(w_hi, x_lo, nt, preferred_element_type=F32)
              + lax.dot_general(w_lo, x_hi, nt, preferred_element_type=F32))
    mx = jnp.max(logits, axis=0, keepdims=True)
    ex = jnp.exp(logits - mx)
    probs = ex / jnp.sum(ex, axis=0, keepdims=True)
    eid = lax.broadcasted_iota(jnp.int32, (ne, tr), 0)
    p1 = jnp.max(probs, axis=0, keepdims=True)
    i1 = jnp.min(jnp.where(probs == p1, eid, ne), axis=0, keepdims=True)
    rest = jnp.where(eid == i1, -1.0, probs)
    p2 = jnp.max(rest, axis=0, keepdims=True)
    i2 = jnp.min(jnp.where(rest == p2, eid, ne), axis=0, keepdims=True)
    den = p1 + p2
    oh1 = (eid == i1).astype(F32)
    oh2 = (eid == i2).astype(F32)
    oh = oh1 + oh2
    ri = lax.broadcasted_iota(jnp.int32, (tr, tr), 0)
    ci = lax.broadcasted_iota(jnp.int32, (tr, tr), 1)
    upper = (ri <= ci).astype(BF16)
    incl = _mm(oh.astype(BF16), upper)
    before = carry[:, 0:1] + incl - oh
    r1 = jnp.sum(oh1 * before, axis=0, keepdims=True)
    r2 = jnp.sum(oh2 * before, axis=0, keepdims=True)
    zero = jnp.zeros((1, tr), F32)
    route_ref[...] = jnp.concatenate(
        [i1.astype(F32), i2.astype(F32), p1 / den, p2 / den, r1, r2, zero, zero], axis=0)
    total = carry[:, 0:1] + incl[:, tr - 1:tr]
    carry[...] = jnp.broadcast_to(total, carry.shape)
    cnt_ref[...] = jnp.broadcast_to(total, cnt_ref.shape)


def _router(x, w_router, tr):
    t, d = x.shape
    ne = w_router.shape[1]
    return pl.pallas_call(
        functools.partial(_router_kernel, ne=ne, tr=tr),
        grid=(t // tr,),
        in_specs=[pl.BlockSpec((tr, d), lambda i: (i, 0)), _full_spec((ne, d))],
        out_specs=[pl.BlockSpec((8, tr), lambda i: (0, i)), pl.BlockSpec((ne, 128), lambda i: (0, 0))],
        out_shape=[jax.ShapeDtypeStruct((8, t), F32), jax.ShapeDtypeStruct((ne, 128), F32)],
        scratch_shapes=[pltpu.VMEM((ne, 128), F32)],
        compiler_params=pltpu.CompilerParams(dimension_semantics=("arbitrary",)),
        name="moe_router",
    )(x, w_router.T.astype(F32))


def _dispatch_kernel(pos_ref, x_ref, zeros_ref, xs_ref, sem, *, tm):
    del zeros_ref

    def row_copy(r, k):
        return pltpu.make_async_copy(x_ref.at[pl.ds(r, 1), :], xs_ref.at[pl.ds(pos_ref[0, 0, 2 * r + k], 1), :], sem)

    def start(g, carry):
        for u in range(ROW_DMA_UNROLL):
            row_copy(g * ROW_DMA_UNROLL + u, 0).start(priority=0)
            row_copy(g * ROW_DMA_UNROLL + u, 1).start(priority=1)
        return carry

    lax.fori_loop(0, tm // ROW_DMA_UNROLL, start, 0)
    for _ in range(TOP_K):
        pltpu.make_async_copy(x_ref, xs_ref.at[pl.ds(0, tm), :], sem).wait()


def _dispatch(x, pos, n_rows, tm):
    t, d = x.shape
    zeros = jnp.zeros((n_rows, d), F32)
    return pl.pallas_call(
        functools.partial(_dispatch_kernel, tm=tm),
        grid=(t // tm,),
        in_specs=[
            pl.BlockSpec((1, 1, 2 * tm), lambda i: (i, 0, 0), memory_space=pltpu.SMEM),
            pl.BlockSpec((tm, d), lambda i: (i, 0)),
            pl.BlockSpec(memory_space=pl.ANY),
        ],
        out_specs=pl.BlockSpec(memory_space=pl.ANY),
        out_shape=jax.ShapeDtypeStruct((n_rows, d), F32),
        scratch_shapes=[pltpu.SemaphoreType.DMA],
        input_output_aliases={2: 0},
        compiler_params=pltpu.CompilerParams(dimension_semantics=("arbitrary",), has_side_effects=True),
        name="moe_dispatch",
    )(pos.reshape(t // tm, 1, 2 * tm), x, zeros)


def _experts_kernel(te_ref, nv_ref, xs_ref, wg_ref, wu_ref, wd_ref, ys_ref):
    i = pl.program_id(0)

    @pl.when(i < nv_ref[0])
    def _():
        xb = xs_ref[...].astype(BF16)
        h = _silu(_mm(xb, wg_ref[0])) * _mm(xb, wu_ref[0])
        ys_ref[...] = _mm(h.astype(BF16), wd_ref[0])

    @pl.when(i >= nv_ref[0])
    def _():
        ys_ref[...] = jnp.zeros(ys_ref.shape, F32)


def _experts(xs, tile_expert, n_valid, wg, wu, wd, tm):
    n_rows, d = xs.shape
    ne, _, ff = wg.shape
    grid_spec = pltpu.PrefetchScalarGridSpec(
        num_scalar_prefetch=2,
        grid=(n_rows // tm,),
        in_specs=[
            pl.BlockSpec((tm, d), lambda i, te, nv: (i, 0)),
            pl.BlockSpec((1, d, ff), lambda i, te, nv: (te[i], 0, 0)),
            pl.BlockSpec((1, d, ff), lambda i, te, nv: (te[i], 0, 0)),
            pl.BlockSpec((1, ff, d), lambda i, te, nv: (te[i], 0, 0)),
        ],
        out_specs=pl.BlockSpec((tm, d), lambda i, te, nv: (i, 0)),
    )
    return pl.pallas_call(
        _experts_kernel,
        grid_spec=grid_spec,
        out_shape=jax.ShapeDtypeStruct((n_rows, d), F32),
        compiler_params=pltpu.CompilerParams(
            dimension_semantics=("arbitrary",), vmem_limit_bytes=VMEM_LIMIT),
        name="moe_experts",
    )(tile_expert, n_valid, xs, wg, wu, wd)


def _combine_kernel(pos_ref, x_ref, rt_ref, ys_ref, g_ref, b_ref, *rest, tm, alpha, npt):
    o_refs, (buf, sem) = rest[:-2], rest[-2:]

    def row_copy(r, k):
        return pltpu.make_async_copy(ys_ref.at[pl.ds(pos_ref[0, 0, 2 * r + k], 1), :], buf.at[k, pl.ds(r, 1), :], sem)

    def start(g, carry):
        for u in range(ROW_DMA_UNROLL):
            row_copy(g * ROW_DMA_UNROLL + u, 0).start(priority=0)
            row_copy(g * ROW_DMA_UNROLL + u, 1).start(priority=1)
        return carry

    lax.fori_loop(0, tm // ROW_DMA_UNROLL, start, 0)
    for k in range(TOP_K):
        pltpu.make_async_copy(ys_ref.at[pl.ds(0, tm), :], buf.at[k], sem).wait()
    rt = rt_ref[...]
    f = rt[:, 2:3] * buf[0] + rt[:, 3:4] * buf[1]
    _token_store(o_refs, _layer_norm(alpha * x_ref[...] + f, g_ref[...], b_ref[...]), npt)


def _combine(x, pos, route_t, ys, g, b, alpha, tm, split):
    t, d = x.shape
    out_specs, out_shape = _token_out(t, d, tm, split)
    npt = None if split is None else split[0] // tm
    return pl.pallas_call(
        functools.partial(_combine_kernel, tm=tm, alpha=alpha, npt=npt),
        grid=(t // tm,),
        in_specs=[
            pl.BlockSpec((1, 1, 2 * tm), lambda i: (i, 0, 0), memory_space=pltpu.SMEM),
            pl.BlockSpec((tm, d), lambda i: (i, 0)),
            pl.BlockSpec((tm, 8), lambda i: (i, 0)),
            pl.BlockSpec(memory_space=pl.ANY),
            _full_spec((1, d)), _full_spec((1, d)),
        ],
        out_specs=out_specs,
        out_shape=out_shape,
        scratch_shapes=[pltpu.VMEM((2, tm, d), F32), pltpu.SemaphoreType.DMA],
        compiler_params=pltpu.CompilerParams(dimension_semantics=("arbitrary",)),
        name="moe_combine",
    )(pos.reshape(t // tm, 1, 2 * tm), x, route_t, ys, g.reshape(1, d), b.reshape(1, d))


def _ffn_moe(x, w_router, wg, wu, wd, g, b, alpha, tr, tm, split):
    t, d = x.shape
    ne = w_router.shape[1]
    route, counts = _router(x, w_router, tr)
    cnt = counts[:, 0].astype(jnp.int32)
    gsz = ((cnt + tm - 1) // tm) * tm
    ends = jnp.cumsum(gsz)
    offs = ends - gsz
    n_tiles = (TOP_K * t) // tm + ne
    n_rows = n_tiles * tm
    tile_start = jnp.arange(n_tiles, dtype=jnp.int32) * tm
    tile_e = jnp.sum((tile_start[:, None] >= ends[None, :]).astype(jnp.int32), axis=1)
    n_valid = (ends[ne - 1] // tm).astype(jnp.int32).reshape(1)
    last_e = jnp.sum((ends[ne - 1] - 1 >= ends).astype(jnp.int32))
    tile_e = jnp.minimum(tile_e, last_e).astype(jnp.int32)
    i12 = route[0:2].astype(jnp.int32)
    base = sum(jnp.where(i12 == e, offs[e], 0) for e in range(ne))
    pos = (base + route[4:6].astype(jnp.int32)).T.reshape(-1)
    xs = _dispatch(x, pos, n_rows, tr)
    ys = _experts(xs, tile_e, n_valid, wg.astype(BF16), wu.astype(BF16), wd.astype(BF16), tm)
    return _combine(x, pos, route.T, ys, g, b, alpha, tr, split)


def kernel(x_prompt, x_sample, cache_conv, state_gla, w_in, b_in, conv_w, conv_b, conv_ln_g, conv_ln_b, w_conv_out, b_conv_out, w_gate_up, b_gate, gla_norm_g, w_gla_out, w_o, ln1_g, ln1_b, ln2_g, ln2_b, ff_w_gate, ff_w_up, ff_w_down, w_router, moe_w_gate, moe_w_up, moe_w_down):
    p = dict(w_in=w_in, b_in=b_in, conv_w=conv_w, conv_b=conv_b, conv_ln_g=conv_ln_g, conv_ln_b=conv_ln_b,
             w_conv_out=w_conv_out, b_conv_out=b_conv_out, w_gate_up=w_gate_up, b_gate=b_gate,
             gla_norm_g=gla_norm_g, w_gla_out=w_gla_out, w_o=w_o, ln1_g=ln1_g, ln1_b=ln1_b)
    depth = w_in.shape[0]
    bsz, seq, d = x_prompt.shape
    ns, ls, _ = x_sample.shape
    dc = conv_w.shape[-1]
    rank, dk = w_gate_up.shape[1], w_gate_up.shape[2]
    dv = w_gla_out.shape[1]
    dims = dict(d=d, dc=dc, dk=dk, dv=dv, rank=rank, hk=dk // GLA_HEADS, hv=dv // GLA_HEADS,
                alpha=(2.0 * depth) ** 0.25)
    alpha = dims["alpha"]
    tl = min(512, seq)
    tp = bsz * seq
    ts = ns * ls
    tm = min(512, ts)
    assert seq % tl == 0 and tp % ts == 0 and ts % tm == 0 and ts % tl == 0 and ns % SEQ_PER_STEP == 0

    x_p, x_s, s_blk = x_prompt.reshape(tp, d), x_sample.reshape(ts, d), 0
    hist_p, state_p, hist_s, state_s = [], [], [], []
    for l in range(depth):
        wts = _mixer_weights(l, p, dims)
        x1s, hs, ss = _mixer_sample(x_s, s_blk, cache_conv[l], state_gla[l], wts, dims, ns, ls)
        x1, hp, sp = _mixer_prompt(x_p, x1s, wts, dims, tl, bsz, seq)
        hist_p.append(hp), state_p.append(sp), hist_s.append(hs), state_s.append(ss)
        split = (tp, ts) if l == depth - 1 else None
        if l % 2 == 0:
            x2 = _ffn_dense(x1, ff_w_gate[l // 2], ff_w_up[l // 2], ff_w_down[l // 2], ln2_g[l], ln2_b[l],
                            alpha, tm, split)
        else:
            x2 = _ffn_moe(x1, w_router[l // 2], moe_w_gate[l // 2], moe_w_up[l // 2], moe_w_down[l // 2],
                          ln2_g[l], ln2_b[l], alpha, tm, tm, split)
        if split is None:
            x_p, x_s, s_blk = x2[0], x2[0], tp // ts
    y_p, y_s = x2
    return (y_p.reshape(bsz, seq, d), y_s.reshape(ns, ls, d), jnp.stack(hist_p),
            jnp.stack(state_p).astype(state_gla.dtype), jnp.stack(hist_s), jnp.stack(state_s).astype(state_gla.dtype))
```

```python
import functools

import jax
import jax.numpy as jnp
from jax import lax
from jax.experimental import pallas as pl
from jax.experimental.pallas import tpu as pltpu

CHUNK = 64
CONV_W = 31
GLA_HEADS = 4
GATE_TAU = 16.0
LN_EPS = 1e-5
RMS_EPS = 1e-6
TOP_K = 2

HIST_PAD = 32
CONV_ROWS = 32
ROW_DMA_UNROLL = 8
SEQ_PER_STEP = 4
VMEM_LIMIT = 56 * 1024 * 1024

BF16 = jnp.bfloat16
F32 = jnp.float32


def _mm(a, b):
    return jnp.dot(a, b, preferred_element_type=F32)


def _sigmoid(x):
    return 0.5 * jnp.tanh(0.5 * x) + 0.5


def _silu(x):
    return x * _sigmoid(x)


def _log_sigmoid(z):
    return -(jnp.maximum(-z, 0.0) + jnp.log(1.0 + jnp.exp(-jnp.abs(z))))


def _layer_norm(x, g, b):
    mu = jnp.mean(x, axis=-1, keepdims=True)
    xc = x - mu
    var = jnp.mean(xc * xc, axis=-1, keepdims=True)
    return xc * lax.rsqrt(var + LN_EPS) * g + b


def _split3(x):
    hi = x.astype(BF16)
    r1 = x - hi.astype(F32)
    mid = r1.astype(BF16)
    lo = (r1 - mid.astype(F32)).astype(BF16)
    return hi, mid, lo


def _full_spec(shape):
    zeros = (0,) * len(shape)
    return pl.BlockSpec(shape, lambda *_: zeros, pipeline_mode=pl.Buffered(1))


def _project(xb, w, dims, u_store, q_s, k_s, v_s, lf_s):
    dc, dk, dv = dims["dc"], dims["dk"], dims["dv"]
    off_q, off_k, off_v = 2 * dc, 2 * dc + dk, 2 * dc + 2 * dk
    off_g = off_v + dv
    glu = _mm(xb, w["wmain"][:, 0:off_q]) + w["bmain"][:, 0:off_q]
    u_store(glu[:, 0:dc] * _sigmoid(glu[:, dc:off_q]))
    q_s[...] = (_mm(xb, w["wmain"][:, off_q:off_k]) + w["bmain"][:, off_q:off_k]) * (dims["hk"] ** -0.5)
    k_s[...] = _mm(xb, w["wmain"][:, off_k:off_v]) + w["bmain"][:, off_k:off_v]
    v_s[...] = _mm(xb, w["wmain"][:, off_v:off_g]) + w["bmain"][:, off_v:off_g]
    lr = _mm(xb, w["wlr"][...]) + w["blr"][...]
    z = _mm(lr.astype(BF16), w["wgu"][...]) + w["bgu"][...]
    lf_s[...] = _log_sigmoid(z) * (1.0 / GATE_TAU)


def _gla_chunk(q_s, k_s, v_s, lf_s, o_s, rows, c, get_state, set_state, dims):
    hk, hv, dk = dims["hk"], dims["hv"], dims["dk"]
    lf = lf_s[rows, :]
    ri = lax.broadcasted_iota(jnp.int32, (c, c), 0)
    ci = lax.broadcasted_iota(jnp.int32, (c, c), 1)
    causal = ri >= ci
    tri = causal.astype(BF16)
    hi, mid, lo = _split3(lf)
    b = _mm(tri, hi) + _mm(tri, mid) + _mm(tri, lo)
    b_last = b[c - 1:c, :]
    q_t = q_s[rows, :] * jnp.exp(b)
    kk = k_s[rows, :]
    k_t = kk * jnp.exp(-b)
    k_e = kk * jnp.exp(b_last - b)
    d_t = jnp.transpose(jnp.broadcast_to(jnp.exp(b_last), (128, dk)))
    vv = v_s[rows, :]
    for h in range(GLA_HEADS):
        ks = slice(h * hk, (h + 1) * hk)
        vs = slice(h * hv, (h + 1) * hv)
        qh = q_t[:, ks].astype(BF16)
        kh = k_t[:, ks].astype(BF16)
        keh = k_e[:, ks].astype(BF16)
        vh = vv[:, vs].astype(BF16)
        att = lax.dot_general(qh, kh, (((1,), (1,)), ((), ())), preferred_element_type=F32)
        att = jnp.where(causal, att, 0.0)
        s_old = get_state(h)
        o_s[rows, vs] = _mm(qh, s_old.astype(BF16)) + _mm(att.astype(BF16), vh)
        upd = lax.dot_general(keh, vh, (((0,), (0,)), ((), ())), preferred_element_type=F32)
        dcol = d_t[ks, :]
        decay = jnp.concatenate([dcol] * (hv // 128), axis=1)
        set_state(h, decay * s_old + upd)


def _conv_block(win, w8_ref, n):
    wn, ch = win.shape
    base = HIST_PAD - (CONV_W - 1)
    acc = None
    for b in range(8):
        wb = win if b == 0 else pltpu.roll(win, wn - b, axis=0)
        for a in range((base + CONV_W + 7) // 8):
            j = 8 * a + b - base
            if 0 <= j < CONV_W:
                term = w8_ref[8 * j:8 * j + 8, :][None] * wb[8 * a:8 * a + n, :].reshape(n // 8, 8, ch)
                acc = term if acc is None else acc + term
    return acc.reshape(n, ch)


def _tail(x, xb, c, w, dims, o_s):
    d, dc, dk, dv, hv = dims["d"], dims["dc"], dims["dk"], dims["dv"], dims["hv"]
    off_g = 2 * dc + 2 * dk + dv
    off_lr = off_g + dv
    c = _silu(_layer_norm(c + w["convb"][...], w["clng"][...], w["clnb"][...]))
    y_a = _mm(c.astype(BF16), w["wco"][...]) + w["bco"][...]
    g_out = _mm(xb, w["wmain"][:, off_g:off_lr]) + w["bmain"][:, off_g:off_lr]
    heads = []
    for h in range(GLA_HEADS):
        oh = o_s[:, h * hv:(h + 1) * hv]
        ms = jnp.mean(oh * oh, axis=-1, keepdims=True)
        heads.append(oh * lax.rsqrt(ms + RMS_EPS) * w["gng"][...])
    o = jnp.concatenate(heads, axis=1) * _silu(g_out)
    y_b = _mm(o.astype(BF16), w["wgo"][...])
    gates = _sigmoid(_mm(xb, w["wgates"][...]) + w["bgates"][...])
    merged = gates[:, 0:d] * y_a + gates[:, d:2 * d] * y_b
    m = _mm(merged.astype(BF16), w["wo"][...])
    return _layer_norm(dims["alpha"] * x + m, w["ln1g"][...], w["ln1b"][...])


_W_NAMES = ("wmain", "bmain", "wlr", "blr", "wgates", "bgates", "convw", "convb", "clng", "clnb",
            "wco", "bco", "wgu", "bgu", "gng", "wgo", "wo", "ln1g", "ln1b")


def _mixer_prompt_kernel(*refs, dims, tl, nt, n_prompt):
    nw = len(_W_NAMES)
    x_ref, tail_ref = refs[0:2]
    w = dict(zip(_W_NAMES, refs[2:2 + nw]))
    x1_ref, hist_ref, state_ref = refs[2 + nw:5 + nw]
    scratch = refs[5 + nw:]
    i = pl.program_id(0)

    @pl.when(i < n_prompt)
    def _():
        _mixer_prompt_tile(i % nt, x_ref, w, x1_ref, hist_ref, state_ref, scratch, dims, tl, nt)

    @pl.when(i >= n_prompt)
    def _():
        x1_ref[...] = tail_ref[...]


def _mixer_prompt_tile(j, x_ref, w, x1_ref, hist_ref, state_ref, scratch, dims, tl, nt):
    ubuf, qt_s, kt_s, ke_s, vb_s, g_s, gt_s, o_s, c_s, s_s, dt_s = scratch
    d, dc, dk, dv, hk, hv = dims["d"], dims["dc"], dims["dk"], dims["dv"], dims["hk"], dims["hv"]
    off_q, off_k, off_v = 2 * dc, 2 * dc + dk, 2 * dc + 2 * dk
    off_g = off_v + dv
    off_lr = off_g + dv
    nch = tl // CHUNK
    rep = 128 // nch
    wm, bm = w["wmain"], w["bmain"]

    @pl.when(j == 0)
    def _():
        ubuf[0:HIST_PAD, :] = jnp.zeros((HIST_PAD, dc), F32)
        s_s[...] = jnp.zeros(s_s.shape, F32)

    x = x_ref[...]
    xb = x.astype(BF16)

    glu = _mm(xb, wm[:, 0:off_q]) + bm[:, 0:off_q]
    ubuf[HIST_PAD:HIST_PAD + tl, :] = glu[:, 0:dc] * _sigmoid(glu[:, dc:off_q])
    lr = _mm(xb, w["wlr"][...]) + w["blr"][...]
    z = _mm(lr.astype(BF16), w["wgu"][...]) + w["bgu"][...]
    lf = _log_sigmoid(z) * (1.0 / GATE_TAU)
    ri = lax.broadcasted_iota(jnp.int32, (CHUNK, CHUNK), 0)
    ci = lax.broadcasted_iota(jnp.int32, (CHUNK, CHUNK), 1)
    causal = ri >= ci
    tri = causal.astype(BF16)
    parts = _split3(lf)
    b = jnp.concatenate(
        [sum(_mm(tri, p[c * CHUNK:(c + 1) * CHUNK, :]) for p in parts) for c in range(nch)], axis=0)
    b3 = b.reshape(nch, CHUNK, dk)
    b_last = b3[:, CHUNK - 1:CHUNK, :]
    q = (_mm(xb, wm[:, off_q:off_k]) + bm[:, off_q:off_k]) * (hk ** -0.5)
    qt_s[...] = (q * jnp.exp(b)).astype(BF16)
    kk = _mm(xb, wm[:, off_k:off_v]) + bm[:, off_k:off_v]
    kt_s[...] = (kk * jnp.exp(-b)).astype(BF16)
    ke_s[...] = (kk.reshape(nch, CHUNK, dk) * jnp.exp(b_last - b3)).reshape(tl, dk).astype(BF16)
    vb_s[...] = (_mm(xb, wm[:, off_v:off_g]) + bm[:, off_v:off_g]).astype(BF16)
    d_rows = jnp.broadcast_to(jnp.exp(b_last), (nch, rep, dk)).reshape(nch * rep, dk)
    if nch * rep < 128:
        d_rows = jnp.concatenate([d_rows, jnp.zeros((128 - nch * rep, dk), F32)], axis=0)
    dt_s[...] = jnp.transpose(d_rows)
    g_s[...] = _silu(_mm(xb, wm[:, off_g:off_lr]) + bm[:, off_g:off_lr])

    states = [s_s[h] for h in range(GLA_HEADS)]
    conv_per = tl // CONV_ROWS // nch
    gw = 2 * d // nch
    for c in range(nch):
        rows = slice(c * CHUNK, (c + 1) * CHUNK)
        for h in range(GLA_HEADS):
            ks = slice(h * hk, (h + 1) * hk)
            vs = slice(h * hv, (h + 1) * hv)
            qh, kh, keh, vh = qt_s[rows, ks], kt_s[rows, ks], ke_s[rows, ks], vb_s[rows, vs]
            att = lax.dot_general(qh, kh, (((1,), (1,)), ((), ())), preferred_element_type=F32)
            att = jnp.where(causal, att, 0.0)
            s_old = states[h]
            o_s[rows, vs] = _mm(qh, s_old.astype(BF16)) + _mm(att.astype(BF16), vh)
            upd = lax.dot_general(keh, vh, (((0,), (0,)), ((), ())), preferred_element_type=F32)
            decay = jnp.broadcast_to(dt_s[ks, c * rep:c * rep + 1], (hk, hv))
            states[h] = decay * s_old + upd
        for bi in range(c * conv_per, (c + 1) * conv_per):
            r0 = bi * CONV_ROWS
            c_s[r0:r0 + CONV_ROWS, :] = _conv_block(
                ubuf[r0:r0 + CONV_ROWS + HIST_PAD, :], w["convw"], CONV_ROWS)
        cols = slice(c * gw, (c + 1) * gw)
        gt_s[:, cols] = _sigmoid(_mm(xb, w["wgates"][:, cols]) + w["bgates"][:, cols])
    for h in range(GLA_HEADS):
        s_s[h] = states[h]

    cv = _silu(_layer_norm(c_s[...] + w["convb"][...], w["clng"][...], w["clnb"][...]))
    y_a = _mm(cv.astype(BF16), w["wco"][...]) + w["bco"][...]
    heads = []
    for h in range(GLA_HEADS):
        oh = o_s[:, h * hv:(h + 1) * hv]
        ms = jnp.mean(oh * oh, axis=-1, keepdims=True)
        heads.append(oh * lax.rsqrt(ms + RMS_EPS) * w["gng"][...])
    o = jnp.concatenate(heads, axis=1) * g_s[...]
    y_b = _mm(o.astype(BF16), w["wgo"][...])
    merged = gt_s[:, 0:d] * y_a + gt_s[:, d:2 * d] * y_b
    m = _mm(merged.astype(BF16), w["wo"][...])
    x1_ref[...] = _layer_norm(dims["alpha"] * x + m, w["ln1g"][...], w["ln1b"][...])

    @pl.when(j == nt - 1)
    def _():
        hist_ref[0] = ubuf[tl + HIST_PAD - (CONV_W - 1):tl + HIST_PAD, :]
        state_ref[0] = s_s[...]

    ubuf[0:HIST_PAD, :] = ubuf[tl:tl + HIST_PAD, :]


def _mixer_sample_kernel(*refs, dims, ns, ls):
    nw = len(_W_NAMES)
    x_ref, hist_in_ref, state_in_ref = refs[0:3]
    w = dict(zip(_W_NAMES, refs[3:3 + nw]))
    x1_ref, hist_ref, state_ref = refs[3 + nw:6 + nw]
    ubuf, q_s, k_s, v_s, lf_s, o_s, c_s = refs[6 + nw:]
    i = pl.program_id(0)
    dc = dims["dc"]
    hl = CONV_W - 1

    @pl.when(i == 0)
    def _():
        xb = x_ref[...].astype(BF16)
        ubuf[:, 0:8, :] = jnp.zeros((ns, 8, dc), F32)
        ubuf[:, HIST_PAD - hl:HIST_PAD, :] = hist_in_ref[0]

        def u_store(u):
            ubuf[:, HIST_PAD:HIST_PAD + ls, :] = u.reshape(ns, ls, dc)

        _project(xb, w, dims, u_store, q_s, k_s, v_s, lf_s)

    for s in range(SEQ_PER_STEP):
        seq = i * SEQ_PER_STEP + s
        rows = pl.ds(pl.multiple_of(seq * ls, ls), ls)

        def set_state(h, val, s=s):
            state_ref[s, h] = val

        _gla_chunk(q_s, k_s, v_s, lf_s, o_s, rows, ls, lambda h, s=s: state_in_ref[0, s, h], set_state, dims)
        win = ubuf[seq]
        c_s[rows, :] = _conv_block(win, w["convw"], ls)
        hist_ref[seq] = win[HIST_PAD + ls - hl:HIST_PAD + ls, :]

    @pl.when(i == ns // SEQ_PER_STEP - 1)
    def _():
        x = x_ref[...]
        x1_ref[...] = _tail(x, x.astype(BF16), c_s[...], w, dims, o_s)


def _mixer_weights(l, p, dims):
    dc, dk, dv, rank = dims["dc"], dims["dk"], dims["dv"], dims["rank"]
    off_lr = 2 * dc + 2 * dk + 2 * dv
    off_gates = off_lr + rank
    w_in, b_in = p["w_in"][l], p["b_in"][l]
    row = lambda v: v.reshape(1, -1).astype(F32)
    return (
        w_in[:, :off_lr].astype(BF16), row(b_in[:off_lr]),
        w_in[:, off_lr:off_gates].astype(BF16), row(b_in[off_lr:off_gates]),
        w_in[:, off_gates:].astype(BF16), row(b_in[off_gates:]),
        jnp.repeat(p["conv_w"][l].astype(F32), 8, axis=0),
        row(p["conv_b"][l]), row(p["conv_ln_g"][l]), row(p["conv_ln_b"][l]),
        p["w_conv_out"][l].astype(BF16), row(p["b_conv_out"][l]),
        p["w_gate_up"][l].astype(BF16), row(p["b_gate"][l]),
        row(p["gla_norm_g"][l]),
        p["w_gla_out"][l].astype(BF16), p["w_o"][l].astype(BF16),
        row(p["ln1_g"][l]), row(p["ln1_b"][l]),
    )


def _mixer_prompt(x, x1_tail, wts, dims, tl, bsz, seq):
    d = x.shape[1]
    dc, dk, dv, hk, hv = dims["dc"], dims["dk"], dims["dv"], dims["hk"], dims["hv"]
    hl = CONV_W - 1
    nt = seq // tl
    n_prompt = bsz * nt
    n_tail = x1_tail.shape[0] // tl
    kern = functools.partial(_mixer_prompt_kernel, dims=dims, tl=tl, nt=nt, n_prompt=n_prompt)
    seq_of = lambda i: jnp.minimum(i // nt, bsz - 1)
    return pl.pallas_call(
        kern,
        grid=(n_prompt + n_tail,),
        in_specs=[pl.BlockSpec((tl, d), lambda i: (jnp.minimum(i, n_prompt - 1), 0)),
                  pl.BlockSpec((tl, d), lambda i: (jnp.maximum(i - n_prompt, 0), 0))]
        + [_full_spec(a.shape) for a in wts],
        out_specs=[
            pl.BlockSpec((tl, d), lambda i: (i, 0)),
            pl.BlockSpec((1, hl, dc), lambda i: (seq_of(i), 0, 0)),
            pl.BlockSpec((1, GLA_HEADS, hk, hv), lambda i: (seq_of(i), 0, 0, 0)),
        ],
        out_shape=[
            jax.ShapeDtypeStruct(((n_prompt + n_tail) * tl, d), F32),
            jax.ShapeDtypeStruct((bsz, hl, dc), F32),
            jax.ShapeDtypeStruct((bsz, GLA_HEADS, hk, hv), F32),
        ],
        scratch_shapes=[
            pltpu.VMEM((HIST_PAD + tl, dc), F32),
            pltpu.VMEM((tl, dk), BF16), pltpu.VMEM((tl, dk), BF16), pltpu.VMEM((tl, dk), BF16),
            pltpu.VMEM((tl, dv), BF16),
            pltpu.VMEM((tl, dv), F32), pltpu.VMEM((tl, 2 * d), F32),
            pltpu.VMEM((tl, dv), F32), pltpu.VMEM((tl, dc), F32),
            pltpu.VMEM((GLA_HEADS, hk, hv), F32), pltpu.VMEM((dk, 128), F32),
        ],
        compiler_params=pltpu.CompilerParams(
            dimension_semantics=("arbitrary",), vmem_limit_bytes=VMEM_LIMIT),
        name="mixer_prompt",
    )(x, x1_tail, *wts)


def _mixer_sample(x, in_blk, hist, state, layer, wts, dims, ns, ls):
    d = x.shape[1]
    dc, dk, dv, hk, hv = dims["dc"], dims["dk"], dims["dv"], dims["hk"], dims["hv"]
    hl = CONV_W - 1
    t = ns * ls
    kern = functools.partial(_mixer_sample_kernel, dims=dims, ns=ns, ls=ls)
    x1, hist_o, state_o = pl.pallas_call(
        kern,
        grid=(ns // SEQ_PER_STEP,),
        in_specs=[
            pl.BlockSpec((t, d), lambda i: (in_blk, 0)),
            pl.BlockSpec((1, ns, hl, dc), lambda i: (layer, 0, 0, 0)),
            pl.BlockSpec((1, SEQ_PER_STEP, GLA_HEADS, hk, hv), lambda i: (layer, i, 0, 0, 0)),
        ] + [_full_spec(a.shape) for a in wts],
        out_specs=[
            pl.BlockSpec((t, d), lambda i: (0, 0)),
            pl.BlockSpec((ns, hl, dc), lambda i: (0, 0, 0)),
            pl.BlockSpec((SEQ_PER_STEP, GLA_HEADS, hk, hv), lambda i: (i, 0, 0, 0)),
        ],
        out_shape=[
            jax.ShapeDtypeStruct((t, d), F32),
            jax.ShapeDtypeStruct((ns, hl, dc), F32),
            jax.ShapeDtypeStruct((ns, GLA_HEADS, hk, hv), F32),
        ],
        scratch_shapes=[
            pltpu.VMEM((ns, HIST_PAD + ls, dc), F32),
            pltpu.VMEM((t, dk), F32), pltpu.VMEM((t, dk), F32), pltpu.VMEM((t, dv), F32),
            pltpu.VMEM((t, dk), F32), pltpu.VMEM((t, dv), F32), pltpu.VMEM((t, dc), F32),
        ],
        compiler_params=pltpu.CompilerParams(
            dimension_semantics=("arbitrary",), vmem_limit_bytes=VMEM_LIMIT),
        name="mixer_sample",
    )(x, hist, state, *wts)
    return x1, hist_o, state_o


def _token_out(t, d, tm, split):
    if split is None:
        return [pl.BlockSpec((tm, d), lambda i: (i, 0))], [jax.ShapeDtypeStruct((t, d), F32)]
    tp, ts = split
    npt = tp // tm
    specs = [pl.BlockSpec((tm, d), lambda i: (jnp.minimum(i, npt - 1), 0)),
             pl.BlockSpec((tm, d), lambda i: (jnp.maximum(i - npt, 0), 0))]
    return specs, [jax.ShapeDtypeStruct((tp, d), F32), jax.ShapeDtypeStruct((ts, d), F32)]


def _token_store(o_refs, val, npt):
    if len(o_refs) == 1:
        o_refs[0][...] = val
        return
    i = pl.program_id(0)

    @pl.when(i < npt)
    def _():
        o_refs[0][...] = val

    @pl.when(i >= npt)
    def _():
        o_refs[1][...] = val


def _ffn_dense_kernel(x_ref, wg_ref, wu_ref, wd_ref, g_ref, b_ref, *o_refs, alpha, npt):
    x = x_ref[...]
    xb = x.astype(BF16)
    h = _silu(_mm(xb, wg_ref[...])) * _mm(xb, wu_ref[...])
    f = _mm(h.astype(BF16), wd_ref[...])
    _token_store(o_refs, _layer_norm(alpha * x + f, g_ref[...], b_ref[...]), npt)


def _ffn_dense(x, wg, wu, wd, g, b, alpha, tm, split):
    t, d = x.shape
    ops = (wg.astype(BF16), wu.astype(BF16), wd.astype(BF16), g.reshape(1, d), b.reshape(1, d))
    out_specs, out_shape = _token_out(t, d, tm, split)
    npt = None if split is None else split[0] // tm
    return pl.pallas_call(
        functools.partial(_ffn_dense_kernel, alpha=alpha, npt=npt),
        grid=(t // tm,),
        in_specs=[pl.BlockSpec((tm, d), lambda i: (i, 0))] + [_full_spec(a.shape) for a in ops],
        out_specs=out_specs,
        out_shape=out_shape,
        compiler_params=pltpu.CompilerParams(
            dimension_semantics=("arbitrary",), vmem_limit_bytes=VMEM_LIMIT),
        name="ffn_dense",
    )(x, *ops)


def _cast_plan(arrays, n_steps):
    n_blocks = max(n for n in range(1, n_steps + 1)
                   if all(a.shape[0] % n == 0 and (a.shape[0] // n) % 16 == 0 for a in arrays))
    spec = lambda a: pl.BlockSpec((a.shape[0] // n_blocks, a.shape[1]), lambda i: (jnp.minimum(i, n_blocks - 1), 0))
    return ([spec(a) for a in arrays], [spec(a) for a in arrays],
            [jax.ShapeDtypeStruct(a.shape, BF16) for a in arrays])


def _cast_blocks(src_refs, dst_refs):
    for src, dst in zip(src_refs, dst_refs):
        dst[...] = src[...].astype(BF16)


def _router_kernel(x_ref, wr_ref, *rest, ne, tr, n_cast):
    cast_in, (route_ref, cnt_ref), cast_out = rest[:n_cast], rest[n_cast:n_cast + 2], rest[n_cast + 2:-1]
    carry = rest[-1]
    i = pl.program_id(0)
    _cast_blocks(cast_in, cast_out)

    @pl.when(i == 0)
    def _():
        carry[...] = jnp.zeros(carry.shape, F32)

    xh = x_ref[...]
    x_hi = xh.astype(BF16)
    x_lo = (xh - x_hi.astype(F32)).astype(BF16)
    wr = wr_ref[...]
    w_hi = wr.astype(BF16)
    w_lo = (wr - w_hi.astype(F32)).astype(BF16)
    nt = (((1,), (1,)), ((), ()))
    logits = (lax.dot_general(w_hi, x_hi, nt, preferred_element_type=F32)
              + lax.dot_general(w_hi, x_lo, nt, preferred_element_type=F32)
              + lax.dot_general(w_lo, x_hi, nt, preferred_element_type=F32))
    mx = jnp.max(logits, axis=0, keepdims=True)
    ex = jnp.exp(logits - mx)
    probs = ex / jnp.sum(ex, axis=0, keepdims=True)
    eid = lax.broadcasted_iota(jnp.int32, (ne, tr), 0)
    p1 = jnp.max(probs, axis=0, keepdims=True)
    i1 = jnp.min(jnp.where(probs == p1, eid, ne), axis=0, keepdims=True)
    rest = jnp.where(eid == i1, -1.0, probs)
    p2 = jnp.max(rest, axis=0, keepdims=True)
    i2 = jnp.min(jnp.where(rest == p2, eid, ne), axis=0, keepdims=True)
    den = p1 + p2
    oh1 = (eid == i1).astype(F32)
    oh2 = (eid == i2).astype(F32)
    oh = oh1 + oh2
    ri = lax.broadcasted_iota(jnp.int32, (tr, tr), 0)
    ci = lax.broadcasted_iota(jnp.int32, (tr, tr), 1)
    upper = (ri <= ci).astype(BF16)
    incl = _mm(oh.astype(BF16), upper)
    before = carry[:, 0:1] + incl - oh
    r1 = jnp.sum(oh1 * before, axis=0, keepdims=True)
    r2 = jnp.sum(oh2 * before, axis=0, keepdims=True)
    zero = jnp.zeros((1, tr), F32)
    route_ref[...] = jnp.concatenate(
        [i1.astype(F32), i2.astype(F32), p1 / den, p2 / den, r1, r2, zero, zero], axis=0)
    total = carry[:, 0:1] + incl[:, tr - 1:tr]
    carry[...] = jnp.broadcast_to(total, carry.shape)
    cnt_ref[...] = jnp.broadcast_to(total, cnt_ref.shape)


def _router(x, w_router, tr, to_cast):
    t, d = x.shape
    ne = w_router.shape[1]
    c_in, c_out, c_shapes = _cast_plan(to_cast, t // tr)
    res = pl.pallas_call(
        functools.partial(_router_kernel, ne=ne, tr=tr, n_cast=len(to_cast)),
        grid=(t // tr,),
        in_specs=[pl.BlockSpec((tr, d), lambda i: (i, 0)), _full_spec((ne, d))] + c_in,
        out_specs=[pl.BlockSpec((8, tr), lambda i: (0, i)), pl.BlockSpec((ne, 128), lambda i: (0, 0))] + c_out,
        out_shape=[jax.ShapeDtypeStruct((8, t), F32), jax.ShapeDtypeStruct((ne, 128), F32)] + c_shapes,
        scratch_shapes=[pltpu.VMEM((ne, 128), F32)],
        compiler_params=pltpu.CompilerParams(dimension_semantics=("arbitrary",), vmem_limit_bytes=VMEM_LIMIT),
        name="moe_router",
    )(x, w_router.T.astype(F32), *to_cast)
    return res[0], res[1], res[2:]


def _dispatch_kernel(pos_ref, grp_ref, x_ref, *rest, tm, ne, n_tiles, n_cast):
    cast_in, xs_ref, cast_out, sem = rest[:n_cast], rest[n_cast], rest[n_cast + 1:-1], rest[-1]
    i = pl.program_id(0)

    def row_copy(r, k):
        return pltpu.make_async_copy(x_ref.at[pl.ds(r, 1), :], xs_ref.at[pl.ds(pos_ref[0, 0, 2 * r + k], 1), :], sem)

    def start(g, carry):
        for u in range(ROW_DMA_UNROLL):
            row_copy(g * ROW_DMA_UNROLL + u, 0).start()
            row_copy(g * ROW_DMA_UNROLL + u, 1).start()
        return carry

    lax.fori_loop(0, tm // ROW_DMA_UNROLL, start, 0)
    _cast_blocks(cast_in, cast_out)
    for _ in range(TOP_K):
        pltpu.make_async_copy(x_ref, xs_ref.at[pl.ds(0, tm), :], sem).wait()

    @pl.when(i == pl.num_programs(0) - 1)
    def _():
        for e in range(ne):
            lo, hi = grp_ref[e] + grp_ref[ne + e], grp_ref[e] + grp_ref[2 * ne + e]

            def pad_copy(r):
                return pltpu.make_async_copy(x_ref.at[pl.ds(0, 1), :], xs_ref.at[pl.ds(r, 1), :], sem)

            lax.fori_loop(lo, hi, lambda r, c: (pad_copy(r).start(), c)[1], 0)
            lax.fori_loop(lo, hi, lambda r, c: (pad_copy(r).wait(), c)[1], 0)

        def tile_copy(j):
            return pltpu.make_async_copy(x_ref, xs_ref.at[pl.ds(j * tm, tm), :], sem)

        lax.fori_loop(grp_ref[3 * ne], n_tiles, lambda j, c: (tile_copy(j).start(), c)[1], 0)
        lax.fori_loop(grp_ref[3 * ne], n_tiles, lambda j, c: (tile_copy(j).wait(), c)[1], 0)


def _dispatch(x, pos, grp, n_tiles, tm, to_cast):
    t, d = x.shape
    ne = (grp.shape[0] - 1) // 3
    c_in, c_out, c_shapes = _cast_plan(to_cast, t // tm)
    res = pl.pallas_call(
        functools.partial(_dispatch_kernel, tm=tm, ne=ne, n_tiles=n_tiles, n_cast=len(to_cast)),
        grid=(t // tm,),
        in_specs=[
            pl.BlockSpec((1, 1, 2 * tm), lambda i: (i, 0, 0), memory_space=pltpu.SMEM),
            pl.BlockSpec(memory_space=pltpu.SMEM),
            pl.BlockSpec((tm, d), lambda i: (i, 0)),
        ] + c_in,
        out_specs=[pl.BlockSpec(memory_space=pl.ANY)] + c_out,
        out_shape=[jax.ShapeDtypeStruct((n_tiles * tm, d), F32)] + c_shapes,
        scratch_shapes=[pltpu.SemaphoreType.DMA],
        compiler_params=pltpu.CompilerParams(
            dimension_semantics=("arbitrary",), has_side_effects=True, vmem_limit_bytes=VMEM_LIMIT),
        name="moe_dispatch",
    )(pos.reshape(t // tm, 1, 2 * tm), grp, x, *to_cast)
    return res[0], res[1:]


def _experts_kernel(te_ref, nv_ref, xs_ref, wg_ref, wu_ref, wd_ref, ys_ref):
    i = pl.program_id(0)

    @pl.when(i < nv_ref[0])
    def _():
        xb = xs_ref[...].astype(BF16)
        h = _silu(_mm(xb, wg_ref[0])) * _mm(xb, wu_ref[0])
        ys_ref[...] = _mm(h.astype(BF16), wd_ref[0])

    @pl.when(i >= nv_ref[0])
    def _():
        ys_ref[...] = jnp.zeros(ys_ref.shape, F32)


def _experts(xs, tile_expert, n_valid, wg, wu, wd, tm):
    n_rows, d = xs.shape
    ne, _, ff = wg.shape
    grid_spec = pltpu.PrefetchScalarGridSpec(
        num_scalar_prefetch=2,
        grid=(n_rows // tm,),
        in_specs=[
            pl.BlockSpec((tm, d), lambda i, te, nv: (i, 0)),
            pl.BlockSpec((1, d, ff), lambda i, te, nv: (te[i], 0, 0)),
            pl.BlockSpec((1, d, ff), lambda i, te, nv: (te[i], 0, 0)),
            pl.BlockSpec((1, ff, d), lambda i, te, nv: (te[i], 0, 0)),
        ],
        out_specs=pl.BlockSpec((tm, d), lambda i, te, nv: (i, 0)),
    )
    return pl.pallas_call(
        _experts_kernel,
        grid_spec=grid_spec,
        out_shape=jax.ShapeDtypeStruct((n_rows, d), F32),
        compiler_params=pltpu.CompilerParams(
            dimension_semantics=("arbitrary",), vmem_limit_bytes=VMEM_LIMIT),
        name="moe_experts",
    )(tile_expert, n_valid, xs, wg, wu, wd)


def _combine_kernel(pos_ref, x_ref, rt_ref, ys_ref, g_ref, b_ref, *rest, tm, alpha, npt):
    o_refs, (buf, sem) = rest[:-2], rest[-2:]

    def row_copy(r, k):
        return pltpu.make_async_copy(ys_ref.at[pl.ds(pos_ref[0, 0, 2 * r + k], 1), :], buf.at[k, pl.ds(r, 1), :], sem)

    def start(g, carry):
        for u in range(ROW_DMA_UNROLL):
            row_copy(g * ROW_DMA_UNROLL + u, 0).start(priority=0)
            row_copy(g * ROW_DMA_UNROLL + u, 1).start(priority=1)
        return carry

    lax.fori_loop(0, tm // ROW_DMA_UNROLL, start, 0)
    for k in range(TOP_K):
        pltpu.make_async_copy(ys_ref.at[pl.ds(0, tm), :], buf.at[k], sem).wait()
    rt = rt_ref[...]
    f = rt[:, 2:3] * buf[0] + rt[:, 3:4] * buf[1]
    _token_store(o_refs, _layer_norm(alpha * x_ref[...] + f, g_ref[...], b_ref[...]), npt)


def _combine(x, pos, route_t, ys, g, b, alpha, tm, split):
    t, d = x.shape
    out_specs, out_shape = _token_out(t, d, tm, split)
    npt = None if split is None else split[0] // tm
    return pl.pallas_call(
        functools.partial(_combine_kernel, tm=tm, alpha=alpha, npt=npt),
        grid=(t // tm,),
        in_specs=[
            pl.BlockSpec((1, 1, 2 * tm), lambda i: (i, 0, 0), memory_space=pltpu.SMEM),
            pl.BlockSpec((tm, d), lambda i: (i, 0)),
            pl.BlockSpec((tm, 8), lambda i: (i, 0)),
            pl.BlockSpec(memory_space=pl.ANY),
            _full_spec((1, d)), _full_spec((1, d)),
        ],
        out_specs=out_specs,
        out_shape=out_shape,
        scratch_shapes=[pltpu.VMEM((2, tm, d), F32), pltpu.SemaphoreType.DMA],
        compiler_params=pltpu.CompilerParams(dimension_semantics=("arbitrary",)),
        name="moe_combine",
    )(pos.reshape(t // tm, 1, 2 * tm), x, route_t, ys, g.reshape(1, d), b.reshape(1, d))


def _ffn_moe(x, w_router, wg, wu, wd, g, b, alpha, tr, tm, split):
    t, d = x.shape
    ne = w_router.shape[1]
    ff = wg.shape[2]
    assert tr == tm
    route, counts, (wd_b,) = _router(x, w_router, tr, [wd.reshape(ne * ff, d)])
    cnt = counts[:, 0].astype(jnp.int32)
    gsz = ((cnt + tm - 1) // tm) * tm
    ends = jnp.cumsum(gsz)
    offs = ends - gsz
    n_tiles = (TOP_K * t) // tm + ne
    tile_start = jnp.arange(n_tiles, dtype=jnp.int32) * tm
    tile_e = jnp.sum((tile_start[:, None] >= ends[None, :]).astype(jnp.int32), axis=1)
    n_valid = (ends[ne - 1] // tm).astype(jnp.int32).reshape(1)
    last_e = jnp.sum((ends[ne - 1] - 1 >= ends).astype(jnp.int32))
    tile_e = jnp.minimum(tile_e, last_e).astype(jnp.int32)
    i12 = route[0:2].astype(jnp.int32)
    base = sum(jnp.where(i12 == e, offs[e], 0) for e in range(ne))
    pos = (base + route[4:6].astype(jnp.int32)).T.reshape(-1)
    grp = jnp.concatenate([offs, cnt, gsz, n_valid]).astype(jnp.int32)
    xs, (wg_b, wu_b) = _dispatch(x, pos, grp, n_tiles, tr, [wg.reshape(ne * d, ff), wu.reshape(ne * d, ff)])
    ys = _experts(xs, tile_e, n_valid, wg_b.reshape(ne, d, ff), wu_b.reshape(ne, d, ff),
                  wd_b.reshape(ne, ff, d), tm)
    return _combine(x, pos, route.T, ys, g, b, alpha, tr, split)


def kernel(x_prompt, x_sample, cache_conv, state_gla, w_in, b_in, conv_w, conv_b, conv_ln_g, conv_ln_b, w_conv_out, b_conv_out, w_gate_up, b_gate, gla_norm_g, w_gla_out, w_o, ln1_g, ln1_b, ln2_g, ln2_b, ff_w_gate, ff_w_up, ff_w_down, w_router, moe_w_gate, moe_w_up, moe_w_down):
    p = dict(w_in=w_in, b_in=b_in, conv_w=conv_w, conv_b=conv_b, conv_ln_g=conv_ln_g, conv_ln_b=conv_ln_b,
             w_conv_out=w_conv_out, b_conv_out=b_conv_out, w_gate_up=w_gate_up, b_gate=b_gate,
             gla_norm_g=gla_norm_g, w_gla_out=w_gla_out, w_o=w_o, ln1_g=ln1_g, ln1_b=ln1_b)
    depth = w_in.shape[0]
    bsz, seq, d = x_prompt.shape
    ns, ls, _ = x_sample.shape
    dc = conv_w.shape[-1]
    rank, dk = w_gate_up.shape[1], w_gate_up.shape[2]
    dv = w_gla_out.shape[1]
    dims = dict(d=d, dc=dc, dk=dk, dv=dv, rank=rank, hk=dk // GLA_HEADS, hv=dv // GLA_HEADS,
                alpha=(2.0 * depth) ** 0.25)
    alpha = dims["alpha"]
    tl = min(512, seq)
    tp = bsz * seq
    ts = ns * ls
    tm = min(512, ts)
    assert seq % tl == 0 and tp % ts == 0 and ts % tm == 0 and ts % tl == 0 and ns % SEQ_PER_STEP == 0

    x_p, x_s, s_blk = x_prompt.reshape(tp, d), x_sample.reshape(ts, d), 0
    hist_p, state_p, hist_s, state_s = [], [], [], []
    for l in range(depth):
        wts = _mixer_weights(l, p, dims)
        x1s, hs, ss = _mixer_sample(x_s, s_blk, cache_conv, state_gla, l, wts, dims, ns, ls)
        x1, hp, sp = _mixer_prompt(x_p, x1s, wts, dims, tl, bsz, seq)
        hist_p.append(hp), state_p.append(sp), hist_s.append(hs), state_s.append(ss)
        split = (tp, ts) if l == depth - 1 else None
        if l % 2 == 0:
            x2 = _ffn_dense(x1, ff_w_gate[l // 2], ff_w_up[l // 2], ff_w_down[l // 2], ln2_g[l], ln2_b[l],
                            alpha, tm, split)
        else:
            x2 = _ffn_moe(x1, w_router[l // 2], moe_w_gate[l // 2], moe_w_up[l // 2], moe_w_down[l // 2],
                          ln2_g[l], ln2_b[l], alpha, tm, tm, split)
        if split is None:
            x_p, x_s, s_blk = x2[0], x2[0], tp // ts
    y_p, y_s = x2
    return (y_p.reshape(bsz, seq, d), y_s.reshape(ns, ls, d), jnp.stack(hist_p),
            jnp.stack(state_p).astype(state_gla.dtype), jnp.stack(hist_s), jnp.stack(state_s).astype(state_gla.dtype))
```

```python
import functools

import jax
import jax.numpy as jnp
from jax import lax
from jax.experimental import pallas as pl
from jax.experimental.pallas import tpu as pltpu

CHUNK = 64
CONV_W = 31
GLA_HEADS = 4
GATE_TAU = 16.0
LN_EPS = 1e-5
RMS_EPS = 1e-6
TOP_K = 2

HIST_PAD = 32
CONV_ROWS = 32
ROW_DMA_UNROLL = 8
SEQ_PER_STEP = 4
VMEM_LIMIT = 56 * 1024 * 1024

BF16 = jnp.bfloat16
F32 = jnp.float32


def _mm(a, b):
    return jnp.dot(a, b, preferred_element_type=F32)


def _sigmoid(x):
    return 0.5 * jnp.tanh(0.5 * x) + 0.5


def _silu(x):
    return x * _sigmoid(x)


def _log_sigmoid(z):
    return -(jnp.maximum(-z, 0.0) + jnp.log(1.0 + jnp.exp(-jnp.abs(z))))


def _layer_norm(x, g, b):
    mu = jnp.mean(x, axis=-1, keepdims=True)
    xc = x - mu
    var = jnp.mean(xc * xc, axis=-1, keepdims=True)
    return xc * lax.rsqrt(var + LN_EPS) * g + b


def _split3(x):
    hi = x.astype(BF16)
    r1 = x - hi.astype(F32)
    mid = r1.astype(BF16)
    lo = (r1 - mid.astype(F32)).astype(BF16)
    return hi, mid, lo


def _full_spec(shape):
    zeros = (0,) * len(shape)
    return pl.BlockSpec(shape, lambda *_: zeros, pipeline_mode=pl.Buffered(1))


def _project(xb, w, dims, u_store, q_s, k_s, v_s, lf_s):
    dc, dk, dv = dims["dc"], dims["dk"], dims["dv"]
    off_q, off_k, off_v = 2 * dc, 2 * dc + dk, 2 * dc + 2 * dk
    off_g = off_v + dv
    glu = _mm(xb, w["wmain"][:, 0:off_q]) + w["bmain"][:, 0:off_q]
    u_store(glu[:, 0:dc] * _sigmoid(glu[:, dc:off_q]))
    q_s[...] = (_mm(xb, w["wmain"][:, off_q:off_k]) + w["bmain"][:, off_q:off_k]) * (dims["hk"] ** -0.5)
    k_s[...] = _mm(xb, w["wmain"][:, off_k:off_v]) + w["bmain"][:, off_k:off_v]
    v_s[...] = _mm(xb, w["wmain"][:, off_v:off_g]) + w["bmain"][:, off_v:off_g]
    lr = _mm(xb, w["wlr"][...]) + w["blr"][...]
    z = _mm(lr.astype(BF16), w["wgu"][...]) + w["bgu"][...]
    lf_s[...] = _log_sigmoid(z) * (1.0 / GATE_TAU)


def _gla_chunk(q_s, k_s, v_s, lf_s, o_s, rows, c, get_state, set_state, dims):
    hk, hv, dk = dims["hk"], dims["hv"], dims["dk"]
    lf = lf_s[rows, :]
    ri = lax.broadcasted_iota(jnp.int32, (c, c), 0)
    ci = lax.broadcasted_iota(jnp.int32, (c, c), 1)
    causal = ri >= ci
    tri = causal.astype(BF16)
    hi, mid, lo = _split3(lf)
    b = _mm(tri, hi) + _mm(tri, mid) + _mm(tri, lo)
    b_last = b[c - 1:c, :]
    q_t = q_s[rows, :] * jnp.exp(b)
    kk = k_s[rows, :]
    k_t = kk * jnp.exp(-b)
    k_e = kk * jnp.exp(b_last - b)
    d_t = jnp.transpose(jnp.broadcast_to(jnp.exp(b_last), (128, dk)))
    vv = v_s[rows, :]
    for h in range(GLA_HEADS):
        ks = slice(h * hk, (h + 1) * hk)
        vs = slice(h * hv, (h + 1) * hv)
        qh = q_t[:, ks].astype(BF16)
        kh = k_t[:, ks].astype(BF16)
        keh = k_e[:, ks].astype(BF16)
        vh = vv[:, vs].astype(BF16)
        att = lax.dot_general(qh, kh, (((1,), (1,)), ((), ())), preferred_element_type=F32)
        att = jnp.where(causal, att, 0.0)
        s_old = get_state(h)
        o_s[rows, vs] = _mm(qh, s_old.astype(BF16)) + _mm(att.astype(BF16), vh)
        upd = lax.dot_general(keh, vh, (((0,), (0,)), ((), ())), preferred_element_type=F32)
        dcol = d_t[ks, :]
        decay = jnp.concatenate([dcol] * (hv // 128), axis=1)
        set_state(h, decay * s_old + upd)


def _conv_block(win, w8_ref, n):
    wn, ch = win.shape
    base = HIST_PAD - (CONV_W - 1)
    acc = None
    for b in range(8):
        wb = win if b == 0 else pltpu.roll(win, wn - b, axis=0)
        for a in range((base + CONV_W + 7) // 8):
            j = 8 * a + b - base
            if 0 <= j < CONV_W:
                term = w8_ref[8 * j:8 * j + 8, :][None] * wb[8 * a:8 * a + n, :].reshape(n // 8, 8, ch)
                acc = term if acc is None else acc + term
    return acc.reshape(n, ch)


def _tail(x, xb, c, w, dims, o_s):
    d, dc, dk, dv, hv = dims["d"], dims["dc"], dims["dk"], dims["dv"], dims["hv"]
    off_g = 2 * dc + 2 * dk + dv
    off_lr = off_g + dv
    c = _silu(_layer_norm(c + w["convb"][...], w["clng"][...], w["clnb"][...]))
    y_a = _mm(c.astype(BF16), w["wco"][...]) + w["bco"][...]
    g_out = _mm(xb, w["wmain"][:, off_g:off_lr]) + w["bmain"][:, off_g:off_lr]
    heads = []
    for h in range(GLA_HEADS):
        oh = o_s[:, h * hv:(h + 1) * hv]
        ms = jnp.mean(oh * oh, axis=-1, keepdims=True)
        heads.append(oh * lax.rsqrt(ms + RMS_EPS) * w["gng"][...])
    o = jnp.concatenate(heads, axis=1) * _silu(g_out)
    y_b = _mm(o.astype(BF16), w["wgo"][...])
    gates = _sigmoid(_mm(xb, w["wgates"][...]) + w["bgates"][...])
    merged = gates[:, 0:d] * y_a + gates[:, d:2 * d] * y_b
    m = _mm(merged.astype(BF16), w["wo"][...])
    return _layer_norm(dims["alpha"] * x + m, w["ln1g"][...], w["ln1b"][...])


_W_NAMES = ("wmain", "bmain", "wlr", "blr", "wgates", "bgates", "convw", "convb", "clng", "clnb",
            "wco", "bco", "wgu", "bgu", "gng", "wgo", "wo", "ln1g", "ln1b")


def _mixer_prompt_kernel(*refs, dims, tl, nt, n_prompt):
    nw = len(_W_NAMES)
    x_ref, tail_ref = refs[0:2]
    w = dict(zip(_W_NAMES, refs[2:2 + nw]))
    x1_ref, hist_ref, state_ref = refs[2 + nw:5 + nw]
    scratch = refs[5 + nw:]
    i = pl.program_id(0)

    @pl.when(i < n_prompt)
    def _():
        _mixer_prompt_tile(i % nt, x_ref, w, x1_ref, hist_ref, state_ref, scratch, dims, tl, nt)

    @pl.when(i >= n_prompt)
    def _():
        x1_ref[...] = tail_ref[...]


def _mixer_prompt_tile(j, x_ref, w, x1_ref, hist_ref, state_ref, scratch, dims, tl, nt):
    ubuf, qt_s, kt_s, ke_s, vb_s, g_s, gt_s, o_s, c_s, s_s, dt_s = scratch
    d, dc, dk, dv, hk, hv = dims["d"], dims["dc"], dims["dk"], dims["dv"], dims["hk"], dims["hv"]
    off_q, off_k, off_v = 2 * dc, 2 * dc + dk, 2 * dc + 2 * dk
    off_g = off_v + dv
    off_lr = off_g + dv
    nch = tl // CHUNK
    rep = 128 // nch
    wm, bm = w["wmain"], w["bmain"]

    @pl.when(j == 0)
    def _():
        ubuf[0:HIST_PAD, :] = jnp.zeros((HIST_PAD, dc), F32)
        s_s[...] = jnp.zeros(s_s.shape, F32)

    x = x_ref[...]
    xb = x.astype(BF16)

    glu = _mm(xb, wm[:, 0:off_q]) + bm[:, 0:off_q]
    ubuf[HIST_PAD:HIST_PAD + tl, :] = glu[:, 0:dc] * _sigmoid(glu[:, dc:off_q])
    lr = _mm(xb, w["wlr"][...]) + w["blr"][...]
    z = _mm(lr.astype(BF16), w["wgu"][...]) + w["bgu"][...]
    lf = _log_sigmoid(z) * (1.0 / GATE_TAU)
    ri = lax.broadcasted_iota(jnp.int32, (CHUNK, CHUNK), 0)
    ci = lax.broadcasted_iota(jnp.int32, (CHUNK, CHUNK), 1)
    causal = ri >= ci
    tri = causal.astype(BF16)
    parts = _split3(lf)
    b = jnp.concatenate(
        [sum(_mm(tri, p[c * CHUNK:(c + 1) * CHUNK, :]) for p in parts) for c in range(nch)], axis=0)
    b3 = b.reshape(nch, CHUNK, dk)
    b_last = b3[:, CHUNK - 1:CHUNK, :]
    q = (_mm(xb, wm[:, off_q:off_k]) + bm[:, off_q:off_k]) * (hk ** -0.5)
    qt_s[...] = (q * jnp.exp(b)).astype(BF16)
    kk = _mm(xb, wm[:, off_k:off_v]) + bm[:, off_k:off_v]
    kt_s[...] = (kk * jnp.exp(-b)).astype(BF16)
    ke_s[...] = (kk.reshape(nch, CHUNK, dk) * jnp.exp(b_last - b3)).reshape(tl, dk).astype(BF16)
    vb_s[...] = (_mm(xb, wm[:, off_v:off_g]) + bm[:, off_v:off_g]).astype(BF16)
    d_rows = jnp.broadcast_to(jnp.exp(b_last), (nch, rep, dk)).reshape(nch * rep, dk)
    if nch * rep < 128:
        d_rows = jnp.concatenate([d_rows, jnp.zeros((128 - nch * rep, dk), F32)], axis=0)
    dt_s[...] = jnp.transpose(d_rows)
    g_s[...] = _silu(_mm(xb, wm[:, off_g:off_lr]) + bm[:, off_g:off_lr])

    states = [s_s[h] for h in range(GLA_HEADS)]
    conv_per = tl // CONV_ROWS // nch
    gw = 2 * d // nch
    for c in range(nch):
        rows = slice(c * CHUNK, (c + 1) * CHUNK)
        for h in range(GLA_HEADS):
            ks = slice(h * hk, (h + 1) * hk)
            vs = slice(h * hv, (h + 1) * hv)
            qh, kh, keh, vh = qt_s[rows, ks], kt_s[rows, ks], ke_s[rows, ks], vb_s[rows, vs]
            att = lax.dot_general(qh, kh, (((1,), (1,)), ((), ())), preferred_element_type=F32)
            att = jnp.where(causal, att, 0.0)
            s_old = states[h]
            o_s[rows, vs] = _mm(qh, s_old.astype(BF16)) + _mm(att.astype(BF16), vh)
            upd = lax.dot_general(keh, vh, (((0,), (0,)), ((), ())), preferred_element_type=F32)
            decay = jnp.broadcast_to(dt_s[ks, c * rep:c * rep + 1], (hk, hv))
            states[h] = decay * s_old + upd
        for bi in range(c * conv_per, (c + 1) * conv_per):
            r0 = bi * CONV_ROWS
            c_s[r0:r0 + CONV_ROWS, :] = _conv_block(
                ubuf[r0:r0 + CONV_ROWS + HIST_PAD, :], w["convw"], CONV_ROWS)
        cols = slice(c * gw, (c + 1) * gw)
        gt_s[:, cols] = _sigmoid(_mm(xb, w["wgates"][:, cols]) + w["bgates"][:, cols])
    for h in range(GLA_HEADS):
        s_s[h] = states[h]

    cv = _silu(_layer_norm(c_s[...] + w["convb"][...], w["clng"][...], w["clnb"][...]))
    y_a = _mm(cv.astype(BF16), w["wco"][...]) + w["bco"][...]
    heads = []
    for h in range(GLA_HEADS):
        oh = o_s[:, h * hv:(h + 1) * hv]
        ms = jnp.mean(oh * oh, axis=-1, keepdims=True)
        heads.append(oh * lax.rsqrt(ms + RMS_EPS) * w["gng"][...])
    o = jnp.concatenate(heads, axis=1) * g_s[...]
    y_b = _mm(o.astype(BF16), w["wgo"][...])
    merged = gt_s[:, 0:d] * y_a + gt_s[:, d:2 * d] * y_b
    m = _mm(merged.astype(BF16), w["wo"][...])
    x1_ref[...] = _layer_norm(dims["alpha"] * x + m, w["ln1g"][...], w["ln1b"][...])

    @pl.when(j == nt - 1)
    def _():
        hist_ref[0] = ubuf[tl + HIST_PAD - (CONV_W - 1):tl + HIST_PAD, :]
        state_ref[0] = s_s[...]

    ubuf[0:HIST_PAD, :] = ubuf[tl:tl + HIST_PAD, :]


def _mixer_sample_kernel(*refs, dims, ns, ls):
    nw = len(_W_NAMES)
    x_ref, hist_in_ref, state_in_ref = refs[0:3]
    w = dict(zip(_W_NAMES, refs[3:3 + nw]))
    x1_ref, hist_ref, state_ref = refs[3 + nw:6 + nw]
    ubuf, q_s, k_s, v_s, lf_s, o_s, c_s = refs[6 + nw:]
    i = pl.program_id(0)
    dc = dims["dc"]
    hl = CONV_W - 1

    @pl.when(i == 0)
    def _():
        xb = x_ref[...].astype(BF16)
        ubuf[:, 0:8, :] = jnp.zeros((ns, 8, dc), F32)
        ubuf[:, HIST_PAD - hl:HIST_PAD, :] = hist_in_ref[0]

        def u_store(u):
            ubuf[:, HIST_PAD:HIST_PAD + ls, :] = u.reshape(ns, ls, dc)

        _project(xb, w, dims, u_store, q_s, k_s, v_s, lf_s)

    for s in range(SEQ_PER_STEP):
        seq = i * SEQ_PER_STEP + s
        rows = pl.ds(pl.multiple_of(seq * ls, ls), ls)

        def set_state(h, val, s=s):
            state_ref[s, h] = val

        _gla_chunk(q_s, k_s, v_s, lf_s, o_s, rows, ls, lambda h, s=s: state_in_ref[0, s, h], set_state, dims)
        win = ubuf[seq]
        c_s[rows, :] = _conv_block(win, w["convw"], ls)
        hist_ref[seq] = win[HIST_PAD + ls - hl:HIST_PAD + ls, :]

    @pl.when(i == ns // SEQ_PER_STEP - 1)
    def _():
        x = x_ref[...]
        x1_ref[...] = _tail(x, x.astype(BF16), c_s[...], w, dims, o_s)


def _mixer_weights(l, p, dims):
    dc, dk, dv, rank = dims["dc"], dims["dk"], dims["dv"], dims["rank"]
    off_lr = 2 * dc + 2 * dk + 2 * dv
    off_gates = off_lr + rank
    w_in, b_in = p["w_in"][l], p["b_in"][l]
    row = lambda v: v.reshape(1, -1).astype(F32)
    return (
        w_in[:, :off_lr].astype(BF16), row(b_in[:off_lr]),
        w_in[:, off_lr:off_gates].astype(BF16), row(b_in[off_lr:off_gates]),
        w_in[:, off_gates:].astype(BF16), row(b_in[off_gates:]),
        jnp.repeat(p["conv_w"][l].astype(F32), 8, axis=0),
        row(p["conv_b"][l]), row(p["conv_ln_g"][l]), row(p["conv_ln_b"][l]),
        p["w_conv_out"][l].astype(BF16), row(p["b_conv_out"][l]),
        p["w_gate_up"][l].astype(BF16), row(p["b_gate"][l]),
        row(p["gla_norm_g"][l]),
        p["w_gla_out"][l].astype(BF16), p["w_o"][l].astype(BF16),
        row(p["ln1_g"][l]), row(p["ln1_b"][l]),
    )


def _mixer_prompt(x, x1_tail, wts, dims, tl, bsz, seq):
    d = x.shape[1]
    dc, dk, dv, hk, hv = dims["dc"], dims["dk"], dims["dv"], dims["hk"], dims["hv"]
    hl = CONV_W - 1
    nt = seq // tl
    n_prompt = bsz * nt
    n_tail = x1_tail.shape[0] // tl
    kern = functools.partial(_mixer_prompt_kernel, dims=dims, tl=tl, nt=nt, n_prompt=n_prompt)
    seq_of = lambda i: jnp.minimum(i // nt, bsz - 1)
    return pl.pallas_call(
        kern,
        grid=(n_prompt + n_tail,),
        in_specs=[pl.BlockSpec((tl, d), lambda i: (jnp.minimum(i, n_prompt - 1), 0)),
                  pl.BlockSpec((tl, d), lambda i: (jnp.maximum(i - n_prompt, 0), 0))]
        + [_full_spec(a.shape) for a in wts],
        out_specs=[
            pl.BlockSpec((tl, d), lambda i: (i, 0)),
            pl.BlockSpec((1, hl, dc), lambda i: (seq_of(i), 0, 0)),
            pl.BlockSpec((1, GLA_HEADS, hk, hv), lambda i: (seq_of(i), 0, 0, 0)),
        ],
        out_shape=[
            jax.ShapeDtypeStruct(((n_prompt + n_tail) * tl, d), F32),
            jax.ShapeDtypeStruct((bsz, hl, dc), F32),
            jax.ShapeDtypeStruct((bsz, GLA_HEADS, hk, hv), F32),
        ],
        scratch_shapes=[
            pltpu.VMEM((HIST_PAD + tl, dc), F32),
            pltpu.VMEM((tl, dk), BF16), pltpu.VMEM((tl, dk), BF16), pltpu.VMEM((tl, dk), BF16),
            pltpu.VMEM((tl, dv), BF16),
            pltpu.VMEM((tl, dv), F32), pltpu.VMEM((tl, 2 * d), F32),
            pltpu.VMEM((tl, dv), F32), pltpu.VMEM((tl, dc), F32),
            pltpu.VMEM((GLA_HEADS, hk, hv), F32), pltpu.VMEM((dk, 128), F32),
        ],
        compiler_params=pltpu.CompilerParams(
            dimension_semantics=("arbitrary",), vmem_limit_bytes=VMEM_LIMIT),
        name="mixer_prompt",
    )(x, x1_tail, *wts)


def _mixer_sample(x, in_blk, hist, state, layer, wts, dims, ns, ls):
    d = x.shape[1]
    dc, dk, dv, hk, hv = dims["dc"], dims["dk"], dims["dv"], dims["hk"], dims["hv"]
    hl = CONV_W - 1
    t = ns * ls
    kern = functools.partial(_mixer_sample_kernel, dims=dims, ns=ns, ls=ls)
    x1, hist_o, state_o = pl.pallas_call(
        kern,
        grid=(ns // SEQ_PER_STEP,),
        in_specs=[
            pl.BlockSpec((t, d), lambda i: (in_blk, 0)),
            pl.BlockSpec((1, ns, hl, dc), lambda i: (layer, 0, 0, 0)),
            pl.BlockSpec((1, SEQ_PER_STEP, GLA_HEADS, hk, hv), lambda i: (layer, i, 0, 0, 0)),
        ] + [_full_spec(a.shape) for a in wts],
        out_specs=[
            pl.BlockSpec((t, d), lambda i: (0, 0)),
            pl.BlockSpec((ns, hl, dc), lambda i: (0, 0, 0)),
            pl.BlockSpec((SEQ_PER_STEP, GLA_HEADS, hk, hv), lambda i: (i, 0, 0, 0)),
        ],
        out_shape=[
            jax.ShapeDtypeStruct((t, d), F32),
            jax.ShapeDtypeStruct((ns, hl, dc), F32),
            jax.ShapeDtypeStruct((ns, GLA_HEADS, hk, hv), F32),
        ],
        scratch_shapes=[
            pltpu.VMEM((ns, HIST_PAD + ls, dc), F32),
            pltpu.VMEM((t, dk), F32), pltpu.VMEM((t, dk), F32), pltpu.VMEM((t, dv), F32),
            pltpu.VMEM((t, dk), F32), pltpu.VMEM((t, dv), F32), pltpu.VMEM((t, dc), F32),
        ],
        compiler_params=pltpu.CompilerParams(
            dimension_semantics=("arbitrary",), vmem_limit_bytes=VMEM_LIMIT),
        name="mixer_sample",
    )(x, hist, state, *wts)
    return x1, hist_o, state_o


def _token_out(t, d, tm, split):
    if split is None:
        return [pl.BlockSpec((tm, d), lambda i: (i, 0))], [jax.ShapeDtypeStruct((t, d), F32)]
    tp, ts = split
    npt = tp // tm
    specs = [pl.BlockSpec((tm, d), lambda i: (jnp.minimum(i, npt - 1), 0)),
             pl.BlockSpec((tm, d), lambda i: (jnp.maximum(i - npt, 0), 0))]
    return specs, [jax.ShapeDtypeStruct((tp, d), F32), jax.ShapeDtypeStruct((ts, d), F32)]


def _token_store(o_refs, val, npt):
    if len(o_refs) == 1:
        o_refs[0][...] = val
        return
    i = pl.program_id(0)

    @pl.when(i < npt)
    def _():
        o_refs[0][...] = val

    @pl.when(i >= npt)
    def _():
        o_refs[1][...] = val


def _ffn_dense_kernel(x_ref, wg_ref, wu_ref, wd_ref, g_ref, b_ref, *rest, alpha, npt, n_cast):
    cast_in, o_refs, cast_out = rest[:n_cast], rest[n_cast:len(rest) - n_cast], rest[len(rest) - n_cast:]
    _cast_blocks(cast_in, cast_out)
    x = x_ref[...]
    xb = x.astype(BF16)
    h = _silu(_mm(xb, wg_ref[...])) * _mm(xb, wu_ref[...])
    f = _mm(h.astype(BF16), wd_ref[...])
    _token_store(o_refs, _layer_norm(alpha * x + f, g_ref[...], b_ref[...]), npt)


def _ffn_dense(x, wg, wu, wd, g, b, alpha, tm, split, to_cast):
    t, d = x.shape
    ops = (wg.astype(BF16), wu.astype(BF16), wd.astype(BF16), g.reshape(1, d), b.reshape(1, d))
    out_specs, out_shape = _token_out(t, d, tm, split)
    npt = None if split is None else split[0] // tm
    c_in, c_out, c_shapes = _cast_plan(to_cast, t // tm) if to_cast else ([], [], [])
    res = pl.pallas_call(
        functools.partial(_ffn_dense_kernel, alpha=alpha, npt=npt, n_cast=len(to_cast)),
        grid=(t // tm,),
        in_specs=[pl.BlockSpec((tm, d), lambda i: (i, 0))] + [_full_spec(a.shape) for a in ops] + c_in,
        out_specs=out_specs + c_out,
        out_shape=out_shape + c_shapes,
        compiler_params=pltpu.CompilerParams(
            dimension_semantics=("arbitrary",), vmem_limit_bytes=VMEM_LIMIT),
        name="ffn_dense",
    )(x, *ops, *to_cast)
    return res[:len(out_shape)], res[len(out_shape):]


def _cast_plan(arrays, n_steps):
    n_blocks = max(n for n in range(1, n_steps + 1)
                   if all(a.shape[0] % n == 0 and (a.shape[0] // n) % 16 == 0 for a in arrays))
    spec = lambda a: pl.BlockSpec((a.shape[0] // n_blocks, a.shape[1]), lambda i: (jnp.minimum(i, n_blocks - 1), 0))
    return ([spec(a) for a in arrays], [spec(a) for a in arrays],
            [jax.ShapeDtypeStruct(a.shape, BF16) for a in arrays])


def _cast_blocks(src_refs, dst_refs):
    for src, dst in zip(src_refs, dst_refs):
        dst[...] = src[...].astype(BF16)


def _router_kernel(x_ref, wr_ref, *rest, ne, tr, n_cast):
    cast_in, (route_ref, cnt_ref), cast_out = rest[:n_cast], rest[n_cast:n_cast + 2], rest[n_cast + 2:-1]
    carry = rest[-1]
    i = pl.program_id(0)
    _cast_blocks(cast_in, cast_out)

    @pl.when(i == 0)
    def _():
        carry[...] = jnp.zeros(carry.shape, F32)

    xh = x_ref[...]
    x_hi = xh.astype(BF16)
    x_lo = (xh - x_hi.astype(F32)).astype(BF16)
    wr = wr_ref[...]
    w_hi = wr.astype(BF16)
    w_lo = (wr - w_hi.astype(F32)).astype(BF16)
    nt = (((1,), (1,)), ((), ()))
    logits = (lax.dot_general(w_hi, x_hi, nt, preferred_element_type=F32)
              + lax.dot_general(w_hi, x_lo, nt, preferred_element_type=F32)
              + lax.dot_general(w_lo, x_hi, nt, preferred_element_type=F32))
    mx = jnp.max(logits, axis=0, keepdims=True)
    ex = jnp.exp(logits - mx)
    probs = ex / jnp.sum(ex, axis=0, keepdims=True)
    eid = lax.broadcasted_iota(jnp.int32, (ne, tr), 0)
    p1 = jnp.max(probs, axis=0, keepdims=True)
    i1 = jnp.min(jnp.where(probs == p1, eid, ne), axis=0, keepdims=True)
    rest = jnp.where(eid == i1, -1.0, probs)
    p2 = jnp.max(rest, axis=0, keepdims=True)
    i2 = jnp.min(jnp.where(rest == p2, eid, ne), axis=0, keepdims=True)
    den = p1 + p2
    oh1 = (eid == i1).astype(F32)
    oh2 = (eid == i2).astype(F32)
    oh = oh1 + oh2
    ri = lax.broadcasted_iota(jnp.int32, (tr, tr), 0)
    ci = lax.broadcasted_iota(jnp.int32, (tr, tr), 1)
    upper = (ri <= ci).astype(BF16)
    incl = _mm(oh.astype(BF16), upper)
    before = carry[:, 0:1] + incl - oh
    r1 = jnp.sum(oh1 * before, axis=0, keepdims=True)
    r2 = jnp.sum(oh2 * before, axis=0, keepdims=True)
    zero = jnp.zeros((1, tr), F32)
    route_ref[...] = jnp.concatenate(
        [i1.astype(F32), i2.astype(F32), p1 / den, p2 / den, r1, r2, zero, zero], axis=0)
    total = carry[:, 0:1] + incl[:, tr - 1:tr]
    carry[...] = jnp.broadcast_to(total, carry.shape)
    cnt_ref[...] = jnp.broadcast_to(total, cnt_ref.shape)


def _router(x, w_router, tr, to_cast):
    t, d = x.shape
    ne = w_router.shape[1]
    c_in, c_out, c_shapes = _cast_plan(to_cast, t // tr)
    res = pl.pallas_call(
        functools.partial(_router_kernel, ne=ne, tr=tr, n_cast=len(to_cast)),
        grid=(t // tr,),
        in_specs=[pl.BlockSpec((tr, d), lambda i: (i, 0)), _full_spec((ne, d))] + c_in,
        out_specs=[pl.BlockSpec((8, tr), lambda i: (0, i)), pl.BlockSpec((ne, 128), lambda i: (0, 0))] + c_out,
        out_shape=[jax.ShapeDtypeStruct((8, t), F32), jax.ShapeDtypeStruct((ne, 128), F32)] + c_shapes,
        scratch_shapes=[pltpu.VMEM((ne, 128), F32)],
        compiler_params=pltpu.CompilerParams(dimension_semantics=("arbitrary",), vmem_limit_bytes=VMEM_LIMIT),
        name="moe_router",
    )(x, w_router.T.astype(F32), *to_cast)
    return res[0], res[1], res[2:]


def _dispatch_kernel(pos_ref, grp_ref, x_ref, *rest, tm, ne, n_tiles, n_cast):
    cast_in, xs_ref, cast_out, sem = rest[:n_cast], rest[n_cast], rest[n_cast + 1:-1], rest[-1]
    i = pl.program_id(0)

    def row_copy(r, k):
        return pltpu.make_async_copy(x_ref.at[pl.ds(r, 1), :], xs_ref.at[pl.ds(pos_ref[0, 0, 2 * r + k], 1), :], sem)

    def start(g, carry):
        for u in range(ROW_DMA_UNROLL):
            row_copy(g * ROW_DMA_UNROLL + u, 0).start()
            row_copy(g * ROW_DMA_UNROLL + u, 1).start()
        return carry

    lax.fori_loop(0, tm // ROW_DMA_UNROLL, start, 0)
    _cast_blocks(cast_in, cast_out)
    for _ in range(TOP_K):
        pltpu.make_async_copy(x_ref, xs_ref.at[pl.ds(0, tm), :], sem).wait()

    @pl.when(i == pl.num_programs(0) - 1)
    def _():
        for e in range(ne):
            lo, hi = grp_ref[e] + grp_ref[ne + e], grp_ref[e] + grp_ref[2 * ne + e]

            def pad_copy(r):
                return pltpu.make_async_copy(x_ref.at[pl.ds(0, 1), :], xs_ref.at[pl.ds(r, 1), :], sem)

            lax.fori_loop(lo, hi, lambda r, c: (pad_copy(r).start(), c)[1], 0)
            lax.fori_loop(lo, hi, lambda r, c: (pad_copy(r).wait(), c)[1], 0)

        def tile_copy(j):
            return pltpu.make_async_copy(x_ref, xs_ref.at[pl.ds(j * tm, tm), :], sem)

        lax.fori_loop(grp_ref[3 * ne], n_tiles, lambda j, c: (tile_copy(j).start(), c)[1], 0)
        lax.fori_loop(grp_ref[3 * ne], n_tiles, lambda j, c: (tile_copy(j).wait(), c)[1], 0)


def _dispatch(x, pos, grp, n_tiles, tm, to_cast):
    t, d = x.shape
    ne = (grp.shape[0] - 1) // 3
    c_in, c_out, c_shapes = _cast_plan(to_cast, t // tm)
    res = pl.pallas_call(
        functools.partial(_dispatch_kernel, tm=tm, ne=ne, n_tiles=n_tiles, n_cast=len(to_cast)),
        grid=(t // tm,),
        in_specs=[
            pl.BlockSpec((1, 1, 2 * tm), lambda i: (i, 0, 0), memory_space=pltpu.SMEM),
            pl.BlockSpec(memory_space=pltpu.SMEM),
            pl.BlockSpec((tm, d), lambda i: (i, 0)),
        ] + c_in,
        out_specs=[pl.BlockSpec(memory_space=pl.ANY)] + c_out,
        out_shape=[jax.ShapeDtypeStruct((n_tiles * tm, d), F32)] + c_shapes,
        scratch_shapes=[pltpu.SemaphoreType.DMA],
        compiler_params=pltpu.CompilerParams(
            dimension_semantics=("arbitrary",), has_side_effects=True, vmem_limit_bytes=VMEM_LIMIT),
        name="moe_dispatch",
    )(pos.reshape(t // tm, 1, 2 * tm), grp, x, *to_cast)
    return res[0], res[1:]


def _experts_kernel(te_ref, nv_ref, xs_ref, wg_ref, wu_ref, wd_ref, ys_ref):
    i = pl.program_id(0)

    @pl.when(i < nv_ref[0])
    def _():
        xb = xs_ref[...].astype(BF16)
        h = _silu(_mm(xb, wg_ref[0])) * _mm(xb, wu_ref[0])
        ys_ref[...] = _mm(h.astype(BF16), wd_ref[0])

    @pl.when(i >= nv_ref[0])
    def _():
        ys_ref[...] = jnp.zeros(ys_ref.shape, F32)


def _experts(xs, tile_expert, n_valid, wg, wu, wd, tm):
    n_rows, d = xs.shape
    ne, _, ff = wg.shape
    grid_spec = pltpu.PrefetchScalarGridSpec(
        num_scalar_prefetch=2,
        grid=(n_rows // tm,),
        in_specs=[
            pl.BlockSpec((tm, d), lambda i, te, nv: (i, 0)),
            pl.BlockSpec((1, d, ff), lambda i, te, nv: (te[i], 0, 0)),
            pl.BlockSpec((1, d, ff), lambda i, te, nv: (te[i], 0, 0)),
            pl.BlockSpec((1, ff, d), lambda i, te, nv: (te[i], 0, 0)),
        ],
        out_specs=pl.BlockSpec((tm, d), lambda i, te, nv: (i, 0)),
    )
    return pl.pallas_call(
        _experts_kernel,
        grid_spec=grid_spec,
        out_shape=jax.ShapeDtypeStruct((n_rows, d), F32),
        compiler_params=pltpu.CompilerParams(
            dimension_semantics=("arbitrary",), vmem_limit_bytes=VMEM_LIMIT),
        name="moe_experts",
    )(tile_expert, n_valid, xs, wg, wu, wd)


def _combine_kernel(pos_ref, pos_next_ref, x_ref, rt_ref, ys_ref, g_ref, b_ref, *rest, tm, alpha, npt):
    o_refs, (buf, sem) = rest[:-2], rest[-2:]
    i = pl.program_id(0)
    n = pl.num_programs(0)
    slot = i % 2

    def gather(p_ref, s):
        def row_copy(r, k):
            return pltpu.make_async_copy(
                ys_ref.at[pl.ds(p_ref[0, 0, 2 * r + k], 1), :], buf.at[s, k, pl.ds(r, 1), :], sem.at[s])

        def start(g, carry):
            for u in range(ROW_DMA_UNROLL):
                row_copy(g * ROW_DMA_UNROLL + u, 0).start()
                row_copy(g * ROW_DMA_UNROLL + u, 1).start()
            return carry

        lax.fori_loop(0, tm // ROW_DMA_UNROLL, start, 0)

    @pl.when(i == 0)
    def _():
        gather(pos_ref, 0)

    @pl.when(i + 1 < n)
    def _():
        gather(pos_next_ref, 1 - slot)

    for k in range(TOP_K):
        pltpu.make_async_copy(ys_ref.at[pl.ds(0, tm), :], buf.at[slot, k], sem.at[slot]).wait()
    rt = rt_ref[...]
    f = rt[:, 2:3] * buf[slot, 0] + rt[:, 3:4] * buf[slot, 1]
    _token_store(o_refs, _layer_norm(alpha * x_ref[...] + f, g_ref[...], b_ref[...]), npt)


def _combine(x, pos, route_t, ys, g, b, alpha, tm, split):
    t, d = x.shape
    out_specs, out_shape = _token_out(t, d, tm, split)
    npt = None if split is None else split[0] // tm
    n_steps = t // tm
    pos3 = pos.reshape(n_steps, 1, 2 * tm)
    return pl.pallas_call(
        functools.partial(_combine_kernel, tm=tm, alpha=alpha, npt=npt),
        grid=(t // tm,),
        in_specs=[
            pl.BlockSpec((1, 1, 2 * tm), lambda i: (i, 0, 0), memory_space=pltpu.SMEM),
            pl.BlockSpec((1, 1, 2 * tm), lambda i: (jnp.minimum(i + 1, n_steps - 1), 0, 0), memory_space=pltpu.SMEM),
            pl.BlockSpec((tm, d), lambda i: (i, 0)),
            pl.BlockSpec((tm, 8), lambda i: (i, 0)),
            pl.BlockSpec(memory_space=pl.ANY),
            _full_spec((1, d)), _full_spec((1, d)),
        ],
        out_specs=out_specs,
        out_shape=out_shape,
        scratch_shapes=[pltpu.VMEM((2, TOP_K, tm, d), F32), pltpu.SemaphoreType.DMA((2,))],
        compiler_params=pltpu.CompilerParams(dimension_semantics=("arbitrary",), vmem_limit_bytes=VMEM_LIMIT),
        name="moe_combine",
    )(pos3, pos3, x, route_t, ys, g.reshape(1, d), b.reshape(1, d))


def _ffn_moe(x, w_router, wg, wu, wd, g, b, alpha, tr, tm, split, pre):
    t, d = x.shape
    ne = w_router.shape[1]
    ff = wg.shape[2]
    assert tr == tm
    flat = {"wg": wg.reshape(ne * d, ff), "wu": wu.reshape(ne * d, ff), "wd": wd.reshape(ne * ff, d)}
    conv = dict(pre)
    r_keys = [k for k in ("wd",) if k not in conv]
    d_keys = [k for k in ("wg", "wu") if k not in conv]
    route, counts, r_out = _router(x, w_router, tr, [flat[k] for k in r_keys])
    conv.update(zip(r_keys, r_out))
    cnt = counts[:, 0].astype(jnp.int32)
    gsz = ((cnt + tm - 1) // tm) * tm
    ends = jnp.cumsum(gsz)
    offs = ends - gsz
    n_tiles = (TOP_K * t) // tm + ne
    tile_start = jnp.arange(n_tiles, dtype=jnp.int32) * tm
    tile_e = jnp.sum((tile_start[:, None] >= ends[None, :]).astype(jnp.int32), axis=1)
    n_valid = (ends[ne - 1] // tm).astype(jnp.int32).reshape(1)
    last_e = jnp.sum((ends[ne - 1] - 1 >= ends).astype(jnp.int32))
    tile_e = jnp.minimum(tile_e, last_e).astype(jnp.int32)
    i12 = route[0:2].astype(jnp.int32)
    base = sum(jnp.where(i12 == e, offs[e], 0) for e in range(ne))
    pos = (base + route[4:6].astype(jnp.int32)).T.reshape(-1)
    grp = jnp.concatenate([offs, cnt, gsz, n_valid]).astype(jnp.int32)
    xs, d_out = _dispatch(x, pos, grp, n_tiles, tr, [flat[k] for k in d_keys])
    conv.update(zip(d_keys, d_out))
    ys = _experts(xs, tile_e, n_valid, conv["wg"].reshape(ne, d, ff), conv["wu"].reshape(ne, d, ff),
                  conv["wd"].reshape(ne, ff, d), tm)
    return _combine(x, pos, route.T, ys, g, b, alpha, tr, split)


def kernel(x_prompt, x_sample, cache_conv, state_gla, w_in, b_in, conv_w, conv_b, conv_ln_g, conv_ln_b, w_conv_out, b_conv_out, w_gate_up, b_gate, gla_norm_g, w_gla_out, w_o, ln1_g, ln1_b, ln2_g, ln2_b, ff_w_gate, ff_w_up, ff_w_down, w_router, moe_w_gate, moe_w_up, moe_w_down):
    p = dict(w_in=w_in, b_in=b_in, conv_w=conv_w, conv_b=conv_b, conv_ln_g=conv_ln_g, conv_ln_b=conv_ln_b,
             w_conv_out=w_conv_out, b_conv_out=b_conv_out, w_gate_up=w_gate_up, b_gate=b_gate,
             gla_norm_g=gla_norm_g, w_gla_out=w_gla_out, w_o=w_o, ln1_g=ln1_g, ln1_b=ln1_b)
    depth = w_in.shape[0]
    bsz, seq, d = x_prompt.shape
    ns, ls, _ = x_sample.shape
    dc = conv_w.shape[-1]
    rank, dk = w_gate_up.shape[1], w_gate_up.shape[2]
    dv = w_gla_out.shape[1]
    dims = dict(d=d, dc=dc, dk=dk, dv=dv, rank=rank, hk=dk // GLA_HEADS, hv=dv // GLA_HEADS,
                alpha=(2.0 * depth) ** 0.25)
    alpha = dims["alpha"]
    tl = min(512, seq)
    tp = bsz * seq
    ts = ns * ls
    tm = min(512, ts)
    assert seq % tl == 0 and tp % ts == 0 and ts % tm == 0 and ts % tl == 0 and ns % SEQ_PER_STEP == 0

    x_p, x_s, s_blk = x_prompt.reshape(tp, d), x_sample.reshape(ts, d), 0
    hist_p, state_p, hist_s, state_s = [], [], [], []
    for l in range(depth):
        wts = _mixer_weights(l, p, dims)
        x1s, hs, ss = _mixer_sample(x_s, s_blk, cache_conv, state_gla, l, wts, dims, ns, ls)
        x1, hp, sp = _mixer_prompt(x_p, x1s, wts, dims, tl, bsz, seq)
        hist_p.append(hp), state_p.append(sp), hist_s.append(hs), state_s.append(ss)
        split = (tp, ts) if l == depth - 1 else None
        if l % 2 == 0:
            nxt = {}
            if l + 1 < depth:
                mg, md = moe_w_gate[l // 2], moe_w_down[l // 2]
                nxt = {"wg": mg.reshape(-1, mg.shape[-1]), "wd": md.reshape(-1, md.shape[-1])}
            x2, done = _ffn_dense(x1, ff_w_gate[l // 2], ff_w_up[l // 2], ff_w_down[l // 2], ln2_g[l], ln2_b[l],
                                  alpha, tm, split, list(nxt.values()))
            pre_cast = dict(zip(nxt.keys(), done))
        else:
            x2 = _ffn_moe(x1, w_router[l // 2], moe_w_gate[l // 2], moe_w_up[l // 2], moe_w_down[l // 2],
                          ln2_g[l], ln2_b[l], alpha, tm, tm, split, pre_cast)
        if split is None:
            x_p, x_s, s_blk = x2[0], x2[0], tp // ts
    y_p, y_s = x2
    return (y_p.reshape(bsz, seq, d), y_s.reshape(ns, ls, d), jnp.stack(hist_p),
            jnp.stack(state_p).astype(state_gla.dtype), jnp.stack(hist_s), jnp.stack(state_s).astype(state_gla.dtype))
```

```python
import functools

import jax
import jax.numpy as jnp
from jax import lax
from jax.experimental import pallas as pl
from jax.experimental.pallas import tpu as pltpu

CHUNK = 64
CONV_W = 31
GLA_HEADS = 4
GATE_TAU = 16.0
LN_EPS = 1e-5
RMS_EPS = 1e-6
TOP_K = 2

HIST_PAD = 32
CONV_ROWS = 32
ROW_DMA_UNROLL = 8
SEQ_PER_STEP = 4
EXPERT_TILE = 512
VMEM_LIMIT = 56 * 1024 * 1024

BF16 = jnp.bfloat16
F32 = jnp.float32


def _mm(a, b):
    return jnp.dot(a, b, preferred_element_type=F32)


def _sigmoid(x):
    return 0.5 * jnp.tanh(0.5 * x) + 0.5


def _silu(x):
    return x * _sigmoid(x)


def _log_sigmoid(z):
    return -(jnp.maximum(-z, 0.0) + jnp.log(1.0 + jnp.exp(-jnp.abs(z))))


def _layer_norm(x, g, b):
    mu = jnp.mean(x, axis=-1, keepdims=True)
    xc = x - mu
    var = jnp.mean(xc * xc, axis=-1, keepdims=True)
    return xc * lax.rsqrt(var + LN_EPS) * g + b


def _split3(x):
    hi = x.astype(BF16)
    r1 = x - hi.astype(F32)
    mid = r1.astype(BF16)
    lo = (r1 - mid.astype(F32)).astype(BF16)
    return hi, mid, lo


def _full_spec(shape):
    zeros = (0,) * len(shape)
    return pl.BlockSpec(shape, lambda *_: zeros, pipeline_mode=pl.Buffered(1))


def _project(xb, w, dims, u_store, q_s, k_s, v_s, lf_s):
    dc, dk, dv = dims["dc"], dims["dk"], dims["dv"]
    off_q, off_k, off_v = 2 * dc, 2 * dc + dk, 2 * dc + 2 * dk
    off_g = off_v + dv
    glu = _mm(xb, w["wmain"][:, 0:off_q]) + w["bmain"][:, 0:off_q]
    u_store(glu[:, 0:dc] * _sigmoid(glu[:, dc:off_q]))
    q_s[...] = (_mm(xb, w["wmain"][:, off_q:off_k]) + w["bmain"][:, off_q:off_k]) * (dims["hk"] ** -0.5)
    k_s[...] = _mm(xb, w["wmain"][:, off_k:off_v]) + w["bmain"][:, off_k:off_v]
    v_s[...] = _mm(xb, w["wmain"][:, off_v:off_g]) + w["bmain"][:, off_v:off_g]
    lr = _mm(xb, w["wlr"][...]) + w["blr"][...]
    z = _mm(lr.astype(BF16), w["wgu"][...]) + w["bgu"][...]
    lf_s[...] = _log_sigmoid(z) * (1.0 / GATE_TAU)


def _gla_chunk(q_s, k_s, v_s, lf_s, o_s, rows, c, get_state, set_state, dims):
    hk, hv, dk = dims["hk"], dims["hv"], dims["dk"]
    lf = lf_s[rows, :]
    ri = lax.broadcasted_iota(jnp.int32, (c, c), 0)
    ci = lax.broadcasted_iota(jnp.int32, (c, c), 1)
    causal = ri >= ci
    tri = causal.astype(BF16)
    hi, mid, lo = _split3(lf)
    b = _mm(tri, hi) + _mm(tri, mid) + _mm(tri, lo)
    b_last = b[c - 1:c, :]
    q_t = q_s[rows, :] * jnp.exp(b)
    kk = k_s[rows, :]
    k_t = kk * jnp.exp(-b)
    k_e = kk * jnp.exp(b_last - b)
    d_t = jnp.transpose(jnp.broadcast_to(jnp.exp(b_last), (128, dk)))
    vv = v_s[rows, :]
    for h in range(GLA_HEADS):
        ks = slice(h * hk, (h + 1) * hk)
        vs = slice(h * hv, (h + 1) * hv)
        qh = q_t[:, ks].astype(BF16)
        kh = k_t[:, ks].astype(BF16)
        keh = k_e[:, ks].astype(BF16)
        vh = vv[:, vs].astype(BF16)
        att = lax.dot_general(qh, kh, (((1,), (1,)), ((), ())), preferred_element_type=F32)
        att = jnp.where(causal, att, 0.0)
        s_old = get_state(h)
        o_s[rows, vs] = _mm(qh, s_old.astype(BF16)) + _mm(att.astype(BF16), vh)
        upd = lax.dot_general(keh, vh, (((0,), (0,)), ((), ())), preferred_element_type=F32)
        dcol = d_t[ks, :]
        decay = jnp.concatenate([dcol] * (hv // 128), axis=1)
        set_state(h, decay * s_old + upd)


def _conv_block(win, w8_ref, n):
    wn, ch = win.shape
    base = HIST_PAD - (CONV_W - 1)
    acc = None
    for b in range(8):
        wb = win if b == 0 else pltpu.roll(win, wn - b, axis=0)
        for a in range((base + CONV_W + 7) // 8):
            j = 8 * a + b - base
            if 0 <= j < CONV_W:
                term = w8_ref[8 * j:8 * j + 8, :][None] * wb[8 * a:8 * a + n, :].reshape(n // 8, 8, ch)
                acc = term if acc is None else acc + term
    return acc.reshape(n, ch)


def _tail(x, xb, c, w, dims, o_s):
    d, dc, dk, dv, hv = dims["d"], dims["dc"], dims["dk"], dims["dv"], dims["hv"]
    off_g = 2 * dc + 2 * dk + dv
    off_lr = off_g + dv
    c = _silu(_layer_norm(c + w["convb"][...], w["clng"][...], w["clnb"][...]))
    y_a = _mm(c.astype(BF16), w["wco"][...]) + w["bco"][...]
    g_out = _mm(xb, w["wmain"][:, off_g:off_lr]) + w["bmain"][:, off_g:off_lr]
    heads = []
    for h in range(GLA_HEADS):
        oh = o_s[:, h * hv:(h + 1) * hv]
        ms = jnp.mean(oh * oh, axis=-1, keepdims=True)
        heads.append(oh * lax.rsqrt(ms + RMS_EPS) * w["gng"][...])
    o = jnp.concatenate(heads, axis=1) * _silu(g_out)
    y_b = _mm(o.astype(BF16), w["wgo"][...])
    gates = _sigmoid(_mm(xb, w["wgates"][...]) + w["bgates"][...])
    merged = gates[:, 0:d] * y_a + gates[:, d:2 * d] * y_b
    m = _mm(merged.astype(BF16), w["wo"][...])
    return _layer_norm(dims["alpha"] * x + m, w["ln1g"][...], w["ln1b"][...])


_W_NAMES = ("wmain", "bmain", "wlr", "blr", "wgates", "bgates", "convw", "convb", "clng", "clnb",
            "wco", "bco", "wgu", "bgu", "gng", "wgo", "wo", "ln1g", "ln1b")


def _mixer_prompt_kernel(*refs, dims, tl, nt, n_prompt, n_cast):
    nw = len(_W_NAMES)
    x_ref, tail_ref = refs[0:2]
    w = dict(zip(_W_NAMES, refs[2:2 + nw]))
    cast_in = refs[2 + nw:2 + nw + n_cast]
    x1_ref, hist_ref, state_ref = refs[2 + nw + n_cast:5 + nw + n_cast]
    cast_out = refs[5 + nw + n_cast:5 + nw + 2 * n_cast]
    scratch = refs[5 + nw + 2 * n_cast:]
    i = pl.program_id(0)
    _cast_blocks(cast_in, cast_out)

    @pl.when(i < n_prompt)
    def _():
        _mixer_prompt_tile(i % nt, x_ref, w, x1_ref, hist_ref, state_ref, scratch, dims, tl, nt)

    @pl.when(i >= n_prompt)
    def _():
        x1_ref[...] = tail_ref[...]


def _mixer_prompt_tile(j, x_ref, w, x1_ref, hist_ref, state_ref, scratch, dims, tl, nt):
    ubuf, qt_s, kt_s, ke_s, vb_s, g_s, gt_s, o_s, c_s, s_s, dt_s = scratch
    d, dc, dk, dv, hk, hv = dims["d"], dims["dc"], dims["dk"], dims["dv"], dims["hk"], dims["hv"]
    off_q, off_k, off_v = 2 * dc, 2 * dc + dk, 2 * dc + 2 * dk
    off_g = off_v + dv
    off_lr = off_g + dv
    nch = tl // CHUNK
    rep = 128 // nch
    wm, bm = w["wmain"], w["bmain"]

    @pl.when(j == 0)
    def _():
        ubuf[0:HIST_PAD, :] = jnp.zeros((HIST_PAD, dc), F32)
        s_s[...] = jnp.zeros(s_s.shape, F32)

    x = x_ref[...]
    xb = x.astype(BF16)

    glu = _mm(xb, wm[:, 0:off_q]) + bm[:, 0:off_q]
    ubuf[HIST_PAD:HIST_PAD + tl, :] = glu[:, 0:dc] * _sigmoid(glu[:, dc:off_q])
    lr = _mm(xb, w["wlr"][...]) + w["blr"][...]
    z = _mm(lr.astype(BF16), w["wgu"][...]) + w["bgu"][...]
    lf = _log_sigmoid(z) * (1.0 / GATE_TAU)
    ri = lax.broadcasted_iota(jnp.int32, (CHUNK, CHUNK), 0)
    ci = lax.broadcasted_iota(jnp.int32, (CHUNK, CHUNK), 1)
    causal = ri >= ci
    tri = causal.astype(BF16)
    parts = _split3(lf)
    b = jnp.concatenate(
        [sum(_mm(tri, p[c * CHUNK:(c + 1) * CHUNK, :]) for p in parts) for c in range(nch)], axis=0)
    b3 = b.reshape(nch, CHUNK, dk)
    b_last = b3[:, CHUNK - 1:CHUNK, :]
    q = (_mm(xb, wm[:, off_q:off_k]) + bm[:, off_q:off_k]) * (hk ** -0.5)
    qt_s[...] = (q * jnp.exp(b)).astype(BF16)
    kk = _mm(xb, wm[:, off_k:off_v]) + bm[:, off_k:off_v]
    kt_s[...] = (kk * jnp.exp(-b)).astype(BF16)
    ke_s[...] = (kk.reshape(nch, CHUNK, dk) * jnp.exp(b_last - b3)).reshape(tl, dk).astype(BF16)
    vb_s[...] = (_mm(xb, wm[:, off_v:off_g]) + bm[:, off_v:off_g]).astype(BF16)
    d_rows = jnp.broadcast_to(jnp.exp(b_last), (nch, rep, dk)).reshape(nch * rep, dk)
    if nch * rep < 128:
        d_rows = jnp.concatenate([d_rows, jnp.zeros((128 - nch * rep, dk), F32)], axis=0)
    dt_s[...] = jnp.transpose(d_rows)
    g_s[...] = _silu(_mm(xb, wm[:, off_g:off_lr]) + bm[:, off_g:off_lr])

    states = [s_s[h] for h in range(GLA_HEADS)]
    conv_per = tl // CONV_ROWS // nch
    gw = 2 * d // nch
    for c in range(nch):
        rows = slice(c * CHUNK, (c + 1) * CHUNK)
        for h in range(GLA_HEADS):
            ks = slice(h * hk, (h + 1) * hk)
            vs = slice(h * hv, (h + 1) * hv)
            qh, kh, keh, vh = qt_s[rows, ks], kt_s[rows, ks], ke_s[rows, ks], vb_s[rows, vs]
            att = lax.dot_general(qh, kh, (((1,), (1,)), ((), ())), preferred_element_type=F32)
            att = jnp.where(causal, att, 0.0)
            s_old = states[h]
            o_s[rows, vs] = _mm(qh, s_old.astype(BF16)) + _mm(att.astype(BF16), vh)
            upd = lax.dot_general(keh, vh, (((0,), (0,)), ((), ())), preferred_element_type=F32)
            decay = jnp.broadcast_to(dt_s[ks, c * rep:c * rep + 1], (hk, hv))
            states[h] = decay * s_old + upd
        for bi in range(c * conv_per, (c + 1) * conv_per):
            r0 = bi * CONV_ROWS
            c_s[r0:r0 + CONV_ROWS, :] = _conv_block(
                ubuf[r0:r0 + CONV_ROWS + HIST_PAD, :], w["convw"], CONV_ROWS)
        cols = slice(c * gw, (c + 1) * gw)
        gt_s[:, cols] = _sigmoid(_mm(xb, w["wgates"][:, cols]) + w["bgates"][:, cols])
    for h in range(GLA_HEADS):
        s_s[h] = states[h]

    cv = _silu(_layer_norm(c_s[...] + w["convb"][...], w["clng"][...], w["clnb"][...]))
    y_a = _mm(cv.astype(BF16), w["wco"][...]) + w["bco"][...]
    heads = []
    for h in range(GLA_HEADS):
        oh = o_s[:, h * hv:(h + 1) * hv]
        ms = jnp.mean(oh * oh, axis=-1, keepdims=True)
        heads.append(oh * lax.rsqrt(ms + RMS_EPS) * w["gng"][...])
    o = jnp.concatenate(heads, axis=1) * g_s[...]
    y_b = _mm(o.astype(BF16), w["wgo"][...])
    merged = gt_s[:, 0:d] * y_a + gt_s[:, d:2 * d] * y_b
    m = _mm(merged.astype(BF16), w["wo"][...])
    x1_ref[...] = _layer_norm(dims["alpha"] * x + m, w["ln1g"][...], w["ln1b"][...])

    @pl.when(j == nt - 1)
    def _():
        hist_ref[0] = ubuf[tl + HIST_PAD - (CONV_W - 1):tl + HIST_PAD, :]
        state_ref[0] = s_s[...]

    ubuf[0:HIST_PAD, :] = ubuf[tl:tl + HIST_PAD, :]


def _mixer_sample_kernel(*refs, dims, ns, ls):
    nw = len(_W_NAMES)
    x_ref, hist_in_ref, state_in_ref = refs[0:3]
    w = dict(zip(_W_NAMES, refs[3:3 + nw]))
    x1_ref, hist_ref, state_ref = refs[3 + nw:6 + nw]
    ubuf, q_s, k_s, v_s, lf_s, o_s, c_s = refs[6 + nw:]
    i = pl.program_id(0)
    dc = dims["dc"]
    hl = CONV_W - 1

    @pl.when(i == 0)
    def _():
        xb = x_ref[...].astype(BF16)
        ubuf[:, 0:8, :] = jnp.zeros((ns, 8, dc), F32)
        ubuf[:, HIST_PAD - hl:HIST_PAD, :] = hist_in_ref[0]

        def u_store(u):
            ubuf[:, HIST_PAD:HIST_PAD + ls, :] = u.reshape(ns, ls, dc)

        _project(xb, w, dims, u_store, q_s, k_s, v_s, lf_s)

    for s in range(SEQ_PER_STEP):
        seq = i * SEQ_PER_STEP + s
        rows = pl.ds(pl.multiple_of(seq * ls, ls), ls)

        def set_state(h, val, s=s):
            state_ref[s, h] = val

        _gla_chunk(q_s, k_s, v_s, lf_s, o_s, rows, ls, lambda h, s=s: state_in_ref[0, s, h], set_state, dims)
        win = ubuf[seq]
        c_s[rows, :] = _conv_block(win, w["convw"], ls)
        hist_ref[seq] = win[HIST_PAD + ls - hl:HIST_PAD + ls, :]

    @pl.when(i == ns // SEQ_PER_STEP - 1)
    def _():
        x = x_ref[...]
        x1_ref[...] = _tail(x, x.astype(BF16), c_s[...], w, dims, o_s)


def _mixer_weights(l, p, dims):
    dc, dk, dv, rank = dims["dc"], dims["dk"], dims["dv"], dims["rank"]
    off_lr = 2 * dc + 2 * dk + 2 * dv
    off_gates = off_lr + rank
    w_in, b_in = p["w_in"][l], p["b_in"][l]
    row = lambda v: v.reshape(1, -1).astype(F32)
    return (
        w_in[:, :off_lr].astype(BF16), row(b_in[:off_lr]),
        w_in[:, off_lr:off_gates].astype(BF16), row(b_in[off_lr:off_gates]),
        w_in[:, off_gates:].astype(BF16), row(b_in[off_gates:]),
        jnp.repeat(p["conv_w"][l].astype(F32), 8, axis=0),
        row(p["conv_b"][l]), row(p["conv_ln_g"][l]), row(p["conv_ln_b"][l]),
        p["w_conv_out"][l].astype(BF16), row(p["b_conv_out"][l]),
        p["w_gate_up"][l].astype(BF16), row(p["b_gate"][l]),
        row(p["gla_norm_g"][l]),
        p["w_gla_out"][l].astype(BF16), p["w_o"][l].astype(BF16),
        row(p["ln1_g"][l]), row(p["ln1_b"][l]),
    )


def _mixer_prompt(x, x1_tail, wts, dims, tl, bsz, seq, to_cast):
    d = x.shape[1]
    dc, dk, dv, hk, hv = dims["dc"], dims["dk"], dims["dv"], dims["hk"], dims["hv"]
    hl = CONV_W - 1
    nt = seq // tl
    n_prompt = bsz * nt
    n_tail = x1_tail.shape[0] // tl
    kern = functools.partial(_mixer_prompt_kernel, dims=dims, tl=tl, nt=nt, n_prompt=n_prompt,
                             n_cast=len(to_cast))
    seq_of = lambda i: jnp.minimum(i // nt, bsz - 1)
    c_in, c_out, c_shapes = _cast_plan(to_cast, n_prompt + n_tail) if to_cast else ([], [], [])
    return pl.pallas_call(
        kern,
        grid=(n_prompt + n_tail,),
        in_specs=[pl.BlockSpec((tl, d), lambda i: (jnp.minimum(i, n_prompt - 1), 0)),
                  pl.BlockSpec((tl, d), lambda i: (jnp.maximum(i - n_prompt, 0), 0))]
        + [_full_spec(a.shape) for a in wts] + c_in,
        out_specs=[
            pl.BlockSpec((tl, d), lambda i: (i, 0)),
            pl.BlockSpec((1, hl, dc), lambda i: (seq_of(i), 0, 0)),
            pl.BlockSpec((1, GLA_HEADS, hk, hv), lambda i: (seq_of(i), 0, 0, 0)),
        ] + c_out,
        out_shape=[
            jax.ShapeDtypeStruct(((n_prompt + n_tail) * tl, d), F32),
            jax.ShapeDtypeStruct((bsz, hl, dc), F32),
            jax.ShapeDtypeStruct((bsz, GLA_HEADS, hk, hv), F32),
        ] + c_shapes,
        scratch_shapes=[
            pltpu.VMEM((HIST_PAD + tl, dc), F32),
            pltpu.VMEM((tl, dk), BF16), pltpu.VMEM((tl, dk), BF16), pltpu.VMEM((tl, dk), BF16),
            pltpu.VMEM((tl, dv), BF16),
            pltpu.VMEM((tl, dv), F32), pltpu.VMEM((tl, 2 * d), F32),
            pltpu.VMEM((tl, dv), F32), pltpu.VMEM((tl, dc), F32),
            pltpu.VMEM((GLA_HEADS, hk, hv), F32), pltpu.VMEM((dk, 128), F32),
        ],
        compiler_params=pltpu.CompilerParams(
            dimension_semantics=("arbitrary",), vmem_limit_bytes=VMEM_LIMIT),
        name="mixer_prompt",
    )(x, x1_tail, *wts, *to_cast)


def _mixer_sample(x, in_blk, hist, state, layer, wts, dims, ns, ls):
    d = x.shape[1]
    dc, dk, dv, hk, hv = dims["dc"], dims["dk"], dims["dv"], dims["hk"], dims["hv"]
    hl = CONV_W - 1
    t = ns * ls
    kern = functools.partial(_mixer_sample_kernel, dims=dims, ns=ns, ls=ls)
    x1, hist_o, state_o = pl.pallas_call(
        kern,
        grid=(ns // SEQ_PER_STEP,),
        in_specs=[
            pl.BlockSpec((t, d), lambda i: (in_blk, 0)),
            pl.BlockSpec((1, ns, hl, dc), lambda i: (layer, 0, 0, 0)),
            pl.BlockSpec((1, SEQ_PER_STEP, GLA_HEADS, hk, hv), lambda i: (layer, i, 0, 0, 0)),
        ] + [_full_spec(a.shape) for a in wts],
        out_specs=[
            pl.BlockSpec((t, d), lambda i: (0, 0)),
            pl.BlockSpec((ns, hl, dc), lambda i: (0, 0, 0)),
            pl.BlockSpec((SEQ_PER_STEP, GLA_HEADS, hk, hv), lambda i: (i, 0, 0, 0)),
        ],
        out_shape=[
            jax.ShapeDtypeStruct((t, d), F32),
            jax.ShapeDtypeStruct((ns, hl, dc), F32),
            jax.ShapeDtypeStruct((ns, GLA_HEADS, hk, hv), F32),
        ],
        scratch_shapes=[
            pltpu.VMEM((ns, HIST_PAD + ls, dc), F32),
            pltpu.VMEM((t, dk), F32), pltpu.VMEM((t, dk), F32), pltpu.VMEM((t, dv), F32),
            pltpu.VMEM((t, dk), F32), pltpu.VMEM((t, dv), F32), pltpu.VMEM((t, dc), F32),
        ],
        compiler_params=pltpu.CompilerParams(
            dimension_semantics=("arbitrary",), vmem_limit_bytes=VMEM_LIMIT),
        name="mixer_sample",
    )(x, hist, state, *wts)
    return x1, hist_o, state_o


def _token_out(t, d, tm, split):
    if split is None:
        return [pl.BlockSpec((tm, d), lambda i: (i, 0))], [jax.ShapeDtypeStruct((t, d), F32)]
    tp, ts = split
    npt = tp // tm
    specs = [pl.BlockSpec((tm, d), lambda i: (jnp.minimum(i, npt - 1), 0)),
             pl.BlockSpec((tm, d), lambda i: (jnp.maximum(i - npt, 0), 0))]
    return specs, [jax.ShapeDtypeStruct((tp, d), F32), jax.ShapeDtypeStruct((ts, d), F32)]


def _token_store(o_refs, val, npt):
    if len(o_refs) == 1:
        o_refs[0][...] = val
        return
    i = pl.program_id(0)

    @pl.when(i < npt)
    def _():
        o_refs[0][...] = val

    @pl.when(i >= npt)
    def _():
        o_refs[1][...] = val


def _ffn_dense_kernel(x_ref, wg_ref, wu_ref, wd_ref, g_ref, b_ref, *rest, alpha, npt, n_cast):
    cast_in, o_refs, cast_out = rest[:n_cast], rest[n_cast:len(rest) - n_cast], rest[len(rest) - n_cast:]
    _cast_blocks(cast_in, cast_out)
    x = x_ref[...]
    xb = x.astype(BF16)
    h = _silu(_mm(xb, wg_ref[...])) * _mm(xb, wu_ref[...])
    f = _mm(h.astype(BF16), wd_ref[...])
    _token_store(o_refs, _layer_norm(alpha * x + f, g_ref[...], b_ref[...]), npt)


def _ffn_dense(x, wg, wu, wd, g, b, alpha, tm, split, to_cast):
    t, d = x.shape
    ops = (wg.astype(BF16), wu.astype(BF16), wd.astype(BF16), g.reshape(1, d), b.reshape(1, d))
    out_specs, out_shape = _token_out(t, d, tm, split)
    npt = None if split is None else split[0] // tm
    c_in, c_out, c_shapes = _cast_plan(to_cast, t // tm) if to_cast else ([], [], [])
    res = pl.pallas_call(
        functools.partial(_ffn_dense_kernel, alpha=alpha, npt=npt, n_cast=len(to_cast)),
        grid=(t // tm,),
        in_specs=[pl.BlockSpec((tm, d), lambda i: (i, 0))] + [_full_spec(a.shape) for a in ops] + c_in,
        out_specs=out_specs + c_out,
        out_shape=out_shape + c_shapes,
        compiler_params=pltpu.CompilerParams(
            dimension_semantics=("arbitrary",), vmem_limit_bytes=VMEM_LIMIT),
        name="ffn_dense",
    )(x, *ops, *to_cast)
    return res[:len(out_shape)], res[len(out_shape):]


def _cast_plan(arrays, n_steps):
    n_blocks = max(n for n in range(1, n_steps + 1)
                   if all(a.shape[0] % n == 0 and (a.shape[0] // n) % 16 == 0 for a in arrays))
    spec = lambda a: pl.BlockSpec((a.shape[0] // n_blocks, a.shape[1]), lambda i: (jnp.minimum(i, n_blocks - 1), 0))
    return ([spec(a) for a in arrays], [spec(a) for a in arrays],
            [jax.ShapeDtypeStruct(a.shape, BF16) for a in arrays])


def _cast_blocks(src_refs, dst_refs):
    for src, dst in zip(src_refs, dst_refs):
        dst[...] = src[...].astype(BF16)


def _router_kernel(x_ref, wr_ref, *rest, ne, tr, n_cast):
    cast_in, (route_ref, cnt_ref), cast_out = rest[:n_cast], rest[n_cast:n_cast + 2], rest[n_cast + 2:-1]
    carry = rest[-1]
    i = pl.program_id(0)
    _cast_blocks(cast_in, cast_out)

    @pl.when(i == 0)
    def _():
        carry[...] = jnp.zeros(carry.shape, F32)

    xh = x_ref[...]
    x_hi = xh.astype(BF16)
    x_lo = (xh - x_hi.astype(F32)).astype(BF16)
    wr = wr_ref[...]
    w_hi = wr.astype(BF16)
    w_lo = (wr - w_hi.astype(F32)).astype(BF16)
    nt = (((1,), (1,)), ((), ()))
    logits = (lax.dot_general(w_hi, x_hi, nt, preferred_element_type=F32)
              + lax.dot_general(w_hi, x_lo, nt, preferred_element_type=F32)
              + lax.dot_general(w_lo, x_hi, nt, preferred_element_type=F32))
    mx = jnp.max(logits, axis=0, keepdims=True)
    ex = jnp.exp(logits - mx)
    probs = ex / jnp.sum(ex, axis=0, keepdims=True)
    eid = lax.broadcasted_iota(jnp.int32, (ne, tr), 0)
    p1 = jnp.max(probs, axis=0, keepdims=True)
    i1 = jnp.min(jnp.where(probs == p1, eid, ne), axis=0, keepdims=True)
    rest = jnp.where(eid == i1, -1.0, probs)
    p2 = jnp.max(rest, axis=0, keepdims=True)
    i2 = jnp.min(jnp.where(rest == p2, eid, ne), axis=0, keepdims=True)
    den = p1 + p2
    oh1 = (eid == i1).astype(F32)
    oh2 = (eid == i2).astype(F32)
    oh = oh1 + oh2
    ri = lax.broadcasted_iota(jnp.int32, (tr, tr), 0)
    ci = lax.broadcasted_iota(jnp.int32, (tr, tr), 1)
    upper = (ri <= ci).astype(BF16)
    incl = _mm(oh.astype(BF16), upper)
    before = carry[:, 0:1] + incl - oh
    r1 = jnp.sum(oh1 * before, axis=0, keepdims=True)
    r2 = jnp.sum(oh2 * before, axis=0, keepdims=True)
    zero = jnp.zeros((1, tr), F32)
    route_ref[...] = jnp.concatenate(
        [i1.astype(F32), i2.astype(F32), p1 / den, p2 / den, r1, r2, zero, zero], axis=0)
    total = carry[:, 0:1] + incl[:, tr - 1:tr]
    carry[...] = jnp.broadcast_to(total, carry.shape)
    cnt_ref[...] = jnp.broadcast_to(total, cnt_ref.shape)


def _router(x, w_router, tr, to_cast):
    t, d = x.shape
    ne = w_router.shape[1]
    c_in, c_out, c_shapes = _cast_plan(to_cast, t // tr)
    res = pl.pallas_call(
        functools.partial(_router_kernel, ne=ne, tr=tr, n_cast=len(to_cast)),
        grid=(t // tr,),
        in_specs=[pl.BlockSpec((tr, d), lambda i: (i, 0)), _full_spec((ne, d))] + c_in,
        out_specs=[pl.BlockSpec((8, tr), lambda i: (0, i)), pl.BlockSpec((ne, 128), lambda i: (0, 0))] + c_out,
        out_shape=[jax.ShapeDtypeStruct((8, t), F32), jax.ShapeDtypeStruct((ne, 128), F32)] + c_shapes,
        scratch_shapes=[pltpu.VMEM((ne, 128), F32)],
        compiler_params=pltpu.CompilerParams(dimension_semantics=("arbitrary",), vmem_limit_bytes=VMEM_LIMIT),
        name="moe_router",
    )(x, w_router.T.astype(F32), *to_cast)
    return res[0], res[1], res[2:]


def _dispatch_kernel(pos_ref, grp_ref, x_ref, *rest, tm, te, ne, n_tiles, n_cast):
    cast_in, xs_ref, cast_out, sem = rest[:n_cast], rest[n_cast], rest[n_cast + 1:-1], rest[-1]
    i = pl.program_id(0)

    def row_copy(r, k):
        return pltpu.make_async_copy(x_ref.at[pl.ds(r, 1), :], xs_ref.at[pl.ds(pos_ref[0, 0, 2 * r + k], 1), :], sem)

    def start(g, carry):
        for u in range(ROW_DMA_UNROLL):
            row_copy(g * ROW_DMA_UNROLL + u, 0).start()
            row_copy(g * ROW_DMA_UNROLL + u, 1).start()
        return carry

    lax.fori_loop(0, tm // ROW_DMA_UNROLL, start, 0)
    _cast_blocks(cast_in, cast_out)
    for _ in range(TOP_K):
        pltpu.make_async_copy(x_ref, xs_ref.at[pl.ds(0, tm), :], sem).wait()

    @pl.when(i == pl.num_programs(0) - 1)
    def _():
        for e in range(ne):
            lo, hi = grp_ref[e] + grp_ref[ne + e], grp_ref[e] + grp_ref[2 * ne + e]

            def pad_copy(r):
                return pltpu.make_async_copy(x_ref.at[pl.ds(0, 1), :], xs_ref.at[pl.ds(r, 1), :], sem)

            lax.fori_loop(lo, hi, lambda r, c: (pad_copy(r).start(), c)[1], 0)
            lax.fori_loop(lo, hi, lambda r, c: (pad_copy(r).wait(), c)[1], 0)

        def tile_copy(j):
            return pltpu.make_async_copy(x_ref.at[pl.ds(0, te), :], xs_ref.at[pl.ds(j * te, te), :], sem)

        lax.fori_loop(grp_ref[3 * ne], n_tiles, lambda j, c: (tile_copy(j).start(), c)[1], 0)
        lax.fori_loop(grp_ref[3 * ne], n_tiles, lambda j, c: (tile_copy(j).wait(), c)[1], 0)


def _dispatch(x, pos, grp, n_tiles, tm, te, to_cast):
    t, d = x.shape
    ne = (grp.shape[0] - 1) // 3
    assert te <= tm
    c_in, c_out, c_shapes = _cast_plan(to_cast, t // tm)
    res = pl.pallas_call(
        functools.partial(_dispatch_kernel, tm=tm, te=te, ne=ne, n_tiles=n_tiles, n_cast=len(to_cast)),
        grid=(t // tm,),
        in_specs=[
            pl.BlockSpec((1, 1, 2 * tm), lambda i: (i, 0, 0), memory_space=pltpu.SMEM),
            pl.BlockSpec(memory_space=pltpu.SMEM),
            pl.BlockSpec((tm, d), lambda i: (i, 0)),
        ] + c_in,
        out_specs=[pl.BlockSpec(memory_space=pl.ANY)] + c_out,
        out_shape=[jax.ShapeDtypeStruct((n_tiles * te, d), F32)] + c_shapes,
        scratch_shapes=[pltpu.SemaphoreType.DMA],
        compiler_params=pltpu.CompilerParams(
            dimension_semantics=("arbitrary",), has_side_effects=True, vmem_limit_bytes=VMEM_LIMIT),
        name="moe_dispatch",
    )(pos.reshape(t // tm, 1, 2 * tm), grp, x, *to_cast)
    return res[0], res[1:]


def _experts_kernel(te_ref, nv_ref, xs_ref, wg_ref, wu_ref, wd_ref, ys_ref):
    i = pl.program_id(0)

    @pl.when(i < nv_ref[0])
    def _():
        xb = xs_ref[...].astype(BF16)
        h = _silu(_mm(xb, wg_ref[0])) * _mm(xb, wu_ref[0])
        ys_ref[...] = _mm(h.astype(BF16), wd_ref[0])

    @pl.when(i >= nv_ref[0])
    def _():
        ys_ref[...] = jnp.zeros(ys_ref.shape, F32)


def _experts(xs, tile_expert, n_valid, wg, wu, wd, tm):
    n_rows, d = xs.shape
    ne, _, ff = wg.shape
    grid_spec = pltpu.PrefetchScalarGridSpec(
        num_scalar_prefetch=2,
        grid=(n_rows // tm,),
        in_specs=[
            pl.BlockSpec((tm, d), lambda i, te, nv: (i, 0)),
            pl.BlockSpec((1, d, ff), lambda i, te, nv: (te[i], 0, 0)),
            pl.BlockSpec((1, d, ff), lambda i, te, nv: (te[i], 0, 0)),
            pl.BlockSpec((1, ff, d), lambda i, te, nv: (te[i], 0, 0)),
        ],
        out_specs=pl.BlockSpec((tm, d), lambda i, te, nv: (i, 0)),
    )
    return pl.pallas_call(
        _experts_kernel,
        grid_spec=grid_spec,
        out_shape=jax.ShapeDtypeStruct((n_rows, d), F32),
        compiler_params=pltpu.CompilerParams(
            dimension_semantics=("arbitrary",), vmem_limit_bytes=VMEM_LIMIT),
        name="moe_experts",
    )(tile_expert, n_valid, xs, wg, wu, wd)


def _combine_kernel(pos_ref, pos_next_ref, x_ref, rt_ref, ys_ref, g_ref, b_ref, *rest, tm, alpha, npt):
    o_refs, (buf, sem) = rest[:-2], rest[-2:]
    i = pl.program_id(0)
    n = pl.num_programs(0)
    slot = i % 2

    def gather(p_ref, s):
        def row_copy(r, k):
            return pltpu.make_async_copy(
                ys_ref.at[pl.ds(p_ref[0, 0, 2 * r + k], 1), :], buf.at[s, k, pl.ds(r, 1), :], sem.at[s])

        def start(g, carry):
            for u in range(ROW_DMA_UNROLL):
                row_copy(g * ROW_DMA_UNROLL + u, 0).start()
                row_copy(g * ROW_DMA_UNROLL + u, 1).start()
            return carry

        lax.fori_loop(0, tm // ROW_DMA_UNROLL, start, 0)

    @pl.when(i == 0)
    def _():
        gather(pos_ref, 0)

    @pl.when(i + 1 < n)
    def _():
        gather(pos_next_ref, 1 - slot)

    for k in range(TOP_K):
        pltpu.make_async_copy(ys_ref.at[pl.ds(0, tm), :], buf.at[slot, k], sem.at[slot]).wait()
    rt = rt_ref[...]
    f = rt[:, 2:3] * buf[slot, 0] + rt[:, 3:4] * buf[slot, 1]
    _token_store(o_refs, _layer_norm(alpha * x_ref[...] + f, g_ref[...], b_ref[...]), npt)


def _combine(x, pos, route_t, ys, g, b, alpha, tm, split):
    t, d = x.shape
    out_specs, out_shape = _token_out(t, d, tm, split)
    npt = None if split is None else split[0] // tm
    n_steps = t // tm
    pos3 = pos.reshape(n_steps, 1, 2 * tm)
    return pl.pallas_call(
        functools.partial(_combine_kernel, tm=tm, alpha=alpha, npt=npt),
        grid=(t // tm,),
        in_specs=[
            pl.BlockSpec((1, 1, 2 * tm), lambda i: (i, 0, 0), memory_space=pltpu.SMEM),
            pl.BlockSpec((1, 1, 2 * tm), lambda i: (jnp.minimum(i + 1, n_steps - 1), 0, 0), memory_space=pltpu.SMEM),
            pl.BlockSpec((tm, d), lambda i: (i, 0)),
            pl.BlockSpec((tm, 8), lambda i: (i, 0)),
            pl.BlockSpec(memory_space=pl.ANY),
            _full_spec((1, d)), _full_spec((1, d)),
        ],
        out_specs=out_specs,
        out_shape=out_shape,
        scratch_shapes=[pltpu.VMEM((2, TOP_K, tm, d), F32), pltpu.SemaphoreType.DMA((2,))],
        compiler_params=pltpu.CompilerParams(dimension_semantics=("arbitrary",), vmem_limit_bytes=VMEM_LIMIT),
        name="moe_combine",
    )(pos3, pos3, x, route_t, ys, g.reshape(1, d), b.reshape(1, d))


def _ffn_moe(x, w_router, wg, wu, wd, g, b, alpha, tr, tm, split, pre):
    t, d = x.shape
    ne = w_router.shape[1]
    ff = wg.shape[2]
    flat = {"wg": wg.reshape(ne * d, ff), "wu": wu.reshape(ne * d, ff), "wd": wd.reshape(ne * ff, d)}
    conv = dict(pre)
    r_keys = [k for k in ("wd",) if k not in conv]
    d_keys = [k for k in ("wg", "wu") if k not in conv]
    route, counts, r_out = _router(x, w_router, tr, [flat[k] for k in r_keys])
    conv.update(zip(r_keys, r_out))
    cnt = counts[:, 0].astype(jnp.int32)
    gsz = ((cnt + tm - 1) // tm) * tm
    ends = jnp.cumsum(gsz)
    offs = ends - gsz
    n_tiles = (TOP_K * t) // tm + ne
    tile_start = jnp.arange(n_tiles, dtype=jnp.int32) * tm
    tile_e = jnp.sum((tile_start[:, None] >= ends[None, :]).astype(jnp.int32), axis=1)
    n_valid = (ends[ne - 1] // tm).astype(jnp.int32).reshape(1)
    last_e = jnp.sum((ends[ne - 1] - 1 >= ends).astype(jnp.int32))
    tile_e = jnp.minimum(tile_e, last_e).astype(jnp.int32)
    i12 = route[0:2].astype(jnp.int32)
    base = sum(jnp.where(i12 == e, offs[e], 0) for e in range(ne))
    pos = (base + route[4:6].astype(jnp.int32)).T.reshape(-1)
    grp = jnp.concatenate([offs, cnt, gsz, n_valid]).astype(jnp.int32)
    xs, d_out = _dispatch(x, pos, grp, n_tiles, tr, tm, [flat[k] for k in d_keys])
    conv.update(zip(d_keys, d_out))
    ys = _experts(xs, tile_e, n_valid, conv["wg"].reshape(ne, d, ff), conv["wu"].reshape(ne, d, ff),
                  conv["wd"].reshape(ne, ff, d), tm)
    return _combine(x, pos, route.T, ys, g, b, alpha, tr, split)


def kernel(x_prompt, x_sample, cache_conv, state_gla, w_in, b_in, conv_w, conv_b, conv_ln_g, conv_ln_b, w_conv_out, b_conv_out, w_gate_up, b_gate, gla_norm_g, w_gla_out, w_o, ln1_g, ln1_b, ln2_g, ln2_b, ff_w_gate, ff_w_up, ff_w_down, w_router, moe_w_gate, moe_w_up, moe_w_down):
    p = dict(w_in=w_in, b_in=b_in, conv_w=conv_w, conv_b=conv_b, conv_ln_g=conv_ln_g, conv_ln_b=conv_ln_b,
             w_conv_out=w_conv_out, b_conv_out=b_conv_out, w_gate_up=w_gate_up, b_gate=b_gate,
             gla_norm_g=gla_norm_g, w_gla_out=w_gla_out, w_o=w_o, ln1_g=ln1_g, ln1_b=ln1_b)
    depth = w_in.shape[0]
    bsz, seq, d = x_prompt.shape
    ns, ls, _ = x_sample.shape
    dc = conv_w.shape[-1]
    rank, dk = w_gate_up.shape[1], w_gate_up.shape[2]
    dv = w_gla_out.shape[1]
    dims = dict(d=d, dc=dc, dk=dk, dv=dv, rank=rank, hk=dk // GLA_HEADS, hv=dv // GLA_HEADS,
                alpha=(2.0 * depth) ** 0.25)
    alpha = dims["alpha"]
    tl = min(512, seq)
    tp = bsz * seq
    ts = ns * ls
    tm = min(512, ts)
    assert seq % tl == 0 and tp % ts == 0 and ts % tm == 0 and ts % tl == 0 and ns % SEQ_PER_STEP == 0

    x_p, x_s, s_blk = x_prompt.reshape(tp, d), x_sample.reshape(ts, d), 0
    hist_p, state_p, hist_s, state_s = [], [], [], []
    for l in range(depth):
        wts = _mixer_weights(l, p, dims)
        nxt = {}
        if l % 2 == 0 and l + 1 < depth:
            nxt = {k: m[l // 2].reshape(-1, m.shape[-1])
                   for k, m in (("wg", moe_w_gate), ("wu", moe_w_up), ("wd", moe_w_down))}
        x1s, hs, ss = _mixer_sample(x_s, s_blk, cache_conv, state_gla, l, wts, dims, ns, ls)
        x1, hp, sp, *done_mix = _mixer_prompt(x_p, x1s, wts, dims, tl, bsz, seq, [nxt["wu"]] if nxt else [])
        hist_p.append(hp), state_p.append(sp), hist_s.append(hs), state_s.append(ss)
        split = (tp, ts) if l == depth - 1 else None
        if l % 2 == 0:
            x2, done = _ffn_dense(x1, ff_w_gate[l // 2], ff_w_up[l // 2], ff_w_down[l // 2], ln2_g[l], ln2_b[l],
                                  alpha, tm, split, [nxt["wg"], nxt["wd"]] if nxt else [])
            pre_cast = dict(zip(("wg", "wd", "wu"), list(done) + done_mix)) if nxt else {}
        else:
            x2 = _ffn_moe(x1, w_router[l // 2], moe_w_gate[l // 2], moe_w_up[l // 2], moe_w_down[l // 2],
                          ln2_g[l], ln2_b[l], alpha, tm, min(EXPERT_TILE, tm), split, pre_cast)
        if split is None:
            x_p, x_s, s_blk = x2[0], x2[0], tp // ts
    y_p, y_s = x2
    return (y_p.reshape(bsz, seq, d), y_s.reshape(ns, ls, d), jnp.stack(hist_p),
            jnp.stack(state_p).astype(state_gla.dtype), jnp.stack(hist_s), jnp.stack(state_s).astype(state_gla.dtype))
```

```python
import functools

import jax
import jax.numpy as jnp
from jax import lax
from jax.experimental import pallas as pl
from jax.experimental.pallas import tpu as pltpu

CHUNK = 64
CONV_W = 31
GLA_HEADS = 4
GATE_TAU = 16.0
LN_EPS = 1e-5
RMS_EPS = 1e-6
TOP_K = 2

HIST_PAD = 32
CONV_ROWS = 32
ROW_DMA_UNROLL = 8
SEQ_PER_STEP = 4
EXPERT_TILE = 512
VMEM_LIMIT = 56 * 1024 * 1024

BF16 = jnp.bfloat16
F32 = jnp.float32


def _mm(a, b):
    return jnp.dot(a, b, preferred_element_type=F32)


def _sigmoid(x):
    return 0.5 * jnp.tanh(0.5 * x) + 0.5


def _silu(x):
    return x * _sigmoid(x)


def _log_sigmoid(z):
    return -(jnp.maximum(-z, 0.0) + jnp.log(1.0 + jnp.exp(-jnp.abs(z))))


def _layer_norm(x, g, b):
    mu = jnp.mean(x, axis=-1, keepdims=True)
    xc = x - mu
    var = jnp.mean(xc * xc, axis=-1, keepdims=True)
    return xc * lax.rsqrt(var + LN_EPS) * g + b


def _split3(x):
    hi = x.astype(BF16)
    r1 = x - hi.astype(F32)
    mid = r1.astype(BF16)
    lo = (r1 - mid.astype(F32)).astype(BF16)
    return hi, mid, lo


def _full_spec(shape):
    zeros = (0,) * len(shape)
    return pl.BlockSpec(shape, lambda *_: zeros, pipeline_mode=pl.Buffered(1))


def _project(xb, w, dims, u_store, q_s, k_s, v_s, lf_s):
    dc, dk, dv = dims["dc"], dims["dk"], dims["dv"]
    off_q, off_k, off_v = 2 * dc, 2 * dc + dk, 2 * dc + 2 * dk
    off_g = off_v + dv
    glu = _mm(xb, w["wmain"][:, 0:off_q]) + w["bmain"][:, 0:off_q]
    u_store(glu[:, 0:dc] * _sigmoid(glu[:, dc:off_q]))
    q_s[...] = (_mm(xb, w["wmain"][:, off_q:off_k]) + w["bmain"][:, off_q:off_k]) * (dims["hk"] ** -0.5)
    k_s[...] = _mm(xb, w["wmain"][:, off_k:off_v]) + w["bmain"][:, off_k:off_v]
    v_s[...] = _mm(xb, w["wmain"][:, off_v:off_g]) + w["bmain"][:, off_v:off_g]
    lr = _mm(xb, w["wlr"][...]) + w["blr"][...]
    z = _mm(lr.astype(BF16), w["wgu"][...]) + w["bgu"][...]
    lf_s[...] = _log_sigmoid(z) * (1.0 / GATE_TAU)


def _gla_chunk(q_s, k_s, v_s, lf_s, o_s, rows, c, get_state, set_state, dims):
    hk, hv, dk = dims["hk"], dims["hv"], dims["dk"]
    lf = lf_s[rows, :]
    ri = lax.broadcasted_iota(jnp.int32, (c, c), 0)
    ci = lax.broadcasted_iota(jnp.int32, (c, c), 1)
    causal = ri >= ci
    tri = causal.astype(BF16)
    hi, mid, lo = _split3(lf)
    b = _mm(tri, hi) + _mm(tri, mid) + _mm(tri, lo)
    b_last = b[c - 1:c, :]
    q_t = q_s[rows, :] * jnp.exp(b)
    kk = k_s[rows, :]
    k_t = kk * jnp.exp(-b)
    k_e = kk * jnp.exp(b_last - b)
    d_t = jnp.transpose(jnp.broadcast_to(jnp.exp(b_last), (128, dk)))
    vv = v_s[rows, :]
    for h in range(GLA_HEADS):
        ks = slice(h * hk, (h + 1) * hk)
        vs = slice(h * hv, (h + 1) * hv)
        qh = q_t[:, ks].astype(BF16)
        kh = k_t[:, ks].astype(BF16)
        keh = k_e[:, ks].astype(BF16)
        vh = vv[:, vs].astype(BF16)
        att = lax.dot_general(qh, kh, (((1,), (1,)), ((), ())), preferred_element_type=F32)
        att = jnp.where(causal, att, 0.0)
        s_old = get_state(h)
        o_s[rows, vs] = _mm(qh, s_old.astype(BF16)) + _mm(att.astype(BF16), vh)
        upd = lax.dot_general(keh, vh, (((0,), (0,)), ((), ())), preferred_element_type=F32)
        dcol = d_t[ks, :]
        decay = jnp.concatenate([dcol] * (hv // 128), axis=1)
        set_state(h, decay * s_old + upd)


def _conv_block(win, w8_ref, n):
    wn, ch = win.shape
    base = HIST_PAD - (CONV_W - 1)
    acc = None
    for b in range(8):
        wb = win if b == 0 else pltpu.roll(win, wn - b, axis=0)
        for a in range((base + CONV_W + 7) // 8):
            j = 8 * a + b - base
            if 0 <= j < CONV_W:
                term = w8_ref[8 * j:8 * j + 8, :][None] * wb[8 * a:8 * a + n, :].reshape(n // 8, 8, ch)
                acc = term if acc is None else acc + term
    return acc.reshape(n, ch)


def _tail(x, xb, c, w, dims, o_s):
    d, dc, dk, dv, hv = dims["d"], dims["dc"], dims["dk"], dims["dv"], dims["hv"]
    off_g = 2 * dc + 2 * dk + dv
    off_lr = off_g + dv
    c = _silu(_layer_norm(c + w["convb"][...], w["clng"][...], w["clnb"][...]))
    y_a = _mm(c.astype(BF16), w["wco"][...]) + w["bco"][...]
    g_out = _mm(xb, w["wmain"][:, off_g:off_lr]) + w["bmain"][:, off_g:off_lr]
    heads = []
    for h in range(GLA_HEADS):
        oh = o_s[:, h * hv:(h + 1) * hv]
        ms = jnp.mean(oh * oh, axis=-1, keepdims=True)
        heads.append(oh * lax.rsqrt(ms + RMS_EPS) * w["gng"][...])
    o = jnp.concatenate(heads, axis=1) * _silu(g_out)
    y_b = _mm(o.astype(BF16), w["wgo"][...])
    gates = _sigmoid(_mm(xb, w["wgates"][...]) + w["bgates"][...])
    merged = gates[:, 0:d] * y_a + gates[:, d:2 * d] * y_b
    m = _mm(merged.astype(BF16), w["wo"][...])
    return _layer_norm(dims["alpha"] * x + m, w["ln1g"][...], w["ln1b"][...])


_W_NAMES = ("wmain", "bmain", "wlr", "blr", "wgates", "bgates", "convw", "convb", "clng", "clnb",
            "wco", "bco", "wgu", "bgu", "gng", "wgo", "wo", "ln1g", "ln1b")


def _mixer_prompt_kernel(*refs, dims, tl, nt, n_prompt, n_cast):
    nw = len(_W_NAMES)
    x_ref, tail_ref = refs[0:2]
    w = dict(zip(_W_NAMES, refs[2:2 + nw]))
    cast_in = refs[2 + nw:2 + nw + n_cast]
    x1_ref, hist_ref, state_ref = refs[2 + nw + n_cast:5 + nw + n_cast]
    cast_out = refs[5 + nw + n_cast:5 + nw + 2 * n_cast]
    scratch = refs[5 + nw + 2 * n_cast:]
    i = pl.program_id(0)
    _cast_blocks(cast_in, cast_out)

    @pl.when(i < n_prompt)
    def _():
        _mixer_prompt_tile(i % nt, x_ref, w, x1_ref, hist_ref, state_ref, scratch, dims, tl, nt)

    @pl.when(i >= n_prompt)
    def _():
        x1_ref[...] = tail_ref[...]


def _mixer_prompt_tile(j, x_ref, w, x1_ref, hist_ref, state_ref, scratch, dims, tl, nt):
    ubuf, qt_s, kt_s, ke_s, vb_s, g_s, gt_s, o_s, c_s, s_s, dt_s = scratch
    d, dc, dk, dv, hk, hv = dims["d"], dims["dc"], dims["dk"], dims["dv"], dims["hk"], dims["hv"]
    off_q, off_k, off_v = 2 * dc, 2 * dc + dk, 2 * dc + 2 * dk
    off_g = off_v + dv
    off_lr = off_g + dv
    nch = tl // CHUNK
    rep = 128 // nch
    wm, bm = w["wmain"], w["bmain"]

    @pl.when(j == 0)
    def _():
        ubuf[0:HIST_PAD, :] = jnp.zeros((HIST_PAD, dc), F32)
        s_s[...] = jnp.zeros(s_s.shape, F32)

    x = x_ref[...]
    xb = x.astype(BF16)

    glu = _mm(xb, wm[:, 0:off_q]) + bm[:, 0:off_q]
    ubuf[HIST_PAD:HIST_PAD + tl, :] = glu[:, 0:dc] * _sigmoid(glu[:, dc:off_q])
    lr = _mm(xb, w["wlr"][...]) + w["blr"][...]
    z = _mm(lr.astype(BF16), w["wgu"][...]) + w["bgu"][...]
    lf = _log_sigmoid(z) * (1.0 / GATE_TAU)
    ri = lax.broadcasted_iota(jnp.int32, (CHUNK, CHUNK), 0)
    ci = lax.broadcasted_iota(jnp.int32, (CHUNK, CHUNK), 1)
    causal = ri >= ci
    tri = causal.astype(BF16)
    parts = _split3(lf)
    b = jnp.concatenate(
        [sum(_mm(tri, p[c * CHUNK:(c + 1) * CHUNK, :]) for p in parts) for c in range(nch)], axis=0)
    b3 = b.reshape(nch, CHUNK, dk)
    b_last = b3[:, CHUNK - 1:CHUNK, :]
    q = (_mm(xb, wm[:, off_q:off_k]) + bm[:, off_q:off_k]) * (hk ** -0.5)
    qt_s[...] = (q * jnp.exp(b)).astype(BF16)
    kk = _mm(xb, wm[:, off_k:off_v]) + bm[:, off_k:off_v]
    kt_s[...] = (kk * jnp.exp(-b)).astype(BF16)
    ke_s[...] = (kk.reshape(nch, CHUNK, dk) * jnp.exp(b_last - b3)).reshape(tl, dk).astype(BF16)
    vb_s[...] = (_mm(xb, wm[:, off_v:off_g]) + bm[:, off_v:off_g]).astype(BF16)
    d_rows = jnp.broadcast_to(jnp.exp(b_last), (nch, rep, dk)).reshape(nch * rep, dk)
    if nch * rep < 128:
        d_rows = jnp.concatenate([d_rows, jnp.zeros((128 - nch * rep, dk), F32)], axis=0)
    dt_s[...] = jnp.transpose(d_rows)
    g_s[...] = _silu(_mm(xb, wm[:, off_g:off_lr]) + bm[:, off_g:off_lr])

    states = [s_s[h] for h in range(GLA_HEADS)]
    conv_per = tl // CONV_ROWS // nch
    gw = 2 * d // nch
    for c in range(nch):
        rows = slice(c * CHUNK, (c + 1) * CHUNK)
        for h in range(GLA_HEADS):
            ks = slice(h * hk, (h + 1) * hk)
            vs = slice(h * hv, (h + 1) * hv)
            qh, kh, keh, vh = qt_s[rows, ks], kt_s[rows, ks], ke_s[rows, ks], vb_s[rows, vs]
            att = lax.dot_general(qh, kh, (((1,), (1,)), ((), ())), preferred_element_type=F32)
            att = jnp.where(causal, att, 0.0)
            s_old = states[h]
            o_s[rows, vs] = _mm(qh, s_old.astype(BF16)) + _mm(att.astype(BF16), vh)
            upd = lax.dot_general(keh, vh, (((0,), (0,)), ((), ())), preferred_element_type=F32)
            decay = jnp.broadcast_to(dt_s[ks, c * rep:c * rep + 1], (hk, hv))
            states[h] = decay * s_old + upd
        for bi in range(c * conv_per, (c + 1) * conv_per):
            r0 = bi * CONV_ROWS
            c_s[r0:r0 + CONV_ROWS, :] = _conv_block(
                ubuf[r0:r0 + CONV_ROWS + HIST_PAD, :], w["convw"], CONV_ROWS)
        cols = slice(c * gw, (c + 1) * gw)
        gt_s[:, cols] = _sigmoid(_mm(xb, w["wgates"][:, cols]) + w["bgates"][:, cols])
    for h in range(GLA_HEADS):
        s_s[h] = states[h]

    cv = _silu(_layer_norm(c_s[...] + w["convb"][...], w["clng"][...], w["clnb"][...]))
    y_a = _mm(cv.astype(BF16), w["wco"][...]) + w["bco"][...]
    heads = []
    for h in range(GLA_HEADS):
        oh = o_s[:, h * hv:(h + 1) * hv]
        ms = jnp.mean(oh * oh, axis=-1, keepdims=True)
        heads.append(oh * lax.rsqrt(ms + RMS_EPS) * w["gng"][...])
    o = jnp.concatenate(heads, axis=1) * g_s[...]
    y_b = _mm(o.astype(BF16), w["wgo"][...])
    merged = gt_s[:, 0:d] * y_a + gt_s[:, d:2 * d] * y_b
    m = _mm(merged.astype(BF16), w["wo"][...])
    x1_ref[...] = _layer_norm(dims["alpha"] * x + m, w["ln1g"][...], w["ln1b"][...])

    @pl.when(j == nt - 1)
    def _():
        hist_ref[0] = ubuf[tl + HIST_PAD - (CONV_W - 1):tl + HIST_PAD, :]
        state_ref[0] = s_s[...]

    ubuf[0:HIST_PAD, :] = ubuf[tl:tl + HIST_PAD, :]


def _mixer_sample_kernel(*refs, dims, ns, ls):
    nw = len(_W_NAMES)
    x_ref, hist_in_ref, state_in_ref = refs[0:3]
    w = dict(zip(_W_NAMES, refs[3:3 + nw]))
    x1_ref, hist_ref, state_ref = refs[3 + nw:6 + nw]
    ubuf, q_s, k_s, v_s, lf_s, o_s, c_s = refs[6 + nw:]
    i = pl.program_id(0)
    dc = dims["dc"]
    hl = CONV_W - 1

    @pl.when(i == 0)
    def _():
        xb = x_ref[...].astype(BF16)
        ubuf[:, 0:8, :] = jnp.zeros((ns, 8, dc), F32)
        ubuf[:, HIST_PAD - hl:HIST_PAD, :] = hist_in_ref[0]

        def u_store(u):
            ubuf[:, HIST_PAD:HIST_PAD + ls, :] = u.reshape(ns, ls, dc)

        _project(xb, w, dims, u_store, q_s, k_s, v_s, lf_s)

    for s in range(SEQ_PER_STEP):
        seq = i * SEQ_PER_STEP + s
        rows = pl.ds(pl.multiple_of(seq * ls, ls), ls)

        def set_state(h, val, s=s):
            state_ref[s, h] = val

        _gla_chunk(q_s, k_s, v_s, lf_s, o_s, rows, ls, lambda h, s=s: state_in_ref[0, s, h], set_state, dims)
        win = ubuf[seq]
        c_s[rows, :] = _conv_block(win, w["convw"], ls)
        hist_ref[seq] = win[HIST_PAD + ls - hl:HIST_PAD + ls, :]

    @pl.when(i == ns // SEQ_PER_STEP - 1)
    def _():
        x = x_ref[...]
        x1_ref[...] = _tail(x, x.astype(BF16), c_s[...], w, dims, o_s)


def _mixer_weights(l, p, dims):
    dc, dk, dv, rank = dims["dc"], dims["dk"], dims["dv"], dims["rank"]
    off_lr = 2 * dc + 2 * dk + 2 * dv
    off_gates = off_lr + rank
    w_in, b_in = p["w_in"][l], p["b_in"][l]
    row = lambda v: v.reshape(1, -1).astype(F32)
    return (
        w_in[:, :off_lr].astype(BF16), row(b_in[:off_lr]),
        w_in[:, off_lr:off_gates].astype(BF16), row(b_in[off_lr:off_gates]),
        w_in[:, off_gates:].astype(BF16), row(b_in[off_gates:]),
        jnp.repeat(p["conv_w"][l].astype(F32), 8, axis=0),
        row(p["conv_b"][l]), row(p["conv_ln_g"][l]), row(p["conv_ln_b"][l]),
        p["w_conv_out"][l].astype(BF16), row(p["b_conv_out"][l]),
        p["w_gate_up"][l].astype(BF16), row(p["b_gate"][l]),
        row(p["gla_norm_g"][l]),
        p["w_gla_out"][l].astype(BF16), p["w_o"][l].astype(BF16),
        row(p["ln1_g"][l]), row(p["ln1_b"][l]),
    )


def _mixer_prompt(x, x1_tail, wts, dims, tl, bsz, seq, to_cast):
    d = x.shape[1]
    dc, dk, dv, hk, hv = dims["dc"], dims["dk"], dims["dv"], dims["hk"], dims["hv"]
    hl = CONV_W - 1
    nt = seq // tl
    n_prompt = bsz * nt
    n_tail = x1_tail.shape[0] // tl
    kern = functools.partial(_mixer_prompt_kernel, dims=dims, tl=tl, nt=nt, n_prompt=n_prompt,
                             n_cast=len(to_cast))
    seq_of = lambda i: jnp.minimum(i // nt, bsz - 1)
    c_in, c_out, c_shapes = _cast_plan(to_cast, n_prompt + n_tail) if to_cast else ([], [], [])
    return pl.pallas_call(
        kern,
        grid=(n_prompt + n_tail,),
        in_specs=[pl.BlockSpec((tl, d), lambda i: (jnp.minimum(i, n_prompt - 1), 0)),
                  pl.BlockSpec((tl, d), lambda i: (jnp.maximum(i - n_prompt, 0), 0))]
        + [_full_spec(a.shape) for a in wts] + c_in,
        out_specs=[
            pl.BlockSpec((tl, d), lambda i: (i, 0)),
            pl.BlockSpec((1, hl, dc), lambda i: (seq_of(i), 0, 0)),
            pl.BlockSpec((1, GLA_HEADS, hk, hv), lambda i: (seq_of(i), 0, 0, 0)),
        ] + c_out,
        out_shape=[
            jax.ShapeDtypeStruct(((n_prompt + n_tail) * tl, d), F32),
            jax.ShapeDtypeStruct((bsz, hl, dc), F32),
            jax.ShapeDtypeStruct((bsz, GLA_HEADS, hk, hv), F32),
        ] + c_shapes,
        scratch_shapes=[
            pltpu.VMEM((HIST_PAD + tl, dc), F32),
            pltpu.VMEM((tl, dk), BF16), pltpu.VMEM((tl, dk), BF16), pltpu.VMEM((tl, dk), BF16),
            pltpu.VMEM((tl, dv), BF16),
            pltpu.VMEM((tl, dv), F32), pltpu.VMEM((tl, 2 * d), F32),
            pltpu.VMEM((tl, dv), F32), pltpu.VMEM((tl, dc), F32),
            pltpu.VMEM((GLA_HEADS, hk, hv), F32), pltpu.VMEM((dk, 128), F32),
        ],
        compiler_params=pltpu.CompilerParams(
            dimension_semantics=("arbitrary",), vmem_limit_bytes=VMEM_LIMIT),
        name="mixer_prompt",
    )(x, x1_tail, *wts, *to_cast)


def _mixer_sample(x, in_blk, hist, state, layer, wts, dims, ns, ls):
    d = x.shape[1]
    dc, dk, dv, hk, hv = dims["dc"], dims["dk"], dims["dv"], dims["hk"], dims["hv"]
    hl = CONV_W - 1
    t = ns * ls
    kern = functools.partial(_mixer_sample_kernel, dims=dims, ns=ns, ls=ls)
    x1, hist_o, state_o = pl.pallas_call(
        kern,
        grid=(ns // SEQ_PER_STEP,),
        in_specs=[
            pl.BlockSpec((t, d), lambda i: (in_blk, 0)),
            pl.BlockSpec((1, ns, hl, dc), lambda i: (layer, 0, 0, 0)),
            pl.BlockSpec((1, SEQ_PER_STEP, GLA_HEADS, hk, hv), lambda i: (layer, i, 0, 0, 0)),
        ] + [_full_spec(a.shape) for a in wts],
        out_specs=[
            pl.BlockSpec((t, d), lambda i: (0, 0)),
            pl.BlockSpec((ns, hl, dc), lambda i: (0, 0, 0)),
            pl.BlockSpec((SEQ_PER_STEP, GLA_HEADS, hk, hv), lambda i: (i, 0, 0, 0)),
        ],
        out_shape=[
            jax.ShapeDtypeStruct((t, d), F32),
            jax.ShapeDtypeStruct((ns, hl, dc), F32),
            jax.ShapeDtypeStruct((ns, GLA_HEADS, hk, hv), F32),
        ],
        scratch_shapes=[
            pltpu.VMEM((ns, HIST_PAD + ls, dc), F32),
            pltpu.VMEM((t, dk), F32), pltpu.VMEM((t, dk), F32), pltpu.VMEM((t, dv), F32),
            pltpu.VMEM((t, dk), F32), pltpu.VMEM((t, dv), F32), pltpu.VMEM((t, dc), F32),
        ],
        compiler_params=pltpu.CompilerParams(
            dimension_semantics=("arbitrary",), vmem_limit_bytes=VMEM_LIMIT),
        name="mixer_sample",
    )(x, hist, state, *wts)
    return x1, hist_o, state_o


def _token_out(t, d, tm, split):
    if split is None:
        return [pl.BlockSpec((tm, d), lambda i: (i, 0))], [jax.ShapeDtypeStruct((t, d), F32)]
    tp, ts = split
    npt = tp // tm
    specs = [pl.BlockSpec((tm, d), lambda i: (jnp.minimum(i, npt - 1), 0)),
             pl.BlockSpec((tm, d), lambda i: (jnp.maximum(i - npt, 0), 0))]
    return specs, [jax.ShapeDtypeStruct((tp, d), F32), jax.ShapeDtypeStruct((ts, d), F32)]


def _token_store(o_refs, val, npt):
    if len(o_refs) == 1:
        o_refs[0][...] = val
        return
    i = pl.program_id(0)

    @pl.when(i < npt)
    def _():
        o_refs[0][...] = val

    @pl.when(i >= npt)
    def _():
        o_refs[1][...] = val


def _ffn_dense_kernel(x_ref, wg_ref, wu_ref, wd_ref, g_ref, b_ref, *rest, alpha, npt, n_cast):
    cast_in, o_refs, cast_out = rest[:n_cast], rest[n_cast:len(rest) - n_cast], rest[len(rest) - n_cast:]
    _cast_blocks(cast_in, cast_out)
    x = x_ref[...]
    xb = x.astype(BF16)
    h = _silu(_mm(xb, wg_ref[...])) * _mm(xb, wu_ref[...])
    f = _mm(h.astype(BF16), wd_ref[...])
    _token_store(o_refs, _layer_norm(alpha * x + f, g_ref[...], b_ref[...]), npt)


def _ffn_dense(x, wg, wu, wd, g, b, alpha, tm, split, to_cast):
    t, d = x.shape
    ops = (wg.astype(BF16), wu.astype(BF16), wd.astype(BF16), g.reshape(1, d), b.reshape(1, d))
    out_specs, out_shape = _token_out(t, d, tm, split)
    npt = None if split is None else split[0] // tm
    c_in, c_out, c_shapes = _cast_plan(to_cast, t // tm) if to_cast else ([], [], [])
    res = pl.pallas_call(
        functools.partial(_ffn_dense_kernel, alpha=alpha, npt=npt, n_cast=len(to_cast)),
        grid=(t // tm,),
        in_specs=[pl.BlockSpec((tm, d), lambda i: (i, 0))] + [_full_spec(a.shape) for a in ops] + c_in,
        out_specs=out_specs + c_out,
        out_shape=out_shape + c_shapes,
        compiler_params=pltpu.CompilerParams(
            dimension_semantics=("arbitrary",), vmem_limit_bytes=VMEM_LIMIT),
        name="ffn_dense",
    )(x, *ops, *to_cast)
    return res[:len(out_shape)], res[len(out_shape):]


def _cast_plan(arrays, n_steps):
    n_blocks = max(n for n in range(1, n_steps + 1)
                   if all(a.shape[0] % n == 0 and (a.shape[0] // n) % 16 == 0 for a in arrays))
    spec = lambda a: pl.BlockSpec((a.shape[0] // n_blocks, a.shape[1]), lambda i: (jnp.minimum(i, n_blocks - 1), 0))
    return ([spec(a) for a in arrays], [spec(a) for a in arrays],
            [jax.ShapeDtypeStruct(a.shape, BF16) for a in arrays])


def _cast_blocks(src_refs, dst_refs):
    for src, dst in zip(src_refs, dst_refs):
        dst[...] = src[...].astype(BF16)


def _router_kernel(x_ref, wr_ref, *rest, ne, tr, n_cast):
    cast_in, (route_ref, cnt_ref), cast_out = rest[:n_cast], rest[n_cast:n_cast + 2], rest[n_cast + 2:-1]
    carry = rest[-1]
    i = pl.program_id(0)
    _cast_blocks(cast_in, cast_out)

    @pl.when(i == 0)
    def _():
        carry[...] = jnp.zeros(carry.shape, F32)

    xh = x_ref[...]
    x_hi = xh.astype(BF16)
    x_lo = (xh - x_hi.astype(F32)).astype(BF16)
    wr = wr_ref[...]
    w_hi = wr.astype(BF16)
    w_lo = (wr - w_hi.astype(F32)).astype(BF16)
    nt = (((1,), (1,)), ((), ()))
    logits = (lax.dot_general(w_hi, x_hi, nt, preferred_element_type=F32)
              + lax.dot_general(w_hi, x_lo, nt, preferred_element_type=F32)
              + lax.dot_general(w_lo, x_hi, nt, preferred_element_type=F32))
    mx = jnp.max(logits, axis=0, keepdims=True)
    ex = jnp.exp(logits - mx)
    probs = ex / jnp.sum(ex, axis=0, keepdims=True)
    eid = lax.broadcasted_iota(jnp.int32, (ne, tr), 0)
    p1 = jnp.max(probs, axis=0, keepdims=True)
    i1 = jnp.min(jnp.where(probs == p1, eid, ne), axis=0, keepdims=True)
    rest = jnp.where(eid == i1, -1.0, probs)
    p2 = jnp.max(rest, axis=0, keepdims=True)
    i2 = jnp.min(jnp.where(rest == p2, eid, ne), axis=0, keepdims=True)
    den = p1 + p2
    oh1 = (eid == i1).astype(F32)
    oh2 = (eid == i2).astype(F32)
    oh = oh1 + oh2
    ri = lax.broadcasted_iota(jnp.int32, (tr, tr), 0)
    ci = lax.broadcasted_iota(jnp.int32, (tr, tr), 1)
    upper = (ri <= ci).astype(BF16)
    incl = _mm(oh.astype(BF16), upper)
    before = carry[:, 0:1] + incl - oh
    r1 = jnp.sum(oh1 * before, axis=0, keepdims=True)
    r2 = jnp.sum(oh2 * before, axis=0, keepdims=True)
    zero = jnp.zeros((1, tr), F32)
    route_ref[...] = jnp.concatenate(
        [i1.astype(F32), i2.astype(F32), p1 / den, p2 / den, r1, r2, zero, zero], axis=0)
    total = carry[:, 0:1] + incl[:, tr - 1:tr]
    carry[...] = jnp.broadcast_to(total, carry.shape)
    cnt_ref[...] = jnp.broadcast_to(total, cnt_ref.shape)


def _router(x, w_router, tr, to_cast):
    t, d = x.shape
    ne = w_router.shape[1]
    c_in, c_out, c_shapes = _cast_plan(to_cast, t // tr)
    res = pl.pallas_call(
        functools.partial(_router_kernel, ne=ne, tr=tr, n_cast=len(to_cast)),
        grid=(t // tr,),
        in_specs=[pl.BlockSpec((tr, d), lambda i: (i, 0)), _full_spec((ne, d))] + c_in,
        out_specs=[pl.BlockSpec((8, tr), lambda i: (0, i)), pl.BlockSpec((ne, 128), lambda i: (0, 0))] + c_out,
        out_shape=[jax.ShapeDtypeStruct((8, t), F32), jax.ShapeDtypeStruct((ne, 128), F32)] + c_shapes,
        scratch_shapes=[pltpu.VMEM((ne, 128), F32)],
        compiler_params=pltpu.CompilerParams(dimension_semantics=("arbitrary",), vmem_limit_bytes=VMEM_LIMIT),
        name="moe_router",
    )(x, w_router.T.astype(F32), *to_cast)
    return res[0], res[1], res[2:]


def _dispatch_kernel(pos_ref, grp_ref, x_ref, *rest, tm, te, ne, n_tiles, n_cast):
    cast_in, xs_ref, cast_out = rest[:n_cast], rest[n_cast], rest[n_cast + 1:-2]
    stage, sem = rest[-2:]
    i = pl.program_id(0)
    last = pl.num_programs(0) - 1
    def scatter_tile(s):
        stage[s] = x_ref[...].reshape(tm, 8, x_ref.shape[1] // 8)

        def row_copy(r, k):
            return pltpu.make_async_copy(
                stage.at[s, r], xs_ref.at[pos_ref[0, 0, 2 * r + k]], sem.at[s])

        def start(g, carry):
            for u in range(ROW_DMA_UNROLL):
                row_copy(g * ROW_DMA_UNROLL + u, 0).start()
                row_copy(g * ROW_DMA_UNROLL + u, 1).start()
            return carry

        lax.fori_loop(0, tm // ROW_DMA_UNROLL, start, 0)

    def wait_slot(s):
        for _ in range(TOP_K):
            pltpu.make_async_copy(stage.at[s], xs_ref.at[pl.ds(0, tm)], sem.at[s]).wait()

    for s in range(2):
        @pl.when(i % 2 == s)
        def _(s=s):
            scatter_tile(s)

            @pl.when(i > 0)
            def _():
                wait_slot(1 - s)

            @pl.when(i == last)
            def _():
                wait_slot(s)

    _cast_blocks(cast_in, cast_out)

    @pl.when(i == last)
    def _():
        for e in range(ne):
            lo, hi = grp_ref[e] + grp_ref[ne + e], grp_ref[e] + grp_ref[2 * ne + e]
            mid = jnp.minimum((lo + 7) // 8 * 8, hi)

            def pad_row(r):
                return pltpu.make_async_copy(stage.at[0, pl.ds(0, 1)], xs_ref.at[pl.ds(r, 1)], sem.at[0])

            def pad_rows8(q):
                r = pl.multiple_of(mid + q * 8, 8)
                return pltpu.make_async_copy(stage.at[0, pl.ds(0, 8)], xs_ref.at[pl.ds(r, 8)], sem.at[1])

            lax.fori_loop(lo, mid, lambda r, c: (pad_row(r).start(), c)[1], 0)
            lax.fori_loop(0, (hi - mid) // 8, lambda q, c: (pad_rows8(q).start(), c)[1], 0)
            lax.fori_loop(lo, mid, lambda r, c: (pad_row(r).wait(), c)[1], 0)
            lax.fori_loop(0, (hi - mid) // 8, lambda q, c: (pad_rows8(q).wait(), c)[1], 0)

        def tile_copy(j):
            return pltpu.make_async_copy(stage.at[0, pl.ds(0, te)], xs_ref.at[pl.ds(j * te, te)], sem.at[0])

        lax.fori_loop(grp_ref[3 * ne], n_tiles, lambda j, c: (tile_copy(j).start(), c)[1], 0)
        lax.fori_loop(grp_ref[3 * ne], n_tiles, lambda j, c: (tile_copy(j).wait(), c)[1], 0)


def _dispatch(x, pos, grp, n_tiles, tm, te, to_cast):
    t, d = x.shape
    ne = (grp.shape[0] - 1) // 3
    assert te <= tm
    c_in, c_out, c_shapes = _cast_plan(to_cast, t // tm)
    res = pl.pallas_call(
        functools.partial(_dispatch_kernel, tm=tm, te=te, ne=ne, n_tiles=n_tiles, n_cast=len(to_cast)),
        grid=(t // tm,),
        in_specs=[
            pl.BlockSpec((1, 1, 2 * tm), lambda i: (i, 0, 0), memory_space=pltpu.SMEM),
            pl.BlockSpec(memory_space=pltpu.SMEM),
            pl.BlockSpec((tm, d), lambda i: (i, 0)),
        ] + c_in,
        out_specs=[pl.BlockSpec(memory_space=pl.ANY)] + c_out,
        out_shape=[jax.ShapeDtypeStruct((n_tiles * te, 8, d // 8), F32)] + c_shapes,
        scratch_shapes=[pltpu.VMEM((2, tm, 8, d // 8), F32), pltpu.SemaphoreType.DMA((2,))],
        compiler_params=pltpu.CompilerParams(
            dimension_semantics=("arbitrary",), has_side_effects=True, vmem_limit_bytes=VMEM_LIMIT),
        name="moe_dispatch",
    )(pos.reshape(t // tm, 1, 2 * tm), grp, x, *to_cast)
    return res[0], res[1:]


def _experts_kernel(te_ref, nv_ref, xs_ref, wg_ref, wu_ref, wd_ref, ys_ref):
    i = pl.program_id(0)

    @pl.when(i < nv_ref[0])
    def _():
        tm = xs_ref.shape[0]
        xb = xs_ref[...].reshape(tm, wg_ref.shape[1]).astype(BF16)
        h = _silu(_mm(xb, wg_ref[0])) * _mm(xb, wu_ref[0])
        ys_ref[...] = _mm(h.astype(BF16), wd_ref[0]).reshape(ys_ref.shape)

    @pl.when(i >= nv_ref[0])
    def _():
        ys_ref[...] = jnp.zeros(ys_ref.shape, F32)


def _experts(xs, tile_expert, n_valid, wg, wu, wd, tm):
    n_rows, d = xs.shape[0], xs.shape[1] * xs.shape[2]
    ne, _, ff = wg.shape
    grid_spec = pltpu.PrefetchScalarGridSpec(
        num_scalar_prefetch=2,
        grid=(n_rows // tm,),
        in_specs=[
            pl.BlockSpec((tm, 8, d // 8), lambda i, te, nv: (i, 0, 0)),
            pl.BlockSpec((1, d, ff), lambda i, te, nv: (te[i], 0, 0)),
            pl.BlockSpec((1, d, ff), lambda i, te, nv: (te[i], 0, 0)),
            pl.BlockSpec((1, ff, d), lambda i, te, nv: (te[i], 0, 0)),
        ],
        out_specs=pl.BlockSpec((tm, 8, d // 8), lambda i, te, nv: (i, 0, 0)),
    )
    return pl.pallas_call(
        _experts_kernel,
        grid_spec=grid_spec,
        out_shape=jax.ShapeDtypeStruct((n_rows, 8, d // 8), F32),
        compiler_params=pltpu.CompilerParams(
            dimension_semantics=("arbitrary",), vmem_limit_bytes=VMEM_LIMIT),
        name="moe_experts",
    )(tile_expert, n_valid, xs, wg, wu, wd)


def _combine_kernel(pos_ref, pos_next_ref, x_ref, rt_ref, ys_ref, g_ref, b_ref, *rest, tm, alpha, npt):
    o_refs, (buf, sem) = rest[:-2], rest[-2:]
    i = pl.program_id(0)
    n = pl.num_programs(0)

    def gather(p_ref, s):
        def row_copy(r, k):
            return pltpu.make_async_copy(
                ys_ref.at[p_ref[0, 0, 2 * r + k]], buf.at[s, k, r], sem.at[s])

        def start(g, carry):
            for u in range(ROW_DMA_UNROLL):
                row_copy(g * ROW_DMA_UNROLL + u, 0).start()
                row_copy(g * ROW_DMA_UNROLL + u, 1).start()
            return carry

        lax.fori_loop(0, tm // ROW_DMA_UNROLL, start, 0)

    @pl.when(i == 0)
    def _():
        gather(pos_ref, 0)

    for s in range(2):
        @pl.when(i % 2 == s)
        def _(s=s):
            @pl.when(i + 1 < n)
            def _():
                gather(pos_next_ref, 1 - s)

            for k in range(TOP_K):
                pltpu.make_async_copy(ys_ref.at[pl.ds(0, tm)], buf.at[s, k], sem.at[s]).wait()
            rt = rt_ref[...]
            d = x_ref.shape[1]
            f = rt[:, 2:3] * buf[s, 0].reshape(tm, d) + rt[:, 3:4] * buf[s, 1].reshape(tm, d)
            _token_store(o_refs, _layer_norm(alpha * x_ref[...] + f, g_ref[...], b_ref[...]), npt)


def _combine(x, pos, route_t, ys, g, b, alpha, tm, split):
    t, d = x.shape
    out_specs, out_shape = _token_out(t, d, tm, split)
    npt = None if split is None else split[0] // tm
    n_steps = t // tm
    pos3 = pos.reshape(n_steps, 1, 2 * tm)
    return pl.pallas_call(
        functools.partial(_combine_kernel, tm=tm, alpha=alpha, npt=npt),
        grid=(t // tm,),
        in_specs=[
            pl.BlockSpec((1, 1, 2 * tm), lambda i: (i, 0, 0), memory_space=pltpu.SMEM),
            pl.BlockSpec((1, 1, 2 * tm), lambda i: (jnp.minimum(i + 1, n_steps - 1), 0, 0), memory_space=pltpu.SMEM),
            pl.BlockSpec((tm, d), lambda i: (i, 0)),
            pl.BlockSpec((tm, 8), lambda i: (i, 0)),
            pl.BlockSpec(memory_space=pl.ANY),
            _full_spec((1, d)), _full_spec((1, d)),
        ],
        out_specs=out_specs,
        out_shape=out_shape,
        scratch_shapes=[pltpu.VMEM((2, TOP_K, tm, 8, d // 8), F32), pltpu.SemaphoreType.DMA((2,))],
        compiler_params=pltpu.CompilerParams(dimension_semantics=("arbitrary",), vmem_limit_bytes=VMEM_LIMIT),
        name="moe_combine",
    )(pos3, pos3, x, route_t, ys, g.reshape(1, d), b.reshape(1, d))


def _ffn_moe(x, w_router, wg, wu, wd, g, b, alpha, tr, tm, split, pre):
    t, d = x.shape
    ne = w_router.shape[1]
    ff = wg.shape[2]
    flat = {"wg": wg.reshape(ne * d, ff), "wu": wu.reshape(ne * d, ff), "wd": wd.reshape(ne * ff, d)}
    conv = dict(pre)
    r_keys = [k for k in ("wd",) if k not in conv]
    d_keys = [k for k in ("wg", "wu") if k not in conv]
    route, counts, r_out = _router(x, w_router, tr, [flat[k] for k in r_keys])
    conv.update(zip(r_keys, r_out))
    cnt = counts[:, 0].astype(jnp.int32)
    gsz = ((cnt + tm - 1) // tm) * tm
    ends = jnp.cumsum(gsz)
    offs = ends - gsz
    n_tiles = (TOP_K * t) // tm + ne
    tile_start = jnp.arange(n_tiles, dtype=jnp.int32) * tm
    tile_e = jnp.sum((tile_start[:, None] >= ends[None, :]).astype(jnp.int32), axis=1)
    n_valid = (ends[ne - 1] // tm).astype(jnp.int32).reshape(1)
    last_e = jnp.sum((ends[ne - 1] - 1 >= ends).astype(jnp.int32))
    tile_e = jnp.minimum(tile_e, last_e).astype(jnp.int32)
    i12 = route[0:2].astype(jnp.int32)
    base = sum(jnp.where(i12 == e, offs[e], 0) for e in range(ne))
    pos = (base + route[4:6].astype(jnp.int32)).T.reshape(-1)
    grp = jnp.concatenate([offs, cnt, gsz, n_valid]).astype(jnp.int32)
    xs, d_out = _dispatch(x, pos, grp, n_tiles, tr, tm, [flat[k] for k in d_keys])
    conv.update(zip(d_keys, d_out))
    ys = _experts(xs, tile_e, n_valid, conv["wg"].reshape(ne, d, ff), conv["wu"].reshape(ne, d, ff),
                  conv["wd"].reshape(ne, ff, d), tm)
    return _combine(x, pos, route.T, ys, g, b, alpha, tr, split)


def kernel(x_prompt, x_sample, cache_conv, state_gla, w_in, b_in, conv_w, conv_b, conv_ln_g, conv_ln_b, w_conv_out, b_conv_out, w_gate_up, b_gate, gla_norm_g, w_gla_out, w_o, ln1_g, ln1_b, ln2_g, ln2_b, ff_w_gate, ff_w_up, ff_w_down, w_router, moe_w_gate, moe_w_up, moe_w_down):
    p = dict(w_in=w_in, b_in=b_in, conv_w=conv_w, conv_b=conv_b, conv_ln_g=conv_ln_g, conv_ln_b=conv_ln_b,
             w_conv_out=w_conv_out, b_conv_out=b_conv_out, w_gate_up=w_gate_up, b_gate=b_gate,
             gla_norm_g=gla_norm_g, w_gla_out=w_gla_out, w_o=w_o, ln1_g=ln1_g, ln1_b=ln1_b)
    depth = w_in.shape[0]
    bsz, seq, d = x_prompt.shape
    ns, ls, _ = x_sample.shape
    dc = conv_w.shape[-1]
    rank, dk = w_gate_up.shape[1], w_gate_up.shape[2]
    dv = w_gla_out.shape[1]
    dims = dict(d=d, dc=dc, dk=dk, dv=dv, rank=rank, hk=dk // GLA_HEADS, hv=dv // GLA_HEADS,
                alpha=(2.0 * depth) ** 0.25)
    alpha = dims["alpha"]
    tl = min(512, seq)
    tp = bsz * seq
    ts = ns * ls
    tm = min(512, ts)
    assert seq % tl == 0 and tp % ts == 0 and ts % tm == 0 and ts % tl == 0 and ns % SEQ_PER_STEP == 0

    x_p, x_s, s_blk = x_prompt.reshape(tp, d), x_sample.reshape(ts, d), 0
    hist_p, state_p, hist_s, state_s = [], [], [], []
    for l in range(depth):
        wts = _mixer_weights(l, p, dims)
        nxt = {}
        if l % 2 == 0 and l + 1 < depth:
            nxt = {k: m[l // 2].reshape(-1, m.shape[-1])
                   for k, m in (("wg", moe_w_gate), ("wu", moe_w_up), ("wd", moe_w_down))}
        x1s, hs, ss = _mixer_sample(x_s, s_blk, cache_conv, state_gla, l, wts, dims, ns, ls)
        x1, hp, sp, *done_mix = _mixer_prompt(x_p, x1s, wts, dims, tl, bsz, seq, [nxt["wu"]] if nxt else [])
        hist_p.append(hp), state_p.append(sp), hist_s.append(hs), state_s.append(ss)
        split = (tp, ts) if l == depth - 1 else None
        if l % 2 == 0:
            x2, done = _ffn_dense(x1, ff_w_gate[l // 2], ff_w_up[l // 2], ff_w_down[l // 2], ln2_g[l], ln2_b[l],
                                  alpha, tm, split, [nxt["wg"], nxt["wd"]] if nxt else [])
            pre_cast = dict(zip(("wg", "wd", "wu"), list(done) + done_mix)) if nxt else {}
        else:
            x2 = _ffn_moe(x1, w_router[l // 2], moe_w_gate[l // 2], moe_w_up[l // 2], moe_w_down[l // 2],
                          ln2_g[l], ln2_b[l], alpha, tm, min(EXPERT_TILE, tm), split, pre_cast)
        if split is None:
            x_p, x_s, s_blk = x2[0], x2[0], tp // ts
    y_p, y_s = x2
    return (y_p.reshape(bsz, seq, d), y_s.reshape(ns, ls, d), jnp.stack(hist_p),
            jnp.stack(state_p).astype(state_gla.dtype), jnp.stack(hist_s), jnp.stack(state_s).astype(state_gla.dtype))
```

```python
import functools

import jax
import jax.numpy as jnp
from jax import lax
from jax.experimental import pallas as pl
from jax.experimental.pallas import tpu as pltpu

CHUNK = 64
CONV_W = 31
GLA_HEADS = 4
GATE_TAU = 16.0
LN_EPS = 1e-5
RMS_EPS = 1e-6
TOP_K = 2

HIST_PAD = 32
CONV_ROWS = 32
ROW_DMA_UNROLL = 8
SEQ_PER_STEP = 4
EXPERT_TILE = 512
VMEM_LIMIT = 56 * 1024 * 1024

BF16 = jnp.bfloat16
F32 = jnp.float32


def _mm(a, b):
    return jnp.dot(a, b, preferred_element_type=F32)


def _sigmoid(x):
    return 0.5 * jnp.tanh(0.5 * x) + 0.5


def _silu(x):
    return x * _sigmoid(x)


def _log_sigmoid(z):
    return -(jnp.maximum(-z, 0.0) + jnp.log(1.0 + jnp.exp(-jnp.abs(z))))


def _layer_norm(x, g, b):
    mu = jnp.mean(x, axis=-1, keepdims=True)
    xc = x - mu
    var = jnp.mean(xc * xc, axis=-1, keepdims=True)
    return xc * lax.rsqrt(var + LN_EPS) * g + b


def _split3(x):
    hi = x.astype(BF16)
    r1 = x - hi.astype(F32)
    mid = r1.astype(BF16)
    lo = (r1 - mid.astype(F32)).astype(BF16)
    return hi, mid, lo


def _full_spec(shape):
    zeros = (0,) * len(shape)
    return pl.BlockSpec(shape, lambda *_: zeros, pipeline_mode=pl.Buffered(1))


def _project(xb, w, dims, u_store, q_s, k_s, v_s, lf_s):
    dc, dk, dv = dims["dc"], dims["dk"], dims["dv"]
    off_q, off_k, off_v = 2 * dc, 2 * dc + dk, 2 * dc + 2 * dk
    off_g = off_v + dv
    glu = _mm(xb, w["wmain"][:, 0:off_q]) + w["bmain"][:, 0:off_q]
    u_store(glu[:, 0:dc] * _sigmoid(glu[:, dc:off_q]))
    q_s[...] = (_mm(xb, w["wmain"][:, off_q:off_k]) + w["bmain"][:, off_q:off_k]) * (dims["hk"] ** -0.5)
    k_s[...] = _mm(xb, w["wmain"][:, off_k:off_v]) + w["bmain"][:, off_k:off_v]
    v_s[...] = _mm(xb, w["wmain"][:, off_v:off_g]) + w["bmain"][:, off_v:off_g]
    lr = _mm(xb, w["wlr"][...]) + w["blr"][...]
    z = _mm(lr.astype(BF16), w["wgu"][...]) + w["bgu"][...]
    lf_s[...] = _log_sigmoid(z) * (1.0 / GATE_TAU)


def _gla_chunk(q_s, k_s, v_s, lf_s, o_s, rows, c, get_state, set_state, dims):
    hk, hv, dk = dims["hk"], dims["hv"], dims["dk"]
    lf = lf_s[rows, :]
    ri = lax.broadcasted_iota(jnp.int32, (c, c), 0)
    ci = lax.broadcasted_iota(jnp.int32, (c, c), 1)
    causal = ri >= ci
    tri = causal.astype(BF16)
    hi, mid, lo = _split3(lf)
    b = _mm(tri, hi) + _mm(tri, mid) + _mm(tri, lo)
    b_last = b[c - 1:c, :]
    q_t = q_s[rows, :] * jnp.exp(b)
    kk = k_s[rows, :]
    k_t = kk * jnp.exp(-b)
    k_e = kk * jnp.exp(b_last - b)
    d_t = jnp.transpose(jnp.broadcast_to(jnp.exp(b_last), (128, dk)))
    vv = v_s[rows, :]
    for h in range(GLA_HEADS):
        ks = slice(h * hk, (h + 1) * hk)
        vs = slice(h * hv, (h + 1) * hv)
        qh = q_t[:, ks].astype(BF16)
        kh = k_t[:, ks].astype(BF16)
        keh = k_e[:, ks].astype(BF16)
        vh = vv[:, vs].astype(BF16)
        att = lax.dot_general(qh, kh, (((1,), (1,)), ((), ())), preferred_element_type=F32)
        att = jnp.where(causal, att, 0.0)
        s_old = get_state(h)
        o_s[rows, vs] = _mm(qh, s_old.astype(BF16)) + _mm(att.astype(BF16), vh)
        upd = lax.dot_general(keh, vh, (((0,), (0,)), ((), ())), preferred_element_type=F32)
        dcol = d_t[ks, :]
        decay = jnp.concatenate([dcol] * (hv // 128), axis=1)
        set_state(h, decay * s_old + upd)


def _conv_block(win, w8_ref, n):
    wn, ch = win.shape
    base = HIST_PAD - (CONV_W - 1)
    acc = None
    for b in range(8):
        wb = win if b == 0 else pltpu.roll(win, wn - b, axis=0)
        for a in range((base + CONV_W + 7) // 8):
            j = 8 * a + b - base
            if 0 <= j < CONV_W:
                term = w8_ref[8 * j:8 * j + 8, :][None] * wb[8 * a:8 * a + n, :].reshape(n // 8, 8, ch)
                acc = term if acc is None else acc + term
    return acc.reshape(n, ch)


def _tail(x, xb, c, w, dims, o_s):
    d, dc, dk, dv, hv = dims["d"], dims["dc"], dims["dk"], dims["dv"], dims["hv"]
    off_g = 2 * dc + 2 * dk + dv
    off_lr = off_g + dv
    c = _silu(_layer_norm(c + w["convb"][...], w["clng"][...], w["clnb"][...]))
    y_a = _mm(c.astype(BF16), w["wco"][...]) + w["bco"][...]
    g_out = _mm(xb, w["wmain"][:, off_g:off_lr]) + w["bmain"][:, off_g:off_lr]
    heads = []
    for h in range(GLA_HEADS):
        oh = o_s[:, h * hv:(h + 1) * hv]
        ms = jnp.mean(oh * oh, axis=-1, keepdims=True)
        heads.append(oh * lax.rsqrt(ms + RMS_EPS) * w["gng"][...])
    o = jnp.concatenate(heads, axis=1) * _silu(g_out)
    y_b = _mm(o.astype(BF16), w["wgo"][...])
    gates = _sigmoid(_mm(xb, w["wgates"][...]) + w["bgates"][...])
    merged = gates[:, 0:d] * y_a + gates[:, d:2 * d] * y_b
    m = _mm(merged.astype(BF16), w["wo"][...])
    return _layer_norm(dims["alpha"] * x + m, w["ln1g"][...], w["ln1b"][...])


_W_NAMES = ("wmain", "bmain", "wlr", "blr", "wgates", "bgates", "convw", "convb", "clng", "clnb",
            "wco", "bco", "wgu", "bgu", "gng", "wgo", "wo", "ln1g", "ln1b")


def _mixer_prompt_kernel(*refs, dims, tl, nt, n_prompt, n_cast):
    nw = len(_W_NAMES)
    x_ref, tail_ref = refs[0:2]
    w = dict(zip(_W_NAMES, refs[2:2 + nw]))
    cast_in = refs[2 + nw:2 + nw + n_cast]
    x1_ref, hist_ref, state_ref = refs[2 + nw + n_cast:5 + nw + n_cast]
    cast_out = refs[5 + nw + n_cast:5 + nw + 2 * n_cast]
    scratch = refs[5 + nw + 2 * n_cast:]
    i = pl.program_id(0)
    _cast_blocks(cast_in, cast_out)

    @pl.when(i < n_prompt)
    def _():
        _mixer_prompt_tile(i % nt, x_ref, w, x1_ref, hist_ref, state_ref, scratch, dims, tl, nt)

    @pl.when(i >= n_prompt)
    def _():
        x1_ref[...] = tail_ref[...]


def _mixer_prompt_tile(j, x_ref, w, x1_ref, hist_ref, state_ref, scratch, dims, tl, nt):
    ubuf, qt_s, kt_s, ke_s, vb_s, g_s, gt_s, o_s, c_s, s_s, dt_s = scratch
    d, dc, dk, dv, hk, hv = dims["d"], dims["dc"], dims["dk"], dims["dv"], dims["hk"], dims["hv"]
    off_q, off_k, off_v = 2 * dc, 2 * dc + dk, 2 * dc + 2 * dk
    off_g = off_v + dv
    off_lr = off_g + dv
    nch = tl // CHUNK
    rep = 128 // nch
    wm, bm = w["wmain"], w["bmain"]

    @pl.when(j == 0)
    def _():
        ubuf[0:HIST_PAD, :] = jnp.zeros((HIST_PAD, dc), F32)
        s_s[...] = jnp.zeros(s_s.shape, F32)

    x = x_ref[...]
    xb = x.astype(BF16)

    glu = _mm(xb, wm[:, 0:off_q]) + bm[:, 0:off_q]
    ubuf[HIST_PAD:HIST_PAD + tl, :] = glu[:, 0:dc] * _sigmoid(glu[:, dc:off_q])
    lr = _mm(xb, w["wlr"][...]) + w["blr"][...]
    z = _mm(lr.astype(BF16), w["wgu"][...]) + w["bgu"][...]
    lf = _log_sigmoid(z) * (1.0 / GATE_TAU)
    ri = lax.broadcasted_iota(jnp.int32, (CHUNK, CHUNK), 0)
    ci = lax.broadcasted_iota(jnp.int32, (CHUNK, CHUNK), 1)
    causal = ri >= ci
    tri = causal.astype(BF16)
    parts = _split3(lf)
    b = jnp.concatenate(
        [sum(_mm(tri, p[c * CHUNK:(c + 1) * CHUNK, :]) for p in parts) for c in range(nch)], axis=0)
    b3 = b.reshape(nch, CHUNK, dk)
    b_last = b3[:, CHUNK - 1:CHUNK, :]
    q = (_mm(xb, wm[:, off_q:off_k]) + bm[:, off_q:off_k]) * (hk ** -0.5)
    qt_s[...] = (q * jnp.exp(b)).astype(BF16)
    kk = _mm(xb, wm[:, off_k:off_v]) + bm[:, off_k:off_v]
    kt_s[...] = (kk * jnp.exp(-b)).astype(BF16)
    ke_s[...] = (kk.reshape(nch, CHUNK, dk) * jnp.exp(b_last - b3)).reshape(tl, dk).astype(BF16)
    vb_s[...] = (_mm(xb, wm[:, off_v:off_g]) + bm[:, off_v:off_g]).astype(BF16)
    d_rows = jnp.broadcast_to(jnp.exp(b_last), (nch, rep, dk)).reshape(nch * rep, dk)
    if nch * rep < 128:
        d_rows = jnp.concatenate([d_rows, jnp.zeros((128 - nch * rep, dk), F32)], axis=0)
    dt_s[...] = jnp.transpose(d_rows)
    g_s[...] = _silu(_mm(xb, wm[:, off_g:off_lr]) + bm[:, off_g:off_lr])

    states = [s_s[h] for h in range(GLA_HEADS)]
    conv_per = tl // CONV_ROWS // nch
    gw = 2 * d // nch
    for c in range(nch):
        rows = slice(c * CHUNK, (c + 1) * CHUNK)
        for h in range(GLA_HEADS):
            ks = slice(h * hk, (h + 1) * hk)
            vs = slice(h * hv, (h + 1) * hv)
            qh, kh, keh, vh = qt_s[rows, ks], kt_s[rows, ks], ke_s[rows, ks], vb_s[rows, vs]
            att = lax.dot_general(qh, kh, (((1,), (1,)), ((), ())), preferred_element_type=F32)
            att = jnp.where(causal, att, 0.0)
            s_old = states[h]
            o_s[rows, vs] = _mm(qh, s_old.astype(BF16)) + _mm(att.astype(BF16), vh)
            upd = lax.dot_general(keh, vh, (((0,), (0,)), ((), ())), preferred_element_type=F32)
            decay = jnp.broadcast_to(dt_s[ks, c * rep:c * rep + 1], (hk, hv))
            states[h] = decay * s_old + upd
        for bi in range(c * conv_per, (c + 1) * conv_per):
            r0 = bi * CONV_ROWS
            c_s[r0:r0 + CONV_ROWS, :] = _conv_block(
                ubuf[r0:r0 + CONV_ROWS + HIST_PAD, :], w["convw"], CONV_ROWS)
        cols = slice(c * gw, (c + 1) * gw)
        gt_s[:, cols] = _sigmoid(_mm(xb, w["wgates"][:, cols]) + w["bgates"][:, cols])
    for h in range(GLA_HEADS):
        s_s[h] = states[h]

    cv = _silu(_layer_norm(c_s[...] + w["convb"][...], w["clng"][...], w["clnb"][...]))
    y_a = _mm(cv.astype(BF16), w["wco"][...]) + w["bco"][...]
    heads = []
    for h in range(GLA_HEADS):
        oh = o_s[:, h * hv:(h + 1) * hv]
        ms = jnp.mean(oh * oh, axis=-1, keepdims=True)
        heads.append(oh * lax.rsqrt(ms + RMS_EPS) * w["gng"][...])
    o = jnp.concatenate(heads, axis=1) * g_s[...]
    y_b = _mm(o.astype(BF16), w["wgo"][...])
    merged = gt_s[:, 0:d] * y_a + gt_s[:, d:2 * d] * y_b
    m = _mm(merged.astype(BF16), w["wo"][...])
    x1_ref[...] = _layer_norm(dims["alpha"] * x + m, w["ln1g"][...], w["ln1b"][...])

    @pl.when(j == nt - 1)
    def _():
        hist_ref[0] = ubuf[tl + HIST_PAD - (CONV_W - 1):tl + HIST_PAD, :]
        state_ref[0] = s_s[...]

    ubuf[0:HIST_PAD, :] = ubuf[tl:tl + HIST_PAD, :]


def _mixer_sample_kernel(*refs, dims, ns, ls):
    nw = len(_W_NAMES)
    x_ref, hist_in_ref, state_in_ref = refs[0:3]
    w = dict(zip(_W_NAMES, refs[3:3 + nw]))
    x1_ref, hist_ref, state_ref = refs[3 + nw:6 + nw]
    ubuf, q_s, k_s, v_s, lf_s, o_s, c_s = refs[6 + nw:]
    i = pl.program_id(0)
    dc = dims["dc"]
    hl = CONV_W - 1

    @pl.when(i == 0)
    def _():
        xb = x_ref[...].astype(BF16)
        ubuf[:, 0:8, :] = jnp.zeros((ns, 8, dc), F32)
        ubuf[:, HIST_PAD - hl:HIST_PAD, :] = hist_in_ref[0]

        def u_store(u):
            ubuf[:, HIST_PAD:HIST_PAD + ls, :] = u.reshape(ns, ls, dc)

        _project(xb, w, dims, u_store, q_s, k_s, v_s, lf_s)

    for s in range(SEQ_PER_STEP):
        seq = i * SEQ_PER_STEP + s
        rows = pl.ds(pl.multiple_of(seq * ls, ls), ls)

        def set_state(h, val, s=s):
            state_ref[s, h] = val

        _gla_chunk(q_s, k_s, v_s, lf_s, o_s, rows, ls, lambda h, s=s: state_in_ref[0, s, h], set_state, dims)
        win = ubuf[seq]
        c_s[rows, :] = _conv_block(win, w["convw"], ls)
        hist_ref[seq] = win[HIST_PAD + ls - hl:HIST_PAD + ls, :]

    @pl.when(i == ns // SEQ_PER_STEP - 1)
    def _():
        x = x_ref[...]
        x1_ref[...] = _tail(x, x.astype(BF16), c_s[...], w, dims, o_s)


def _mixer_weights(l, p, dims):
    dc, dk, dv, rank = dims["dc"], dims["dk"], dims["dv"], dims["rank"]
    off_lr = 2 * dc + 2 * dk + 2 * dv
    off_gates = off_lr + rank
    w_in, b_in = p["w_in"][l], p["b_in"][l]
    row = lambda v: v.reshape(1, -1).astype(F32)
    return (
        w_in[:, :off_lr].astype(BF16), row(b_in[:off_lr]),
        w_in[:, off_lr:off_gates].astype(BF16), row(b_in[off_lr:off_gates]),
        w_in[:, off_gates:].astype(BF16), row(b_in[off_gates:]),
        jnp.repeat(p["conv_w"][l].astype(F32), 8, axis=0),
        row(p["conv_b"][l]), row(p["conv_ln_g"][l]), row(p["conv_ln_b"][l]),
        p["w_conv_out"][l].astype(BF16), row(p["b_conv_out"][l]),
        p["w_gate_up"][l].astype(BF16), row(p["b_gate"][l]),
        row(p["gla_norm_g"][l]),
        p["w_gla_out"][l].astype(BF16), p["w_o"][l].astype(BF16),
        row(p["ln1_g"][l]), row(p["ln1_b"][l]),
    )


def _mixer_prompt(x, x1_tail, wts, dims, tl, bsz, seq, to_cast):
    d = x.shape[1]
    dc, dk, dv, hk, hv = dims["dc"], dims["dk"], dims["dv"], dims["hk"], dims["hv"]
    hl = CONV_W - 1
    nt = seq // tl
    n_prompt = bsz * nt
    n_tail = x1_tail.shape[0] // tl
    kern = functools.partial(_mixer_prompt_kernel, dims=dims, tl=tl, nt=nt, n_prompt=n_prompt,
                             n_cast=len(to_cast))
    seq_of = lambda i: jnp.minimum(i // nt, bsz - 1)
    c_in, c_out, c_shapes = _cast_plan(to_cast, n_prompt + n_tail) if to_cast else ([], [], [])
    return pl.pallas_call(
        kern,
        grid=(n_prompt + n_tail,),
        in_specs=[pl.BlockSpec((tl, d), lambda i: (jnp.minimum(i, n_prompt - 1), 0)),
                  pl.BlockSpec((tl, d), lambda i: (jnp.maximum(i - n_prompt, 0), 0))]
        + [_full_spec(a.shape) for a in wts] + c_in,
        out_specs=[
            pl.BlockSpec((tl, d), lambda i: (i, 0)),
            pl.BlockSpec((1, hl, dc), lambda i: (seq_of(i), 0, 0)),
            pl.BlockSpec((1, GLA_HEADS, hk, hv), lambda i: (seq_of(i), 0, 0, 0)),
        ] + c_out,
        out_shape=[
            jax.ShapeDtypeStruct(((n_prompt + n_tail) * tl, d), F32),
            jax.ShapeDtypeStruct((bsz, hl, dc), F32),
            jax.ShapeDtypeStruct((bsz, GLA_HEADS, hk, hv), F32),
        ] + c_shapes,
        scratch_shapes=[
            pltpu.VMEM((HIST_PAD + tl, dc), F32),
            pltpu.VMEM((tl, dk), BF16), pltpu.VMEM((tl, dk), BF16), pltpu.VMEM((tl, dk), BF16),
            pltpu.VMEM((tl, dv), BF16),
            pltpu.VMEM((tl, dv), F32), pltpu.VMEM((tl, 2 * d), F32),
            pltpu.VMEM((tl, dv), F32), pltpu.VMEM((tl, dc), F32),
            pltpu.VMEM((GLA_HEADS, hk, hv), F32), pltpu.VMEM((dk, 128), F32),
        ],
        compiler_params=pltpu.CompilerParams(
            dimension_semantics=("arbitrary",), vmem_limit_bytes=VMEM_LIMIT),
        name="mixer_prompt",
    )(x, x1_tail, *wts, *to_cast)


def _mixer_sample(x, in_blk, hist, state, layer, wts, dims, ns, ls):
    d = x.shape[1]
    dc, dk, dv, hk, hv = dims["dc"], dims["dk"], dims["dv"], dims["hk"], dims["hv"]
    hl = CONV_W - 1
    t = ns * ls
    kern = functools.partial(_mixer_sample_kernel, dims=dims, ns=ns, ls=ls)
    x1, hist_o, state_o = pl.pallas_call(
        kern,
        grid=(ns // SEQ_PER_STEP,),
        in_specs=[
            pl.BlockSpec((t, d), lambda i: (in_blk, 0)),
            pl.BlockSpec((1, ns, hl, dc), lambda i: (layer, 0, 0, 0)),
            pl.BlockSpec((1, SEQ_PER_STEP, GLA_HEADS, hk, hv), lambda i: (layer, i, 0, 0, 0)),
        ] + [_full_spec(a.shape) for a in wts],
        out_specs=[
            pl.BlockSpec((t, d), lambda i: (0, 0)),
            pl.BlockSpec((ns, hl, dc), lambda i: (0, 0, 0)),
            pl.BlockSpec((SEQ_PER_STEP, GLA_HEADS, hk, hv), lambda i: (i, 0, 0, 0)),
        ],
        out_shape=[
            jax.ShapeDtypeStruct((t, d), F32),
            jax.ShapeDtypeStruct((ns, hl, dc), F32),
            jax.ShapeDtypeStruct((ns, GLA_HEADS, hk, hv), F32),
        ],
        scratch_shapes=[
            pltpu.VMEM((ns, HIST_PAD + ls, dc), F32),
            pltpu.VMEM((t, dk), F32), pltpu.VMEM((t, dk), F32), pltpu.VMEM((t, dv), F32),
            pltpu.VMEM((t, dk), F32), pltpu.VMEM((t, dv), F32), pltpu.VMEM((t, dc), F32),
        ],
        compiler_params=pltpu.CompilerParams(
            dimension_semantics=("arbitrary",), vmem_limit_bytes=VMEM_LIMIT),
        name="mixer_sample",
    )(x, hist, state, *wts)
    return x1, hist_o, state_o


def _token_out(t, d, tm, split):
    if split is None:
        return [pl.BlockSpec((tm, d), lambda i: (i, 0))], [jax.ShapeDtypeStruct((t, d), F32)]
    tp, ts = split
    npt = tp // tm
    specs = [pl.BlockSpec((tm, d), lambda i: (jnp.minimum(i, npt - 1), 0)),
             pl.BlockSpec((tm, d), lambda i: (jnp.maximum(i - npt, 0), 0))]
    return specs, [jax.ShapeDtypeStruct((tp, d), F32), jax.ShapeDtypeStruct((ts, d), F32)]


def _token_store(o_refs, val, npt):
    if len(o_refs) == 1:
        o_refs[0][...] = val
        return
    i = pl.program_id(0)

    @pl.when(i < npt)
    def _():
        o_refs[0][...] = val

    @pl.when(i >= npt)
    def _():
        o_refs[1][...] = val


def _ffn_dense_kernel(x_ref, wg_ref, wu_ref, wd_ref, g_ref, b_ref, *rest, alpha, npt, n_cast):
    cast_in, o_refs, cast_out = rest[:n_cast], rest[n_cast:len(rest) - n_cast], rest[len(rest) - n_cast:]
    _cast_blocks(cast_in, cast_out)
    x = x_ref[...]
    xb = x.astype(BF16)
    h = _silu(_mm(xb, wg_ref[...])) * _mm(xb, wu_ref[...])
    f = _mm(h.astype(BF16), wd_ref[...])
    _token_store(o_refs, _layer_norm(alpha * x + f, g_ref[...], b_ref[...]), npt)


def _ffn_dense(x, wg, wu, wd, g, b, alpha, tm, split, to_cast):
    t, d = x.shape
    ops = (wg.astype(BF16), wu.astype(BF16), wd.astype(BF16), g.reshape(1, d), b.reshape(1, d))
    out_specs, out_shape = _token_out(t, d, tm, split)
    npt = None if split is None else split[0] // tm
    c_in, c_out, c_shapes = _cast_plan(to_cast, t // tm) if to_cast else ([], [], [])
    res = pl.pallas_call(
        functools.partial(_ffn_dense_kernel, alpha=alpha, npt=npt, n_cast=len(to_cast)),
        grid=(t // tm,),
        in_specs=[pl.BlockSpec((tm, d), lambda i: (i, 0))] + [_full_spec(a.shape) for a in ops] + c_in,
        out_specs=out_specs + c_out,
        out_shape=out_shape + c_shapes,
        compiler_params=pltpu.CompilerParams(
            dimension_semantics=("arbitrary",), vmem_limit_bytes=VMEM_LIMIT),
        name="ffn_dense",
    )(x, *ops, *to_cast)
    return res[:len(out_shape)], res[len(out_shape):]


def _cast_plan(arrays, n_steps):
    n_blocks = max(n for n in range(1, n_steps + 1)
                   if all(a.shape[0] % n == 0 and (a.shape[0] // n) % 16 == 0 for a in arrays))
    spec = lambda a: pl.BlockSpec((a.shape[0] // n_blocks, a.shape[1]), lambda i: (jnp.minimum(i, n_blocks - 1), 0))
    return ([spec(a) for a in arrays], [spec(a) for a in arrays],
            [jax.ShapeDtypeStruct(a.shape, BF16) for a in arrays])


def _cast_blocks(src_refs, dst_refs):
    for src, dst in zip(src_refs, dst_refs):
        dst[...] = src[...].astype(BF16)


def _router_kernel(x_ref, wr_ref, *rest, ne, tr, n_cast):
    cast_in, (route_ref, cnt_ref), cast_out = rest[:n_cast], rest[n_cast:n_cast + 2], rest[n_cast + 2:-1]
    carry = rest[-1]
    i = pl.program_id(0)
    _cast_blocks(cast_in, cast_out)

    @pl.when(i == 0)
    def _():
        carry[...] = jnp.zeros(carry.shape, F32)

    xh = x_ref[...]
    x_hi = xh.astype(BF16)
    x_lo = (xh - x_hi.astype(F32)).astype(BF16)
    wr = wr_ref[...]
    w_hi = wr.astype(BF16)
    w_lo = (wr - w_hi.astype(F32)).astype(BF16)
    nt = (((1,), (1,)), ((), ()))
    logits = (lax.dot_general(w_hi, x_hi, nt, preferred_element_type=F32)
              + lax.dot_general(w_hi, x_lo, nt, preferred_element_type=F32)
              + lax.dot_general(w_lo, x_hi, nt, preferred_element_type=F32))
    mx = jnp.max(logits, axis=0, keepdims=True)
    ex = jnp.exp(logits - mx)
    probs = ex / jnp.sum(ex, axis=0, keepdims=True)
    eid = lax.broadcasted_iota(jnp.int32, (ne, tr), 0)
    p1 = jnp.max(probs, axis=0, keepdims=True)
    i1 = jnp.min(jnp.where(probs == p1, eid, ne), axis=0, keepdims=True)
    rest = jnp.where(eid == i1, -1.0, probs)
    p2 = jnp.max(rest, axis=0, keepdims=True)
    i2 = jnp.min(jnp.where(rest == p2, eid, ne), axis=0, keepdims=True)
    den = p1 + p2
    oh1 = (eid == i1).astype(F32)
    oh2 = (eid == i2).astype(F32)
    oh = oh1 + oh2
    ri = lax.broadcasted_iota(jnp.int32, (tr, tr), 0)
    ci = lax.broadcasted_iota(jnp.int32, (tr, tr), 1)
    upper = (ri <= ci).astype(BF16)
    incl = _mm(oh.astype(BF16), upper)
    before = carry[:, 0:1] + incl - oh
    r1 = jnp.sum(oh1 * before, axis=0, keepdims=True)
    r2 = jnp.sum(oh2 * before, axis=0, keepdims=True)
    zero = jnp.zeros((1, tr), F32)
    route_ref[...] = jnp.concatenate(
        [i1.astype(F32), i2.astype(F32), p1 / den, p2 / den, r1, r2, zero, zero], axis=0)
    total = carry[:, 0:1] + incl[:, tr - 1:tr]
    carry[...] = jnp.broadcast_to(total, carry.shape)
    cnt_ref[...] = jnp.broadcast_to(total, cnt_ref.shape)


def _router(x, w_router, tr, to_cast):
    t, d = x.shape
    ne = w_router.shape[1]
    c_in, c_out, c_shapes = _cast_plan(to_cast, t // tr)
    res = pl.pallas_call(
        functools.partial(_router_kernel, ne=ne, tr=tr, n_cast=len(to_cast)),
        grid=(t // tr,),
        in_specs=[pl.BlockSpec((tr, d), lambda i: (i, 0)), _full_spec((ne, d))] + c_in,
        out_specs=[pl.BlockSpec((8, tr), lambda i: (0, i)), pl.BlockSpec((ne, 128), lambda i: (0, 0))] + c_out,
        out_shape=[jax.ShapeDtypeStruct((8, t), F32), jax.ShapeDtypeStruct((ne, 128), F32)] + c_shapes,
        scratch_shapes=[pltpu.VMEM((ne, 128), F32)],
        compiler_params=pltpu.CompilerParams(dimension_semantics=("arbitrary",), vmem_limit_bytes=VMEM_LIMIT),
        name="moe_router",
    )(x, w_router.T.astype(F32), *to_cast)
    return res[0], res[1], res[2:]


def _dispatch_kernel(pos_ref, grp_ref, x_ref, *rest, tm, te, ne, n_tiles, n_cast):
    cast_in, xs_ref, cast_out = rest[:n_cast], rest[n_cast], rest[n_cast + 1:-2]
    stage, sem = rest[-2:]
    i = pl.program_id(0)
    last = pl.num_programs(0) - 1
    def scatter_tile(s):
        stage[s] = x_ref[...].reshape(tm, 8, x_ref.shape[1] // 8)

        def row_copy(r, k):
            return pltpu.make_async_copy(
                stage.at[s, r], xs_ref.at[pos_ref[0, 0, 2 * r + k]], sem.at[s])

        def start(g, carry):
            for u in range(ROW_DMA_UNROLL):
                row_copy(g * ROW_DMA_UNROLL + u, 0).start(priority=0)
                row_copy(g * ROW_DMA_UNROLL + u, 1).start(priority=1)
            return carry

        lax.fori_loop(0, tm // ROW_DMA_UNROLL, start, 0)

    def wait_slot(s):
        for _ in range(TOP_K):
            pltpu.make_async_copy(stage.at[s], xs_ref.at[pl.ds(0, tm)], sem.at[s]).wait()

    for s in range(2):
        @pl.when(i % 2 == s)
        def _(s=s):
            scatter_tile(s)

            @pl.when(i > 0)
            def _():
                wait_slot(1 - s)

            @pl.when(i == last)
            def _():
                wait_slot(s)

    _cast_blocks(cast_in, cast_out)

    @pl.when(i == last)
    def _():
        for e in range(ne):
            lo, hi = grp_ref[e] + grp_ref[ne + e], grp_ref[e] + grp_ref[2 * ne + e]
            mid = jnp.minimum((lo + 7) // 8 * 8, hi)

            def pad_row(r):
                return pltpu.make_async_copy(stage.at[0, pl.ds(0, 1)], xs_ref.at[pl.ds(r, 1)], sem.at[0])

            def pad_rows8(q):
                r = pl.multiple_of(mid + q * 8, 8)
                return pltpu.make_async_copy(stage.at[0, pl.ds(0, 8)], xs_ref.at[pl.ds(r, 8)], sem.at[1])

            lax.fori_loop(lo, mid, lambda r, c: (pad_row(r).start(), c)[1], 0)
            lax.fori_loop(0, (hi - mid) // 8, lambda q, c: (pad_rows8(q).start(), c)[1], 0)
            lax.fori_loop(lo, mid, lambda r, c: (pad_row(r).wait(), c)[1], 0)
            lax.fori_loop(0, (hi - mid) // 8, lambda q, c: (pad_rows8(q).wait(), c)[1], 0)

        def tile_copy(j):
            return pltpu.make_async_copy(stage.at[0, pl.ds(0, te)], xs_ref.at[pl.ds(j * te, te)], sem.at[0])

        lax.fori_loop(grp_ref[3 * ne], n_tiles, lambda j, c: (tile_copy(j).start(), c)[1], 0)
        lax.fori_loop(grp_ref[3 * ne], n_tiles, lambda j, c: (tile_copy(j).wait(), c)[1], 0)


def _dispatch(x, pos, grp, n_tiles, tm, te, to_cast):
    t, d = x.shape
    ne = (grp.shape[0] - 1) // 3
    assert te <= tm
    c_in, c_out, c_shapes = _cast_plan(to_cast, t // tm)
    res = pl.pallas_call(
        functools.partial(_dispatch_kernel, tm=tm, te=te, ne=ne, n_tiles=n_tiles, n_cast=len(to_cast)),
        grid=(t // tm,),
        in_specs=[
            pl.BlockSpec((1, 1, 2 * tm), lambda i: (i, 0, 0), memory_space=pltpu.SMEM),
            pl.BlockSpec(memory_space=pltpu.SMEM),
            pl.BlockSpec((tm, d), lambda i: (i, 0)),
        ] + c_in,
        out_specs=[pl.BlockSpec(memory_space=pl.ANY)] + c_out,
        out_shape=[jax.ShapeDtypeStruct((n_tiles * te, 8, d // 8), F32)] + c_shapes,
        scratch_shapes=[pltpu.VMEM((2, tm, 8, d // 8), F32), pltpu.SemaphoreType.DMA((2,))],
        compiler_params=pltpu.CompilerParams(
            dimension_semantics=("arbitrary",), has_side_effects=True, vmem_limit_bytes=VMEM_LIMIT),
        name="moe_dispatch",
    )(pos.reshape(t // tm, 1, 2 * tm), grp, x, *to_cast)
    return res[0], res[1:]


def _experts_kernel(te_ref, nv_ref, xs_ref, wg_ref, wu_ref, wd_ref, ys_ref):
    i = pl.program_id(0)

    @pl.when(i < nv_ref[0])
    def _():
        tm = xs_ref.shape[0]
        xb = xs_ref[...].reshape(tm, wg_ref.shape[1]).astype(BF16)
        h = _silu(_mm(xb, wg_ref[0])) * _mm(xb, wu_ref[0])
        ys_ref[...] = _mm(h.astype(BF16), wd_ref[0]).reshape(ys_ref.shape)

    @pl.when(i >= nv_ref[0])
    def _():
        ys_ref[...] = jnp.zeros(ys_ref.shape, F32)


def _experts(xs, tile_expert, n_valid, wg, wu, wd, tm):
    n_rows, d = xs.shape[0], xs.shape[1] * xs.shape[2]
    ne, _, ff = wg.shape
    grid_spec = pltpu.PrefetchScalarGridSpec(
        num_scalar_prefetch=2,
        grid=(n_rows // tm,),
        in_specs=[
            pl.BlockSpec((tm, 8, d // 8), lambda i, te, nv: (i, 0, 0)),
            pl.BlockSpec((1, d, ff), lambda i, te, nv: (te[i], 0, 0)),
            pl.BlockSpec((1, d, ff), lambda i, te, nv: (te[i], 0, 0)),
            pl.BlockSpec((1, ff, d), lambda i, te, nv: (te[i], 0, 0)),
        ],
        out_specs=pl.BlockSpec((tm, 8, d // 8), lambda i, te, nv: (i, 0, 0)),
    )
    return pl.pallas_call(
        _experts_kernel,
        grid_spec=grid_spec,
        out_shape=jax.ShapeDtypeStruct((n_rows, 8, d // 8), F32),
        compiler_params=pltpu.CompilerParams(
            dimension_semantics=("arbitrary",), vmem_limit_bytes=VMEM_LIMIT),
        name="moe_experts",
    )(tile_expert, n_valid, xs, wg, wu, wd)


def _combine_kernel(pos_ref, pos_next_ref, x_ref, rt_ref, ys_ref, g_ref, b_ref, *rest, tm, alpha, npt):
    o_refs, (buf, sem) = rest[:-2], rest[-2:]
    i = pl.program_id(0)
    n = pl.num_programs(0)

    def gather(p_ref, s):
        def row_copy(r, k):
            return pltpu.make_async_copy(
                ys_ref.at[p_ref[0, 0, 2 * r + k]], buf.at[s, k, r], sem.at[s])

        def start(g, carry):
            for u in range(ROW_DMA_UNROLL):
                row_copy(g * ROW_DMA_UNROLL + u, 0).start(priority=0)
                row_copy(g * ROW_DMA_UNROLL + u, 1).start(priority=1)
            return carry

        lax.fori_loop(0, tm // ROW_DMA_UNROLL, start, 0)

    @pl.when(i == 0)
    def _():
        gather(pos_ref, 0)

    for s in range(2):
        @pl.when(i % 2 == s)
        def _(s=s):
            @pl.when(i + 1 < n)
            def _():
                gather(pos_next_ref, 1 - s)

            for k in range(TOP_K):
                pltpu.make_async_copy(ys_ref.at[pl.ds(0, tm)], buf.at[s, k], sem.at[s]).wait()
            rt = rt_ref[...]
            d = x_ref.shape[1]
            f = rt[:, 2:3] * buf[s, 0].reshape(tm, d) + rt[:, 3:4] * buf[s, 1].reshape(tm, d)
            _token_store(o_refs, _layer_norm(alpha * x_ref[...] + f, g_ref[...], b_ref[...]), npt)


def _combine(x, pos, route_t, ys, g, b, alpha, tm, split):
    t, d = x.shape
    out_specs, out_shape = _token_out(t, d, tm, split)
    npt = None if split is None else split[0] // tm
    n_steps = t // tm
    pos3 = pos.reshape(n_steps, 1, 2 * tm)
    return pl.pallas_call(
        functools.partial(_combine_kernel, tm=tm, alpha=alpha, npt=npt),
        grid=(t // tm,),
        in_specs=[
            pl.BlockSpec((1, 1, 2 * tm), lambda i: (i, 0, 0), memory_space=pltpu.SMEM),
            pl.BlockSpec((1, 1, 2 * tm), lambda i: (jnp.minimum(i + 1, n_steps - 1), 0, 0), memory_space=pltpu.SMEM),
            pl.BlockSpec((tm, d), lambda i: (i, 0)),
            pl.BlockSpec((tm, 8), lambda i: (i, 0)),
            pl.BlockSpec(memory_space=pl.ANY),
            _full_spec((1, d)), _full_spec((1, d)),
        ],
        out_specs=out_specs,
        out_shape=out_shape,
        scratch_shapes=[pltpu.VMEM((2, TOP_K, tm, 8, d // 8), F32), pltpu.SemaphoreType.DMA((2,))],
        compiler_params=pltpu.CompilerParams(dimension_semantics=("arbitrary",), vmem_limit_bytes=VMEM_LIMIT),
        name="moe_combine",
    )(pos3, pos3, x, route_t, ys, g.reshape(1, d), b.reshape(1, d))


def _ffn_moe(x, w_router, wg, wu, wd, g, b, alpha, tr, tm, split, pre):
    t, d = x.shape
    ne = w_router.shape[1]
    ff = wg.shape[2]
    flat = {"wg": wg.reshape(ne * d, ff), "wu": wu.reshape(ne * d, ff), "wd": wd.reshape(ne * ff, d)}
    conv = dict(pre)
    r_keys = [k for k in ("wd",) if k not in conv]
    d_keys = [k for k in ("wg", "wu") if k not in conv]
    route, counts, r_out = _router(x, w_router, tr, [flat[k] for k in r_keys])
    conv.update(zip(r_keys, r_out))
    cnt = counts[:, 0].astype(jnp.int32)
    gsz = ((cnt + tm - 1) // tm) * tm
    ends = jnp.cumsum(gsz)
    offs = ends - gsz
    n_tiles = (TOP_K * t) // tm + ne
    tile_start = jnp.arange(n_tiles, dtype=jnp.int32) * tm
    tile_e = jnp.sum((tile_start[:, None] >= ends[None, :]).astype(jnp.int32), axis=1)
    n_valid = (ends[ne - 1] // tm).astype(jnp.int32).reshape(1)
    last_e = jnp.sum((ends[ne - 1] - 1 >= ends).astype(jnp.int32))
    tile_e = jnp.minimum(tile_e, last_e).astype(jnp.int32)
    i12 = route[0:2].astype(jnp.int32)
    base = sum(jnp.where(i12 == e, offs[e], 0) for e in range(ne))
    pos = (base + route[4:6].astype(jnp.int32)).T.reshape(-1)
    grp = jnp.concatenate([offs, cnt, gsz, n_valid]).astype(jnp.int32)
    xs, d_out = _dispatch(x, pos, grp, n_tiles, tr, tm, [flat[k] for k in d_keys])
    conv.update(zip(d_keys, d_out))
    ys = _experts(xs, tile_e, n_valid, conv["wg"].reshape(ne, d, ff), conv["wu"].reshape(ne, d, ff),
                  conv["wd"].reshape(ne, ff, d), tm)
    return _combine(x, pos, route.T, ys, g, b, alpha, tr, split)


def kernel(x_prompt, x_sample, cache_conv, state_gla, w_in, b_in, conv_w, conv_b, conv_ln_g, conv_ln_b, w_conv_out, b_conv_out, w_gate_up, b_gate, gla_norm_g, w_gla_out, w_o, ln1_g, ln1_b, ln2_g, ln2_b, ff_w_gate, ff_w_up, ff_w_down, w_router, moe_w_gate, moe_w_up, moe_w_down):
    p = dict(w_in=w_in, b_in=b_in, conv_w=conv_w, conv_b=conv_b, conv_ln_g=conv_ln_g, conv_ln_b=conv_ln_b,
             w_conv_out=w_conv_out, b_conv_out=b_conv_out, w_gate_up=w_gate_up, b_gate=b_gate,
             gla_norm_g=gla_norm_g, w_gla_out=w_gla_out, w_o=w_o, ln1_g=ln1_g, ln1_b=ln1_b)
    depth = w_in.shape[0]
    bsz, seq, d = x_prompt.shape
    ns, ls, _ = x_sample.shape
    dc = conv_w.shape[-1]
    rank, dk = w_gate_up.shape[1], w_gate_up.shape[2]
    dv = w_gla_out.shape[1]
    dims = dict(d=d, dc=dc, dk=dk, dv=dv, rank=rank, hk=dk // GLA_HEADS, hv=dv // GLA_HEADS,
                alpha=(2.0 * depth) ** 0.25)
    alpha = dims["alpha"]
    tl = min(512, seq)
    tp = bsz * seq
    ts = ns * ls
    tm = min(512, ts)
    assert seq % tl == 0 and tp % ts == 0 and ts % tm == 0 and ts % tl == 0 and ns % SEQ_PER_STEP == 0

    x_p, x_s, s_blk = x_prompt.reshape(tp, d), x_sample.reshape(ts, d), 0
    hist_p, state_p, hist_s, state_s = [], [], [], []
    for l in range(depth):
        wts = _mixer_weights(l, p, dims)
        nxt = {}
        if l % 2 == 0 and l + 1 < depth:
            nxt = {k: m[l // 2].reshape(-1, m.shape[-1])
                   for k, m in (("wg", moe_w_gate), ("wu", moe_w_up), ("wd", moe_w_down))}
        x1s, hs, ss = _mixer_sample(x_s, s_blk, cache_conv, state_gla, l, wts, dims, ns, ls)
        x1, hp, sp, *done_mix = _mixer_prompt(x_p, x1s, wts, dims, tl, bsz, seq, [nxt["wu"]] if nxt else [])
        hist_p.append(hp), state_p.append(sp), hist_s.append(hs), state_s.append(ss)
        split = (tp, ts) if l == depth - 1 else None
        if l % 2 == 0:
            x2, done = _ffn_dense(x1, ff_w_gate[l // 2], ff_w_up[l // 2], ff_w_down[l // 2], ln2_g[l], ln2_b[l],
                                  alpha, tm, split, [nxt["wg"], nxt["wd"]] if nxt else [])
            pre_cast = dict(zip(("wg", "wd", "wu"), list(done) + done_mix)) if nxt else {}
        else:
            x2 = _ffn_moe(x1, w_router[l // 2], moe_w_gate[l // 2], moe_w_up[l // 2], moe_w_down[l // 2],
                          ln2_g[l], ln2_b[l], alpha, tm, min(EXPERT_TILE, tm), split, pre_cast)
        if split is None:
            x_p, x_s, s_blk = x2[0], x2[0], tp // ts
    y_p, y_s = x2
    return (y_p.reshape(bsz, seq, d), y_s.reshape(ns, ls, d), jnp.stack(hist_p),
            jnp.stack(state_p).astype(state_gla.dtype), jnp.stack(hist_s), jnp.stack(state_s).astype(state_gla.dtype))
```

```python
import functools

import jax
import jax.numpy as jnp
from jax import lax
from jax.experimental import pallas as pl
from jax.experimental.pallas import tpu as pltpu

CHUNK = 64
CONV_W = 31
GLA_HEADS = 4
GATE_TAU = 16.0
LN_EPS = 1e-5
RMS_EPS = 1e-6
TOP_K = 2

HIST_PAD = 32
CONV_ROWS = 32
ROW_DMA_UNROLL = 8
SEQ_PER_STEP = 4
EXPERT_TILE = 512
VMEM_LIMIT = 56 * 1024 * 1024

BF16 = jnp.bfloat16
F32 = jnp.float32


def _mm(a, b):
    return jnp.dot(a, b, preferred_element_type=F32)


def _sigmoid(x):
    return 0.5 * jnp.tanh(0.5 * x) + 0.5


def _silu(x):
    return x * _sigmoid(x)


def _log_sigmoid(z):
    return -(jnp.maximum(-z, 0.0) + jnp.log(1.0 + jnp.exp(-jnp.abs(z))))


def _layer_norm(x, g, b):
    mu = jnp.mean(x, axis=-1, keepdims=True)
    xc = x - mu
    var = jnp.mean(xc * xc, axis=-1, keepdims=True)
    return xc * lax.rsqrt(var + LN_EPS) * g + b


def _split3(x):
    hi = x.astype(BF16)
    r1 = x - hi.astype(F32)
    mid = r1.astype(BF16)
    lo = (r1 - mid.astype(F32)).astype(BF16)
    return hi, mid, lo


def _full_spec(shape):
    zeros = (0,) * len(shape)
    return pl.BlockSpec(shape, lambda *_: zeros, pipeline_mode=pl.Buffered(1))


def _project(xb, w, dims, u_store, q_s, k_s, v_s, lf_s):
    dc, dk, dv = dims["dc"], dims["dk"], dims["dv"]
    off_q, off_k, off_v = 2 * dc, 2 * dc + dk, 2 * dc + 2 * dk
    off_g = off_v + dv
    glu = _mm(xb, w["wmain"][:, 0:off_q]) + w["bmain"][:, 0:off_q]
    u_store(glu[:, 0:dc] * _sigmoid(glu[:, dc:off_q]))
    q_s[...] = (_mm(xb, w["wmain"][:, off_q:off_k]) + w["bmain"][:, off_q:off_k]) * (dims["hk"] ** -0.5)
    k_s[...] = _mm(xb, w["wmain"][:, off_k:off_v]) + w["bmain"][:, off_k:off_v]
    v_s[...] = _mm(xb, w["wmain"][:, off_v:off_g]) + w["bmain"][:, off_v:off_g]
    lr = _mm(xb, w["wlr"][:, 0:dims["rank"]]) + w["blr"][...]
    z = _mm(lr.astype(BF16), w["wgu"][...]) + w["bgu"][...]
    lf_s[...] = _log_sigmoid(z) * (1.0 / GATE_TAU)


def _gla_chunk(q_s, k_s, v_s, lf_s, o_s, rows, c, get_state, set_state, dims):
    hk, hv, dk = dims["hk"], dims["hv"], dims["dk"]
    lf = lf_s[rows, :]
    ri = lax.broadcasted_iota(jnp.int32, (c, c), 0)
    ci = lax.broadcasted_iota(jnp.int32, (c, c), 1)
    causal = ri >= ci
    tri = causal.astype(BF16)
    hi, mid, lo = _split3(lf)
    b = _mm(tri, hi) + _mm(tri, mid) + _mm(tri, lo)
    b_last = b[c - 1:c, :]
    q_t = q_s[rows, :] * jnp.exp(b)
    kk = k_s[rows, :]
    k_t = kk * jnp.exp(-b)
    k_e = kk * jnp.exp(b_last - b)
    d_t = jnp.transpose(jnp.broadcast_to(jnp.exp(b_last), (128, dk)))
    vv = v_s[rows, :]
    for h in range(GLA_HEADS):
        ks = slice(h * hk, (h + 1) * hk)
        vs = slice(h * hv, (h + 1) * hv)
        qh = q_t[:, ks].astype(BF16)
        kh = k_t[:, ks].astype(BF16)
        keh = k_e[:, ks].astype(BF16)
        vh = vv[:, vs].astype(BF16)
        att = lax.dot_general(qh, kh, (((1,), (1,)), ((), ())), preferred_element_type=F32)
        att = jnp.where(causal, att, 0.0)
        s_old = get_state(h)
        o_s[rows, vs] = _mm(qh, s_old.astype(BF16)) + _mm(att.astype(BF16), vh)
        upd = lax.dot_general(keh, vh, (((0,), (0,)), ((), ())), preferred_element_type=F32)
        dcol = d_t[ks, :]
        decay = jnp.concatenate([dcol] * (hv // 128), axis=1)
        set_state(h, decay * s_old + upd)


def _conv_block(win, w8_ref, n):
    wn, ch = win.shape
    base = HIST_PAD - (CONV_W - 1)
    acc = None
    for b in range(8):
        wb = win if b == 0 else pltpu.roll(win, wn - b, axis=0)
        for a in range((base + CONV_W + 7) // 8):
            j = 8 * a + b - base
            if 0 <= j < CONV_W:
                term = w8_ref[8 * j:8 * j + 8, :][None] * wb[8 * a:8 * a + n, :].reshape(n // 8, 8, ch)
                acc = term if acc is None else acc + term
    return acc.reshape(n, ch)


def _tail(x, xb, c, w, dims, o_s):
    d, dc, dk, dv, hv = dims["d"], dims["dc"], dims["dk"], dims["dv"], dims["hv"]
    off_g = 2 * dc + 2 * dk + dv
    off_lr = off_g + dv
    c = _silu(_layer_norm(c + w["convb"][...], w["clng"][...], w["clnb"][...]))
    y_a = _mm(c.astype(BF16), w["wco"][...]) + w["bco"][...]
    g_out = _mm(xb, w["wmain"][:, off_g:off_lr]) + w["bmain"][:, off_g:off_lr]
    heads = []
    for h in range(GLA_HEADS):
        oh = o_s[:, h * hv:(h + 1) * hv]
        ms = jnp.mean(oh * oh, axis=-1, keepdims=True)
        heads.append(oh * lax.rsqrt(ms + RMS_EPS) * w["gng"][...])
    o = jnp.concatenate(heads, axis=1) * _silu(g_out)
    y_b = _mm(o.astype(BF16), w["wgo"][...])
    gates = _sigmoid(_mm(xb, w["wgates"][...]) + w["bgates"][...])
    merged = gates[:, 0:d] * y_a + gates[:, d:2 * d] * y_b
    m = _mm(merged.astype(BF16), w["wo"][...])
    return _layer_norm(dims["alpha"] * x + m, w["ln1g"][...], w["ln1b"][...])


_W_NAMES = ("wmain", "bmain", "wlr", "blr", "wgates", "bgates", "convw", "convb", "clng", "clnb",
            "wco", "bco", "wgu", "bgu", "gng", "wgo", "wo", "ln1g", "ln1b")


def _weight_refs(refs):
    w = dict(zip(_W_NAMES, refs))
    w["wmain"], w["wlr"] = w["wmain"].at[0], w["wlr"].at[0]
    return w


def _mixer_prompt_kernel(*refs, dims, tl, nt, n_prompt, n_cast):
    nw = len(_W_NAMES)
    x_ref, tail_ref = refs[0:2]
    w = _weight_refs(refs[2:2 + nw])
    cast_in = refs[2 + nw:2 + nw + n_cast]
    x1_ref, hist_ref, state_ref = refs[2 + nw + n_cast:5 + nw + n_cast]
    cast_out = refs[5 + nw + n_cast:5 + nw + 2 * n_cast]
    scratch = refs[5 + nw + 2 * n_cast:]
    i = pl.program_id(0)
    _cast_blocks(cast_in, cast_out)

    @pl.when(i < n_prompt)
    def _():
        _mixer_prompt_tile(i % nt, x_ref, w, x1_ref, hist_ref, state_ref, scratch, dims, tl, nt)

    @pl.when(i >= n_prompt)
    def _():
        x1_ref[...] = tail_ref[...]


def _mixer_prompt_tile(j, x_ref, w, x1_ref, hist_ref, state_ref, scratch, dims, tl, nt):
    ubuf, qt_s, kt_s, ke_s, vb_s, g_s, gt_s, o_s, c_s, s_s, dt_s = scratch
    d, dc, dk, dv, hk, hv = dims["d"], dims["dc"], dims["dk"], dims["dv"], dims["hk"], dims["hv"]
    off_q, off_k, off_v = 2 * dc, 2 * dc + dk, 2 * dc + 2 * dk
    off_g = off_v + dv
    off_lr = off_g + dv
    nch = tl // CHUNK
    rep = 128 // nch
    wm, bm = w["wmain"], w["bmain"]

    @pl.when(j == 0)
    def _():
        ubuf[0:HIST_PAD, :] = jnp.zeros((HIST_PAD, dc), F32)
        s_s[...] = jnp.zeros(s_s.shape, F32)

    x = x_ref[...]
    xb = x.astype(BF16)

    glu = _mm(xb, wm[:, 0:off_q]) + bm[:, 0:off_q]
    ubuf[HIST_PAD:HIST_PAD + tl, :] = glu[:, 0:dc] * _sigmoid(glu[:, dc:off_q])
    lr = _mm(xb, w["wlr"][:, 0:dims["rank"]]) + w["blr"][...]
    z = _mm(lr.astype(BF16), w["wgu"][...]) + w["bgu"][...]
    lf = _log_sigmoid(z) * (1.0 / GATE_TAU)
    ri = lax.broadcasted_iota(jnp.int32, (CHUNK, CHUNK), 0)
    ci = lax.broadcasted_iota(jnp.int32, (CHUNK, CHUNK), 1)
    causal = ri >= ci
    tri = causal.astype(BF16)
    parts = _split3(lf)
    b = jnp.concatenate(
        [sum(_mm(tri, p[c * CHUNK:(c + 1) * CHUNK, :]) for p in parts) for c in range(nch)], axis=0)
    b3 = b.reshape(nch, CHUNK, dk)
    b_last = b3[:, CHUNK - 1:CHUNK, :]
    q = (_mm(xb, wm[:, off_q:off_k]) + bm[:, off_q:off_k]) * (hk ** -0.5)
    qt_s[...] = (q * jnp.exp(b)).astype(BF16)
    kk = _mm(xb, wm[:, off_k:off_v]) + bm[:, off_k:off_v]
    kt_s[...] = (kk * jnp.exp(-b)).astype(BF16)
    ke_s[...] = (kk.reshape(nch, CHUNK, dk) * jnp.exp(b_last - b3)).reshape(tl, dk).astype(BF16)
    vb_s[...] = (_mm(xb, wm[:, off_v:off_g]) + bm[:, off_v:off_g]).astype(BF16)
    d_rows = jnp.broadcast_to(jnp.exp(b_last), (nch, rep, dk)).reshape(nch * rep, dk)
    if nch * rep < 128:
        d_rows = jnp.concatenate([d_rows, jnp.zeros((128 - nch * rep, dk), F32)], axis=0)
    dt_s[...] = jnp.transpose(d_rows)
    g_s[...] = _silu(_mm(xb, wm[:, off_g:off_lr]) + bm[:, off_g:off_lr])

    states = [s_s[h] for h in range(GLA_HEADS)]
    conv_per = tl // CONV_ROWS // nch
    gw = 2 * d // nch
    for c in range(nch):
        rows = slice(c * CHUNK, (c + 1) * CHUNK)
        for h in range(GLA_HEADS):
            ks = slice(h * hk, (h + 1) * hk)
            vs = slice(h * hv, (h + 1) * hv)
            qh, kh, keh, vh = qt_s[rows, ks], kt_s[rows, ks], ke_s[rows, ks], vb_s[rows, vs]
            att = lax.dot_general(qh, kh, (((1,), (1,)), ((), ())), preferred_element_type=F32)
            att = jnp.where(causal, att, 0.0)
            s_old = states[h]
            o_s[rows, vs] = _mm(qh, s_old.astype(BF16)) + _mm(att.astype(BF16), vh)
            upd = lax.dot_general(keh, vh, (((0,), (0,)), ((), ())), preferred_element_type=F32)
            decay = jnp.broadcast_to(dt_s[ks, c * rep:c * rep + 1], (hk, hv))
            states[h] = decay * s_old + upd
        for bi in range(c * conv_per, (c + 1) * conv_per):
            r0 = bi * CONV_ROWS
            c_s[r0:r0 + CONV_ROWS, :] = _conv_block(
                ubuf[r0:r0 + CONV_ROWS + HIST_PAD, :], w["convw"], CONV_ROWS)
        cols = slice(c * gw, (c + 1) * gw)
        gt_s[:, cols] = _sigmoid(_mm(xb, w["wgates"][:, cols]) + w["bgates"][:, cols])
    for h in range(GLA_HEADS):
        s_s[h] = states[h]

    cv = _silu(_layer_norm(c_s[...] + w["convb"][...], w["clng"][...], w["clnb"][...]))
    y_a = _mm(cv.astype(BF16), w["wco"][...]) + w["bco"][...]
    heads = []
    for h in range(GLA_HEADS):
        oh = o_s[:, h * hv:(h + 1) * hv]
        ms = jnp.mean(oh * oh, axis=-1, keepdims=True)
        heads.append(oh * lax.rsqrt(ms + RMS_EPS) * w["gng"][...])
    o = jnp.concatenate(heads, axis=1) * g_s[...]
    y_b = _mm(o.astype(BF16), w["wgo"][...])
    merged = gt_s[:, 0:d] * y_a + gt_s[:, d:2 * d] * y_b
    m = _mm(merged.astype(BF16), w["wo"][...])
    x1_ref[...] = _layer_norm(dims["alpha"] * x + m, w["ln1g"][...], w["ln1b"][...])

    @pl.when(j == nt - 1)
    def _():
        hist_ref[0] = ubuf[tl + HIST_PAD - (CONV_W - 1):tl + HIST_PAD, :]
        state_ref[0] = s_s[...]

    ubuf[0:HIST_PAD, :] = ubuf[tl:tl + HIST_PAD, :]


def _mixer_sample_kernel(*refs, dims, ns, ls):
    nw = len(_W_NAMES)
    x_ref, hist_in_ref, state_in_ref = refs[0:3]
    w = _weight_refs(refs[3:3 + nw])
    x1_ref, hist_ref, state_ref = refs[3 + nw:6 + nw]
    ubuf, q_s, k_s, v_s, lf_s, o_s, c_s = refs[6 + nw:]
    i = pl.program_id(0)
    dc = dims["dc"]
    hl = CONV_W - 1

    @pl.when(i == 0)
    def _():
        xb = x_ref[...].astype(BF16)
        ubuf[:, 0:8, :] = jnp.zeros((ns, 8, dc), F32)
        ubuf[:, HIST_PAD - hl:HIST_PAD, :] = hist_in_ref[0]

        def u_store(u):
            ubuf[:, HIST_PAD:HIST_PAD + ls, :] = u.reshape(ns, ls, dc)

        _project(xb, w, dims, u_store, q_s, k_s, v_s, lf_s)

    for s in range(SEQ_PER_STEP):
        seq = i * SEQ_PER_STEP + s
        rows = pl.ds(pl.multiple_of(seq * ls, ls), ls)

        def set_state(h, val, s=s):
            state_ref[s, h] = val

        _gla_chunk(q_s, k_s, v_s, lf_s, o_s, rows, ls, lambda h, s=s: state_in_ref[0, s, h], set_state, dims)
        win = ubuf[seq]
        c_s[rows, :] = _conv_block(win, w["convw"], ls)
        hist_ref[seq] = win[HIST_PAD + ls - hl:HIST_PAD + ls, :]

    @pl.when(i == ns // SEQ_PER_STEP - 1)
    def _():
        x = x_ref[...]
        x1_ref[...] = _tail(x, x.astype(BF16), c_s[...], w, dims, o_s)


def _mixer_weights(l, p, dims, w_in_b):
    dc, dk, dv, rank = dims["dc"], dims["dk"], dims["dv"], dims["rank"]
    off_lr = 2 * dc + 2 * dk + 2 * dv
    off_gates = off_lr + rank
    assert off_lr % 128 == 0 and rank <= 128
    b_in = p["b_in"][l]
    d = w_in_b.shape[1]
    row = lambda v: v.reshape(1, -1).astype(F32)
    ops = (
        w_in_b, row(b_in[:off_lr]),
        w_in_b, row(b_in[off_lr:off_gates]),
        w_in_b[l][:, off_gates:], row(b_in[off_gates:]),
        jnp.repeat(p["conv_w"][l].astype(F32), 8, axis=0),
        row(p["conv_b"][l]), row(p["conv_ln_g"][l]), row(p["conv_ln_b"][l]),
        p["w_conv_out"][l].astype(BF16), row(p["b_conv_out"][l]),
        p["w_gate_up"][l].astype(BF16), row(p["b_gate"][l]),
        row(p["gla_norm_g"][l]),
        p["w_gla_out"][l].astype(BF16), p["w_o"][l].astype(BF16),
        row(p["ln1_g"][l]), row(p["ln1_b"][l]),
    )
    specs = [_full_spec(a.shape) for a in ops]
    specs[0] = pl.BlockSpec((1, d, off_lr), lambda *_: (l, 0, 0), pipeline_mode=pl.Buffered(1))
    specs[2] = pl.BlockSpec((1, d, 128), lambda *_: (l, 0, off_lr // 128), pipeline_mode=pl.Buffered(1))
    return ops, specs


def _mixer_prompt(x, x1_tail, wts, dims, tl, bsz, seq, to_cast):
    d = x.shape[1]
    dc, dk, dv, hk, hv = dims["dc"], dims["dk"], dims["dv"], dims["hk"], dims["hv"]
    hl = CONV_W - 1
    nt = seq // tl
    n_prompt = bsz * nt
    n_tail = x1_tail.shape[0] // tl
    kern = functools.partial(_mixer_prompt_kernel, dims=dims, tl=tl, nt=nt, n_prompt=n_prompt,
                             n_cast=len(to_cast))
    seq_of = lambda i: jnp.minimum(i // nt, bsz - 1)
    c_in, c_out, c_shapes = _cast_plan(to_cast, n_prompt + n_tail) if to_cast else ([], [], [])
    return pl.pallas_call(
        kern,
        grid=(n_prompt + n_tail,),
        in_specs=[pl.BlockSpec((tl, d), lambda i: (jnp.minimum(i, n_prompt - 1), 0)),
                  pl.BlockSpec((tl, d), lambda i: (jnp.maximum(i - n_prompt, 0), 0))]
        + wts[1] + c_in,
        out_specs=[
            pl.BlockSpec((tl, d), lambda i: (i, 0)),
            pl.BlockSpec((1, hl, dc), lambda i: (seq_of(i), 0, 0)),
            pl.BlockSpec((1, GLA_HEADS, hk, hv), lambda i: (seq_of(i), 0, 0, 0)),
        ] + c_out,
        out_shape=[
            jax.ShapeDtypeStruct(((n_prompt + n_tail) * tl, d), F32),
            jax.ShapeDtypeStruct((bsz, hl, dc), F32),
            jax.ShapeDtypeStruct((bsz, GLA_HEADS, hk, hv), F32),
        ] + c_shapes,
        scratch_shapes=[
            pltpu.VMEM((HIST_PAD + tl, dc), F32),
            pltpu.VMEM((tl, dk), BF16), pltpu.VMEM((tl, dk), BF16), pltpu.VMEM((tl, dk), BF16),
            pltpu.VMEM((tl, dv), BF16),
            pltpu.VMEM((tl, dv), F32), pltpu.VMEM((tl, 2 * d), F32),
            pltpu.VMEM((tl, dv), F32), pltpu.VMEM((tl, dc), F32),
            pltpu.VMEM((GLA_HEADS, hk, hv), F32), pltpu.VMEM((dk, 128), F32),
        ],
        compiler_params=pltpu.CompilerParams(
            dimension_semantics=("arbitrary",), vmem_limit_bytes=VMEM_LIMIT),
        name="mixer_prompt",
    )(x, x1_tail, *wts[0], *to_cast)


def _mixer_sample(x, in_blk, hist, state, layer, wts, dims, ns, ls):
    d = x.shape[1]
    dc, dk, dv, hk, hv = dims["dc"], dims["dk"], dims["dv"], dims["hk"], dims["hv"]
    hl = CONV_W - 1
    t = ns * ls
    kern = functools.partial(_mixer_sample_kernel, dims=dims, ns=ns, ls=ls)
    x1, hist_o, state_o = pl.pallas_call(
        kern,
        grid=(ns // SEQ_PER_STEP,),
        in_specs=[
            pl.BlockSpec((t, d), lambda i: (in_blk, 0)),
            pl.BlockSpec((1, ns, hl, dc), lambda i: (layer, 0, 0, 0)),
            pl.BlockSpec((1, SEQ_PER_STEP, GLA_HEADS, hk, hv), lambda i: (layer, i, 0, 0, 0)),
        ] + wts[1],
        out_specs=[
            pl.BlockSpec((t, d), lambda i: (0, 0)),
            pl.BlockSpec((ns, hl, dc), lambda i: (0, 0, 0)),
            pl.BlockSpec((SEQ_PER_STEP, GLA_HEADS, hk, hv), lambda i: (i, 0, 0, 0)),
        ],
        out_shape=[
            jax.ShapeDtypeStruct((t, d), F32),
            jax.ShapeDtypeStruct((ns, hl, dc), F32),
            jax.ShapeDtypeStruct((ns, GLA_HEADS, hk, hv), F32),
        ],
        scratch_shapes=[
            pltpu.VMEM((ns, HIST_PAD + ls, dc), F32),
            pltpu.VMEM((t, dk), F32), pltpu.VMEM((t, dk), F32), pltpu.VMEM((t, dv), F32),
            pltpu.VMEM((t, dk), F32), pltpu.VMEM((t, dv), F32), pltpu.VMEM((t, dc), F32),
        ],
        compiler_params=pltpu.CompilerParams(
            dimension_semantics=("arbitrary",), vmem_limit_bytes=VMEM_LIMIT),
        name="mixer_sample",
    )(x, hist, state, *wts[0])
    return x1, hist_o, state_o


def _token_out(t, d, tm, split):
    if split is None:
        return [pl.BlockSpec((tm, d), lambda i: (i, 0))], [jax.ShapeDtypeStruct((t, d), F32)]
    tp, ts = split
    npt = tp // tm
    specs = [pl.BlockSpec((tm, d), lambda i: (jnp.minimum(i, npt - 1), 0)),
             pl.BlockSpec((tm, d), lambda i: (jnp.maximum(i - npt, 0), 0))]
    return specs, [jax.ShapeDtypeStruct((tp, d), F32), jax.ShapeDtypeStruct((ts, d), F32)]


def _token_store(o_refs, val, npt):
    if len(o_refs) == 1:
        o_refs[0][...] = val
        return
    i = pl.program_id(0)

    @pl.when(i < npt)
    def _():
        o_refs[0][...] = val

    @pl.when(i >= npt)
    def _():
        o_refs[1][...] = val


def _ffn_dense_kernel(x_ref, wg_ref, wu_ref, wd_ref, g_ref, b_ref, *rest, alpha, npt, n_cast):
    cast_in, o_refs, cast_out = rest[:n_cast], rest[n_cast:len(rest) - n_cast], rest[len(rest) - n_cast:]
    _cast_blocks(cast_in, cast_out)
    x = x_ref[...]
    xb = x.astype(BF16)
    h = _silu(_mm(xb, wg_ref[...])) * _mm(xb, wu_ref[...])
    f = _mm(h.astype(BF16), wd_ref[...])
    _token_store(o_refs, _layer_norm(alpha * x + f, g_ref[...], b_ref[...]), npt)


def _ffn_dense(x, wg, wu, wd, g, b, alpha, tm, split, to_cast):
    t, d = x.shape
    ops = (wg.astype(BF16), wu.astype(BF16), wd.astype(BF16), g.reshape(1, d), b.reshape(1, d))
    out_specs, out_shape = _token_out(t, d, tm, split)
    npt = None if split is None else split[0] // tm
    c_in, c_out, c_shapes = _cast_plan(to_cast, t // tm) if to_cast else ([], [], [])
    res = pl.pallas_call(
        functools.partial(_ffn_dense_kernel, alpha=alpha, npt=npt, n_cast=len(to_cast)),
        grid=(t // tm,),
        in_specs=[pl.BlockSpec((tm, d), lambda i: (i, 0))] + [_full_spec(a.shape) for a in ops] + c_in,
        out_specs=out_specs + c_out,
        out_shape=out_shape + c_shapes,
        compiler_params=pltpu.CompilerParams(
            dimension_semantics=("arbitrary",), vmem_limit_bytes=VMEM_LIMIT),
        name="ffn_dense",
    )(x, *ops, *to_cast)
    return res[:len(out_shape)], res[len(out_shape):]


def _cast_plan(arrays, n_steps):
    n_blocks = max(n for n in range(1, n_steps + 1)
                   if all(a.shape[0] % n == 0 and (a.shape[0] // n) % 16 == 0 for a in arrays))
    spec = lambda a: pl.BlockSpec((a.shape[0] // n_blocks, a.shape[1]), lambda i: (jnp.minimum(i, n_blocks - 1), 0))
    return ([spec(a) for a in arrays], [spec(a) for a in arrays],
            [jax.ShapeDtypeStruct(a.shape, BF16) for a in arrays])


def _cast_blocks(src_refs, dst_refs):
    for src, dst in zip(src_refs, dst_refs):
        dst[...] = src[...].astype(BF16)


def _router_kernel(x_ref, wr_ref, *rest, ne, tr, n_cast):
    cast_in, (route_ref, cnt_ref), cast_out = rest[:n_cast], rest[n_cast:n_cast + 2], rest[n_cast + 2:-2]
    carry, upper_s = rest[-2:]
    i = pl.program_id(0)
    _cast_blocks(cast_in, cast_out)

    @pl.when(i == 0)
    def _():
        carry[...] = jnp.zeros(carry.shape, F32)
        ri = lax.broadcasted_iota(jnp.int32, (tr, tr), 0)
        ci = lax.broadcasted_iota(jnp.int32, (tr, tr), 1)
        upper_s[...] = (ri <= ci).astype(BF16)

    xh = x_ref[...]
    x_hi = xh.astype(BF16)
    x_lo = (xh - x_hi.astype(F32)).astype(BF16)
    wr = wr_ref[...]
    w_hi = wr.astype(BF16)
    w_lo = (wr - w_hi.astype(F32)).astype(BF16)
    nt = (((1,), (1,)), ((), ()))
    logits = (lax.dot_general(w_hi, x_hi, nt, preferred_element_type=F32)
              + lax.dot_general(w_hi, x_lo, nt, preferred_element_type=F32)
              + lax.dot_general(w_lo, x_hi, nt, preferred_element_type=F32))
    mx = jnp.max(logits, axis=0, keepdims=True)
    ex = jnp.exp(logits - mx)
    probs = ex / jnp.sum(ex, axis=0, keepdims=True)
    eid = lax.broadcasted_iota(jnp.int32, (ne, tr), 0)
    p1 = jnp.max(probs, axis=0, keepdims=True)
    i1 = jnp.min(jnp.where(probs == p1, eid, ne), axis=0, keepdims=True)
    rest = jnp.where(eid == i1, -1.0, probs)
    p2 = jnp.max(rest, axis=0, keepdims=True)
    i2 = jnp.min(jnp.where(rest == p2, eid, ne), axis=0, keepdims=True)
    den = p1 + p2
    oh1 = (eid == i1).astype(F32)
    oh2 = (eid == i2).astype(F32)
    oh = oh1 + oh2
    incl = _mm(oh.astype(BF16), upper_s[...])
    before = carry[:, 0:1] + incl - oh
    r1 = jnp.sum(oh1 * before, axis=0, keepdims=True)
    r2 = jnp.sum(oh2 * before, axis=0, keepdims=True)
    zero = jnp.zeros((1, tr), F32)
    route_ref[...] = jnp.concatenate(
        [i1.astype(F32), i2.astype(F32), p1 / den, p2 / den, r1, r2, zero, zero], axis=0)
    total = carry[:, 0:1] + incl[:, tr - 1:tr]
    carry[...] = jnp.broadcast_to(total, carry.shape)
    cnt_ref[...] = jnp.broadcast_to(total, cnt_ref.shape)


def _router(x, w_router, tr, to_cast):
    t, d = x.shape
    ne = w_router.shape[1]
    c_in, c_out, c_shapes = _cast_plan(to_cast, t // tr)
    res = pl.pallas_call(
        functools.partial(_router_kernel, ne=ne, tr=tr, n_cast=len(to_cast)),
        grid=(t // tr,),
        in_specs=[pl.BlockSpec((tr, d), lambda i: (i, 0)), _full_spec((ne, d))] + c_in,
        out_specs=[pl.BlockSpec((8, tr), lambda i: (0, i)), pl.BlockSpec((ne, 128), lambda i: (0, 0))] + c_out,
        out_shape=[jax.ShapeDtypeStruct((8, t), F32), jax.ShapeDtypeStruct((ne, 128), F32)] + c_shapes,
        scratch_shapes=[pltpu.VMEM((ne, 128), F32), pltpu.VMEM((tr, tr), BF16)],
        compiler_params=pltpu.CompilerParams(dimension_semantics=("arbitrary",), vmem_limit_bytes=VMEM_LIMIT),
        name="moe_router",
    )(x, w_router.T.astype(F32), *to_cast)
    return res[0], res[1], res[2:]


def _dispatch_kernel(pos_ref, grp_ref, x_ref, *rest, tm, te, ne, n_tiles, n_cast):
    cast_in, xs_ref, cast_out = rest[:n_cast], rest[n_cast], rest[n_cast + 1:-2]
    stage, sem = rest[-2:]
    i = pl.program_id(0)
    last = pl.num_programs(0) - 1
    def scatter_tile(s):
        stage[s] = x_ref[...].reshape(tm, 8, x_ref.shape[1] // 8)

        def row_copy(r, k):
            return pltpu.make_async_copy(
                stage.at[s, r], xs_ref.at[pos_ref[0, 0, 2 * r + k]], sem.at[s])

        def start(g, carry):
            for u in range(ROW_DMA_UNROLL):
                row_copy(g * ROW_DMA_UNROLL + u, 0).start(priority=0)
                row_copy(g * ROW_DMA_UNROLL + u, 1).start(priority=1)
            return carry

        lax.fori_loop(0, tm // ROW_DMA_UNROLL, start, 0)

    def wait_slot(s):
        for _ in range(TOP_K):
            pltpu.make_async_copy(stage.at[s], xs_ref.at[pl.ds(0, tm)], sem.at[s]).wait()

    for s in range(2):
        @pl.when(i % 2 == s)
        def _(s=s):
            scatter_tile(s)

            @pl.when(i > 0)
            def _():
                wait_slot(1 - s)

            @pl.when(i == last)
            def _():
                wait_slot(s)

    _cast_blocks(cast_in, cast_out)

    @pl.when(i == last)
    def _():
        for e in range(ne):
            lo, hi = grp_ref[e] + grp_ref[ne + e], grp_ref[e] + grp_ref[2 * ne + e]
            mid = jnp.minimum((lo + 7) // 8 * 8, hi)

            def pad_row(r):
                return pltpu.make_async_copy(stage.at[0, pl.ds(0, 1)], xs_ref.at[pl.ds(r, 1)], sem.at[0])

            def pad_rows8(q):
                r = pl.multiple_of(mid + q * 8, 8)
                return pltpu.make_async_copy(stage.at[0, pl.ds(0, 8)], xs_ref.at[pl.ds(r, 8)], sem.at[1])

            lax.fori_loop(lo, mid, lambda r, c: (pad_row(r).start(), c)[1], 0)
            lax.fori_loop(0, (hi - mid) // 8, lambda q, c: (pad_rows8(q).start(), c)[1], 0)
            lax.fori_loop(lo, mid, lambda r, c: (pad_row(r).wait(), c)[1], 0)
            lax.fori_loop(0, (hi - mid) // 8, lambda q, c: (pad_rows8(q).wait(), c)[1], 0)

        def tile_copy(j):
            return pltpu.make_async_copy(stage.at[0, pl.ds(0, te)], xs_ref.at[pl.ds(j * te, te)], sem.at[0])

        lax.fori_loop(grp_ref[3 * ne], n_tiles, lambda j, c: (tile_copy(j).start(), c)[1], 0)
        lax.fori_loop(grp_ref[3 * ne], n_tiles, lambda j, c: (tile_copy(j).wait(), c)[1], 0)


def _dispatch(x, pos, grp, n_tiles, tm, te, to_cast):
    t, d = x.shape
    ne = (grp.shape[0] - 1) // 3
    assert te <= tm
    c_in, c_out, c_shapes = _cast_plan(to_cast, t // tm)
    res = pl.pallas_call(
        functools.partial(_dispatch_kernel, tm=tm, te=te, ne=ne, n_tiles=n_tiles, n_cast=len(to_cast)),
        grid=(t // tm,),
        in_specs=[
            pl.BlockSpec((1, 1, 2 * tm), lambda i: (i, 0, 0), memory_space=pltpu.SMEM),
            pl.BlockSpec(memory_space=pltpu.SMEM),
            pl.BlockSpec((tm, d), lambda i: (i, 0)),
        ] + c_in,
        out_specs=[pl.BlockSpec(memory_space=pl.ANY)] + c_out,
        out_shape=[jax.ShapeDtypeStruct((n_tiles * te, 8, d // 8), F32)] + c_shapes,
        scratch_shapes=[pltpu.VMEM((2, tm, 8, d // 8), F32), pltpu.SemaphoreType.DMA((2,))],
        compiler_params=pltpu.CompilerParams(
            dimension_semantics=("arbitrary",), has_side_effects=True, vmem_limit_bytes=VMEM_LIMIT),
        name="moe_dispatch",
    )(pos.reshape(t // tm, 1, 2 * tm), grp, x, *to_cast)
    return res[0], res[1:]


def _experts_kernel(te_ref, nv_ref, xs_ref, wg_ref, wu_ref, wd_ref, ys_ref):
    i = pl.program_id(0)

    @pl.when(i < nv_ref[0])
    def _():
        tm = xs_ref.shape[0]
        xb = xs_ref[...].reshape(tm, wg_ref.shape[1]).astype(BF16)
        h = _silu(_mm(xb, wg_ref[0])) * _mm(xb, wu_ref[0])
        ys_ref[...] = _mm(h.astype(BF16), wd_ref[0]).reshape(ys_ref.shape)

    @pl.when(i >= nv_ref[0])
    def _():
        ys_ref[...] = jnp.zeros(ys_ref.shape, F32)


def _experts(xs, tile_expert, n_valid, wg, wu, wd, tm):
    n_rows, d = xs.shape[0], xs.shape[1] * xs.shape[2]
    ne, _, ff = wg.shape
    grid_spec = pltpu.PrefetchScalarGridSpec(
        num_scalar_prefetch=2,
        grid=(n_rows // tm,),
        in_specs=[
            pl.BlockSpec((tm, 8, d // 8), lambda i, te, nv: (i, 0, 0)),
            pl.BlockSpec((1, d, ff), lambda i, te, nv: (te[i], 0, 0)),
            pl.BlockSpec((1, d, ff), lambda i, te, nv: (te[i], 0, 0)),
            pl.BlockSpec((1, ff, d), lambda i, te, nv: (te[i], 0, 0)),
        ],
        out_specs=pl.BlockSpec((tm, 8, d // 8), lambda i, te, nv: (i, 0, 0)),
    )
    return pl.pallas_call(
        _experts_kernel,
        grid_spec=grid_spec,
        out_shape=jax.ShapeDtypeStruct((n_rows, 8, d // 8), F32),
        compiler_params=pltpu.CompilerParams(
            dimension_semantics=("arbitrary",), vmem_limit_bytes=VMEM_LIMIT),
        name="moe_experts",
    )(tile_expert, n_valid, xs, wg, wu, wd)


def _combine_kernel(pos_ref, pos_next_ref, x_ref, rt_ref, ys_ref, g_ref, b_ref, *rest, tm, alpha, npt):
    o_refs, (buf, sem) = rest[:-2], rest[-2:]
    i = pl.program_id(0)
    n = pl.num_programs(0)

    def gather(p_ref, s):
        def row_copy(r, k):
            return pltpu.make_async_copy(
                ys_ref.at[p_ref[0, 0, 2 * r + k]], buf.at[s, k, r], sem.at[s])

        def start(g, carry):
            for u in range(ROW_DMA_UNROLL):
                row_copy(g * ROW_DMA_UNROLL + u, 0).start(priority=0)
                row_copy(g * ROW_DMA_UNROLL + u, 1).start(priority=1)
            return carry

        lax.fori_loop(0, tm // ROW_DMA_UNROLL, start, 0)

    @pl.when(i == 0)
    def _():
        gather(pos_ref, 0)

    for s in range(2):
        @pl.when(i % 2 == s)
        def _(s=s):
            @pl.when(i + 1 < n)
            def _():
                gather(pos_next_ref, 1 - s)

            for k in range(TOP_K):
                pltpu.make_async_copy(ys_ref.at[pl.ds(0, tm)], buf.at[s, k], sem.at[s]).wait()
            rt = rt_ref[...]
            d = x_ref.shape[1]
            f = rt[:, 2:3] * buf[s, 0].reshape(tm, d) + rt[:, 3:4] * buf[s, 1].reshape(tm, d)
            _token_store(o_refs, _layer_norm(alpha * x_ref[...] + f, g_ref[...], b_ref[...]), npt)


def _combine(x, pos, route_t, ys, g, b, alpha, tm, split):
    t, d = x.shape
    out_specs, out_shape = _token_out(t, d, tm, split)
    npt = None if split is None else split[0] // tm
    n_steps = t // tm
    pos3 = pos.reshape(n_steps, 1, 2 * tm)
    return pl.pallas_call(
        functools.partial(_combine_kernel, tm=tm, alpha=alpha, npt=npt),
        grid=(t // tm,),
        in_specs=[
            pl.BlockSpec((1, 1, 2 * tm), lambda i: (i, 0, 0), memory_space=pltpu.SMEM),
            pl.BlockSpec((1, 1, 2 * tm), lambda i: (jnp.minimum(i + 1, n_steps - 1), 0, 0), memory_space=pltpu.SMEM),
            pl.BlockSpec((tm, d), lambda i: (i, 0)),
            pl.BlockSpec((tm, 8), lambda i: (i, 0)),
            pl.BlockSpec(memory_space=pl.ANY),
            _full_spec((1, d)), _full_spec((1, d)),
        ],
        out_specs=out_specs,
        out_shape=out_shape,
        scratch_shapes=[pltpu.VMEM((2, TOP_K, tm, 8, d // 8), F32), pltpu.SemaphoreType.DMA((2,))],
        compiler_params=pltpu.CompilerParams(dimension_semantics=("arbitrary",), vmem_limit_bytes=VMEM_LIMIT),
        name="moe_combine",
    )(pos3, pos3, x, route_t, ys, g.reshape(1, d), b.reshape(1, d))


def _ffn_moe(x, w_router, wg, wu, wd, g, b, alpha, tr, tm, split, pre):
    t, d = x.shape
    ne = w_router.shape[1]
    ff = wg.shape[2]
    flat = {"wg": wg.reshape(ne * d, ff), "wu": wu.reshape(ne * d, ff), "wd": wd.reshape(ne * ff, d)}
    conv = dict(pre)
    r_keys = [k for k in ("wd",) if k not in conv]
    d_keys = [k for k in ("wg", "wu") if k not in conv]
    route, counts, r_out = _router(x, w_router, tr, [flat[k] for k in r_keys])
    conv.update(zip(r_keys, r_out))
    cnt = counts[:, 0].astype(jnp.int32)
    gsz = ((cnt + tm - 1) // tm) * tm
    ends = jnp.cumsum(gsz)
    offs = ends - gsz
    n_tiles = (TOP_K * t) // tm + ne
    tile_start = jnp.arange(n_tiles, dtype=jnp.int32) * tm
    tile_e = jnp.sum((tile_start[:, None] >= ends[None, :]).astype(jnp.int32), axis=1)
    n_valid = (ends[ne - 1] // tm).astype(jnp.int32).reshape(1)
    last_e = jnp.sum((ends[ne - 1] - 1 >= ends).astype(jnp.int32))
    tile_e = jnp.minimum(tile_e, last_e).astype(jnp.int32)
    i12 = route[0:2].astype(jnp.int32)
    base = sum(jnp.where(i12 == e, offs[e], 0) for e in range(ne))
    pos = (base + route[4:6].astype(jnp.int32)).T.reshape(-1)
    grp = jnp.concatenate([offs, cnt, gsz, n_valid]).astype(jnp.int32)
    xs, d_out = _dispatch(x, pos, grp, n_tiles, tr, tm, [flat[k] for k in d_keys])
    conv.update(zip(d_keys, d_out))
    ys = _experts(xs, tile_e, n_valid, conv["wg"].reshape(ne, d, ff), conv["wu"].reshape(ne, d, ff),
                  conv["wd"].reshape(ne, ff, d), tm)
    return _combine(x, pos, route.T, ys, g, b, alpha, tr, split)


def kernel(x_prompt, x_sample, cache_conv, state_gla, w_in, b_in, conv_w, conv_b, conv_ln_g, conv_ln_b, w_conv_out, b_conv_out, w_gate_up, b_gate, gla_norm_g, w_gla_out, w_o, ln1_g, ln1_b, ln2_g, ln2_b, ff_w_gate, ff_w_up, ff_w_down, w_router, moe_w_gate, moe_w_up, moe_w_down):
    p = dict(w_in=w_in, b_in=b_in, conv_w=conv_w, conv_b=conv_b, conv_ln_g=conv_ln_g, conv_ln_b=conv_ln_b,
             w_conv_out=w_conv_out, b_conv_out=b_conv_out, w_gate_up=w_gate_up, b_gate=b_gate,
             gla_norm_g=gla_norm_g, w_gla_out=w_gla_out, w_o=w_o, ln1_g=ln1_g, ln1_b=ln1_b)
    depth = w_in.shape[0]
    bsz, seq, d = x_prompt.shape
    ns, ls, _ = x_sample.shape
    dc = conv_w.shape[-1]
    rank, dk = w_gate_up.shape[1], w_gate_up.shape[2]
    dv = w_gla_out.shape[1]
    dims = dict(d=d, dc=dc, dk=dk, dv=dv, rank=rank, hk=dk // GLA_HEADS, hv=dv // GLA_HEADS,
                alpha=(2.0 * depth) ** 0.25)
    alpha = dims["alpha"]
    tl = min(512, seq)
    tp = bsz * seq
    ts = ns * ls
    tm = min(512, ts)
    assert seq % tl == 0 and tp % ts == 0 and ts % tm == 0 and ts % tl == 0 and ns % SEQ_PER_STEP == 0

    x_p, x_s, s_blk = x_prompt.reshape(tp, d), x_sample.reshape(ts, d), 0
    hist_p, state_p, hist_s, state_s = [], [], [], []
    w_in_b = w_in.astype(BF16)
    for l in range(depth):
        wts = _mixer_weights(l, p, dims, w_in_b)
        nxt = {}
        if l % 2 == 0 and l + 1 < depth:
            nxt = {k: m[l // 2].reshape(-1, m.shape[-1])
                   for k, m in (("wg", moe_w_gate), ("wu", moe_w_up), ("wd", moe_w_down))}
        x1s, hs, ss = _mixer_sample(x_s, s_blk, cache_conv, state_gla, l, wts, dims, ns, ls)
        x1, hp, sp, *done_mix = _mixer_prompt(x_p, x1s, wts, dims, tl, bsz, seq, [nxt["wu"]] if nxt else [])
        hist_p.append(hp), state_p.append(sp), hist_s.append(hs), state_s.append(ss)
        split = (tp, ts) if l == depth - 1 else None
        if l % 2 == 0:
            x2, done = _ffn_dense(x1, ff_w_gate[l // 2], ff_w_up[l // 2], ff_w_down[l // 2], ln2_g[l], ln2_b[l],
                                  alpha, tm, split, [nxt["wg"], nxt["wd"]] if nxt else [])
            pre_cast = dict(zip(("wg", "wd", "wu"), list(done) + done_mix)) if nxt else {}
        else:
            x2 = _ffn_moe(x1, w_router[l // 2], moe_w_gate[l // 2], moe_w_up[l // 2], moe_w_down[l // 2],
                          ln2_g[l], ln2_b[l], alpha, tm, min(EXPERT_TILE, tm), split, pre_cast)
        if split is None:
            x_p, x_s, s_blk = x2[0], x2[0], tp // ts
    y_p, y_s = x2
    return (y_p.reshape(bsz, seq, d), y_s.reshape(ns, ls, d), jnp.stack(hist_p),
            jnp.stack(state_p).astype(state_gla.dtype), jnp.stack(hist_s), jnp.stack(state_s).astype(state_gla.dtype))
```

```python
import functools

import jax
import jax.numpy as jnp
from jax import lax
from jax.experimental import pallas as pl
from jax.experimental.pallas import tpu as pltpu

CHUNK = 64
CONV_W = 31
GLA_HEADS = 4
GATE_TAU = 16.0
LN_EPS = 1e-5
RMS_EPS = 1e-6
TOP_K = 2

HIST_PAD = 32
CONV_ROWS = 32
ROW_DMA_UNROLL = 8
SEQ_PER_STEP = 4
EXPERT_TILE = 512
MXU_TILE = 256
FF_CHUNKS = 11
VMEM_LIMIT = 56 * 1024 * 1024

BF16 = jnp.bfloat16
F32 = jnp.float32


def _mm(a, b):
    return jnp.dot(a, b, preferred_element_type=F32)


def _sigmoid(x):
    return 0.5 * jnp.tanh(0.5 * x) + 0.5


def _silu(x):
    return x * _sigmoid(x)


def _log_sigmoid(z):
    return -(jnp.maximum(-z, 0.0) + jnp.log(1.0 + jnp.exp(-jnp.abs(z))))


def _layer_norm(x, g, b):
    mu = jnp.mean(x, axis=-1, keepdims=True)
    xc = x - mu
    var = jnp.mean(xc * xc, axis=-1, keepdims=True)
    return xc * lax.rsqrt(var + LN_EPS) * g + b


def _split3(x):
    hi = x.astype(BF16)
    r1 = x - hi.astype(F32)
    mid = r1.astype(BF16)
    lo = (r1 - mid.astype(F32)).astype(BF16)
    return hi, mid, lo


def _full_spec(shape):
    zeros = (0,) * len(shape)
    return pl.BlockSpec(shape, lambda *_: zeros, pipeline_mode=pl.Buffered(1))


def _project(xb, w, dims, u_store, q_s, k_s, v_s, lf_s):
    dc, dk, dv = dims["dc"], dims["dk"], dims["dv"]
    off_q, off_k, off_v = 2 * dc, 2 * dc + dk, 2 * dc + 2 * dk
    off_g = off_v + dv
    glu = _mm(xb, w["wmain"][:, 0:off_q]) + w["bmain"][:, 0:off_q]
    u_store(glu[:, 0:dc] * _sigmoid(glu[:, dc:off_q]))
    q_s[...] = (_mm(xb, w["wmain"][:, off_q:off_k]) + w["bmain"][:, off_q:off_k]) * (dims["hk"] ** -0.5)
    k_s[...] = _mm(xb, w["wmain"][:, off_k:off_v]) + w["bmain"][:, off_k:off_v]
    v_s[...] = _mm(xb, w["wmain"][:, off_v:off_g]) + w["bmain"][:, off_v:off_g]
    lr = _mm(xb, w["wlr"][:, 0:dims["rank"]]) + w["blr"][...]
    z = _mm(lr.astype(BF16), w["wgu"][...]) + w["bgu"][...]
    lf_s[...] = _log_sigmoid(z) * (1.0 / GATE_TAU)


def _gla_chunk(q_s, k_s, v_s, lf_s, o_s, rows, c, get_state, set_state, dims):
    hk, hv, dk = dims["hk"], dims["hv"], dims["dk"]
    lf = lf_s[rows, :]
    ri = lax.broadcasted_iota(jnp.int32, (c, c), 0)
    ci = lax.broadcasted_iota(jnp.int32, (c, c), 1)
    causal = ri >= ci
    tri = causal.astype(BF16)
    hi, mid, lo = _split3(lf)
    b = _mm(tri, hi) + _mm(tri, mid) + _mm(tri, lo)
    b_last = b[c - 1:c, :]
    q_t = q_s[rows, :] * jnp.exp(b)
    kk = k_s[rows, :]
    k_t = kk * jnp.exp(-b)
    k_e = kk * jnp.exp(b_last - b)
    d_t = jnp.transpose(jnp.broadcast_to(jnp.exp(b_last), (128, dk)))
    vv = v_s[rows, :]
    for h in range(GLA_HEADS):
        ks = slice(h * hk, (h + 1) * hk)
        vs = slice(h * hv, (h + 1) * hv)
        qh = q_t[:, ks].astype(BF16)
        kh = k_t[:, ks].astype(BF16)
        keh = k_e[:, ks].astype(BF16)
        vh = vv[:, vs].astype(BF16)
        att = lax.dot_general(qh, kh, (((1,), (1,)), ((), ())), preferred_element_type=F32)
        att = jnp.where(causal, att, 0.0)
        s_old = get_state(h)
        o_s[rows, vs] = _mm(qh, s_old.astype(BF16)) + _mm(att.astype(BF16), vh)
        upd = lax.dot_general(keh, vh, (((0,), (0,)), ((), ())), preferred_element_type=F32)
        dcol = d_t[ks, :]
        decay = jnp.concatenate([dcol] * (hv // 128), axis=1)
        set_state(h, decay * s_old + upd)


def _conv_block(win, w8_ref, n):
    wn, ch = win.shape
    base = HIST_PAD - (CONV_W - 1)
    acc = None
    for b in range(8):
        wb = win if b == 0 else pltpu.roll(win, wn - b, axis=0)
        for a in range((base + CONV_W + 7) // 8):
            j = 8 * a + b - base
            if 0 <= j < CONV_W:
                term = w8_ref[8 * j:8 * j + 8, :][None] * wb[8 * a:8 * a + n, :].reshape(n // 8, 8, ch)
                acc = term if acc is None else acc + term
    return acc.reshape(n, ch)


def _tail(x, xb, c, w, dims, o_s):
    d, dc, dk, dv, hv = dims["d"], dims["dc"], dims["dk"], dims["dv"], dims["hv"]
    off_g = 2 * dc + 2 * dk + dv
    off_lr = off_g + dv
    c = _silu(_layer_norm(c + w["convb"][...], w["clng"][...], w["clnb"][...]))
    y_a = _mm(c.astype(BF16), w["wco"][...]) + w["bco"][...]
    g_out = _mm(xb, w["wmain"][:, off_g:off_lr]) + w["bmain"][:, off_g:off_lr]
    heads = []
    for h in range(GLA_HEADS):
        oh = o_s[:, h * hv:(h + 1) * hv]
        ms = jnp.mean(oh * oh, axis=-1, keepdims=True)
        heads.append(oh * lax.rsqrt(ms + RMS_EPS) * w["gng"][...])
    o = jnp.concatenate(heads, axis=1) * _silu(g_out)
    y_b = _mm(o.astype(BF16), w["wgo"][...])
    gates = _sigmoid(_mm(xb, w["wgates"][...]) + w["bgates"][...])
    merged = gates[:, 0:d] * y_a + gates[:, d:2 * d] * y_b
    m = _mm(merged.astype(BF16), w["wo"][...])
    return _layer_norm(dims["alpha"] * x + m, w["ln1g"][...], w["ln1b"][...])


_W_NAMES = ("wmain", "bmain", "wlr", "blr", "wgates", "bgates", "convw", "convb", "clng", "clnb",
            "wco", "bco", "wgu", "bgu", "gng", "wgo", "wo", "ln1g", "ln1b")


def _weight_refs(refs):
    w = dict(zip(_W_NAMES, refs))
    w["wmain"], w["wlr"] = w["wmain"].at[0], w["wlr"].at[0]
    return w


def _mixer_prompt_kernel(*refs, dims, tl, nt, n_prompt, n_cast):
    nw = len(_W_NAMES)
    x_ref, tail_ref = refs[0:2]
    w = _weight_refs(refs[2:2 + nw])
    cast_in = refs[2 + nw:2 + nw + n_cast]
    x1_ref, hist_ref, state_ref = refs[2 + nw + n_cast:5 + nw + n_cast]
    cast_out = refs[5 + nw + n_cast:5 + nw + 2 * n_cast]
    scratch = refs[5 + nw + 2 * n_cast:]
    i = pl.program_id(0)
    _cast_blocks(cast_in, cast_out)

    @pl.when(i < n_prompt)
    def _():
        _mixer_prompt_tile(i % nt, x_ref, w, x1_ref, hist_ref, state_ref, scratch, dims, tl, nt)

    @pl.when(i >= n_prompt)
    def _():
        x1_ref[...] = tail_ref[...]


def _mixer_prompt_tile(j, x_ref, w, x1_ref, hist_ref, state_ref, scratch, dims, tl, nt):
    ubuf, qt_s, kt_s, ke_s, vb_s, g_s, gt_s, o_s, c_s, s_s, dt_s = scratch
    d, dc, dk, dv, hk, hv = dims["d"], dims["dc"], dims["dk"], dims["dv"], dims["hk"], dims["hv"]
    off_q, off_k, off_v = 2 * dc, 2 * dc + dk, 2 * dc + 2 * dk
    off_g = off_v + dv
    off_lr = off_g + dv
    nch = tl // CHUNK
    rep = 128 // nch
    wm, bm = w["wmain"], w["bmain"]

    @pl.when(j == 0)
    def _():
        ubuf[0:HIST_PAD, :] = jnp.zeros((HIST_PAD, dc), F32)
        s_s[...] = jnp.zeros(s_s.shape, F32)

    x = x_ref[...]
    xb = x.astype(BF16)

    glu = _mm(xb, wm[:, 0:off_q]) + bm[:, 0:off_q]
    ubuf[HIST_PAD:HIST_PAD + tl, :] = glu[:, 0:dc] * _sigmoid(glu[:, dc:off_q])
    lr = _mm(xb, w["wlr"][:, 0:dims["rank"]]) + w["blr"][...]
    z = _mm(lr.astype(BF16), w["wgu"][...]) + w["bgu"][...]
    lf = _log_sigmoid(z) * (1.0 / GATE_TAU)
    ri = lax.broadcasted_iota(jnp.int32, (CHUNK, CHUNK), 0)
    ci = lax.broadcasted_iota(jnp.int32, (CHUNK, CHUNK), 1)
    causal = ri >= ci
    tri = causal.astype(BF16)
    parts = _split3(lf)
    b = jnp.concatenate(
        [sum(_mm(tri, p[c * CHUNK:(c + 1) * CHUNK, :]) for p in parts) for c in range(nch)], axis=0)
    b3 = b.reshape(nch, CHUNK, dk)
    b_last = b3[:, CHUNK - 1:CHUNK, :]
    q = (_mm(xb, wm[:, off_q:off_k]) + bm[:, off_q:off_k]) * (hk ** -0.5)
    qt_s[...] = (q * jnp.exp(b)).astype(BF16)
    kk = _mm(xb, wm[:, off_k:off_v]) + bm[:, off_k:off_v]
    kt_s[...] = (kk * jnp.exp(-b)).astype(BF16)
    ke_s[...] = (kk.reshape(nch, CHUNK, dk) * jnp.exp(b_last - b3)).reshape(tl, dk).astype(BF16)
    vb_s[...] = (_mm(xb, wm[:, off_v:off_g]) + bm[:, off_v:off_g]).astype(BF16)
    d_rows = jnp.broadcast_to(jnp.exp(b_last), (nch, rep, dk)).reshape(nch * rep, dk)
    if nch * rep < 128:
        d_rows = jnp.concatenate([d_rows, jnp.zeros((128 - nch * rep, dk), F32)], axis=0)
    dt_s[...] = jnp.transpose(d_rows)
    g_s[...] = _silu(_mm(xb, wm[:, off_g:off_lr]) + bm[:, off_g:off_lr])

    states = [s_s[h] for h in range(GLA_HEADS)]
    conv_per = tl // CONV_ROWS // nch
    gw = 2 * d // nch
    for c in range(nch):
        rows = slice(c * CHUNK, (c + 1) * CHUNK)
        for h in range(GLA_HEADS):
            ks = slice(h * hk, (h + 1) * hk)
            vs = slice(h * hv, (h + 1) * hv)
            qh, kh, keh, vh = qt_s[rows, ks], kt_s[rows, ks], ke_s[rows, ks], vb_s[rows, vs]
            att = lax.dot_general(qh, kh, (((1,), (1,)), ((), ())), preferred_element_type=F32)
            att = jnp.where(causal, att, 0.0)
            s_old = states[h]
            o_s[rows, vs] = _mm(qh, s_old.astype(BF16)) + _mm(att.astype(BF16), vh)
            upd = lax.dot_general(keh, vh, (((0,), (0,)), ((), ())), preferred_element_type=F32)
            decay = jnp.broadcast_to(dt_s[ks, c * rep:c * rep + 1], (hk, hv))
            states[h] = decay * s_old + upd
        for bi in range(c * conv_per, (c + 1) * conv_per):
            r0 = bi * CONV_ROWS
            c_s[r0:r0 + CONV_ROWS, :] = _conv_block(
                ubuf[r0:r0 + CONV_ROWS + HIST_PAD, :], w["convw"], CONV_ROWS)
        cols = slice(c * gw, (c + 1) * gw)
        gt_s[:, cols] = _sigmoid(_mm(xb, w["wgates"][:, cols]) + w["bgates"][:, cols])
    for h in range(GLA_HEADS):
        s_s[h] = states[h]

    cv = _silu(_layer_norm(c_s[...] + w["convb"][...], w["clng"][...], w["clnb"][...]))
    y_a = _mm(cv.astype(BF16), w["wco"][...]) + w["bco"][...]
    heads = []
    for h in range(GLA_HEADS):
        oh = o_s[:, h * hv:(h + 1) * hv]
        ms = jnp.mean(oh * oh, axis=-1, keepdims=True)
        heads.append(oh * lax.rsqrt(ms + RMS_EPS) * w["gng"][...])
    o = jnp.concatenate(heads, axis=1) * g_s[...]
    y_b = _mm(o.astype(BF16), w["wgo"][...])
    merged = gt_s[:, 0:d] * y_a + gt_s[:, d:2 * d] * y_b
    m = _mm(merged.astype(BF16), w["wo"][...])
    x1_ref[...] = _layer_norm(dims["alpha"] * x + m, w["ln1g"][...], w["ln1b"][...])

    @pl.when(j == nt - 1)
    def _():
        hist_ref[0] = ubuf[tl + HIST_PAD - (CONV_W - 1):tl + HIST_PAD, :]
        state_ref[0] = s_s[...]

    ubuf[0:HIST_PAD, :] = ubuf[tl:tl + HIST_PAD, :]


def _mixer_sample_kernel(*refs, dims, ns, ls):
    nw = len(_W_NAMES)
    x_ref, hist_in_ref, state_in_ref = refs[0:3]
    w = _weight_refs(refs[3:3 + nw])
    x1_ref, hist_ref, state_ref = refs[3 + nw:6 + nw]
    ubuf, q_s, k_s, v_s, lf_s, o_s, c_s = refs[6 + nw:]
    i = pl.program_id(0)
    dc = dims["dc"]
    hl = CONV_W - 1

    @pl.when(i == 0)
    def _():
        xb = x_ref[...].astype(BF16)
        ubuf[:, 0:8, :] = jnp.zeros((ns, 8, dc), F32)
        ubuf[:, HIST_PAD - hl:HIST_PAD, :] = hist_in_ref[0]

        def u_store(u):
            ubuf[:, HIST_PAD:HIST_PAD + ls, :] = u.reshape(ns, ls, dc)

        _project(xb, w, dims, u_store, q_s, k_s, v_s, lf_s)

    for s in range(SEQ_PER_STEP):
        seq = i * SEQ_PER_STEP + s
        rows = pl.ds(pl.multiple_of(seq * ls, ls), ls)

        def set_state(h, val, s=s):
            state_ref[s, h] = val

        _gla_chunk(q_s, k_s, v_s, lf_s, o_s, rows, ls, lambda h, s=s: state_in_ref[0, s, h], set_state, dims)
        win = ubuf[seq]
        c_s[rows, :] = _conv_block(win, w["convw"], ls)
        hist_ref[seq] = win[HIST_PAD + ls - hl:HIST_PAD + ls, :]

    @pl.when(i == ns // SEQ_PER_STEP - 1)
    def _():
        x = x_ref[...]
        x1_ref[...] = _tail(x, x.astype(BF16), c_s[...], w, dims, o_s)


def _mixer_weights(l, p, dims, w_in_b):
    dc, dk, dv, rank = dims["dc"], dims["dk"], dims["dv"], dims["rank"]
    off_lr = 2 * dc + 2 * dk + 2 * dv
    off_gates = off_lr + rank
    assert off_lr % 128 == 0 and rank <= 128
    b_in = p["b_in"][l]
    d = w_in_b.shape[1]
    row = lambda v: v.reshape(1, -1).astype(F32)
    ops = (
        w_in_b, row(b_in[:off_lr]),
        w_in_b, row(b_in[off_lr:off_gates]),
        w_in_b[l][:, off_gates:], row(b_in[off_gates:]),
        jnp.repeat(p["conv_w"][l].astype(F32), 8, axis=0),
        row(p["conv_b"][l]), row(p["conv_ln_g"][l]), row(p["conv_ln_b"][l]),
        p["w_conv_out"][l].astype(BF16), row(p["b_conv_out"][l]),
        p["w_gate_up"][l].astype(BF16), row(p["b_gate"][l]),
        row(p["gla_norm_g"][l]),
        p["w_gla_out"][l].astype(BF16), p["w_o"][l].astype(BF16),
        row(p["ln1_g"][l]), row(p["ln1_b"][l]),
    )
    specs = [_full_spec(a.shape) for a in ops]
    specs[0] = pl.BlockSpec((1, d, off_lr), lambda *_: (l, 0, 0), pipeline_mode=pl.Buffered(1))
    specs[2] = pl.BlockSpec((1, d, 128), lambda *_: (l, 0, off_lr // 128), pipeline_mode=pl.Buffered(1))
    return ops, specs


def _mixer_prompt(x, x1_tail, wts, dims, tl, bsz, seq, to_cast):
    d = x.shape[1]
    dc, dk, dv, hk, hv = dims["dc"], dims["dk"], dims["dv"], dims["hk"], dims["hv"]
    hl = CONV_W - 1
    nt = seq // tl
    n_prompt = bsz * nt
    n_tail = x1_tail.shape[0] // tl
    kern = functools.partial(_mixer_prompt_kernel, dims=dims, tl=tl, nt=nt, n_prompt=n_prompt,
                             n_cast=len(to_cast))
    seq_of = lambda i: jnp.minimum(i // nt, bsz - 1)
    c_in, c_out, c_shapes = _cast_plan(to_cast, n_prompt + n_tail) if to_cast else ([], [], [])
    return pl.pallas_call(
        kern,
        grid=(n_prompt + n_tail,),
        in_specs=[pl.BlockSpec((tl, d), lambda i: (jnp.minimum(i, n_prompt - 1), 0)),
                  pl.BlockSpec((tl, d), lambda i: (jnp.maximum(i - n_prompt, 0), 0))]
        + wts[1] + c_in,
        out_specs=[
            pl.BlockSpec((tl, d), lambda i: (i, 0)),
            pl.BlockSpec((1, hl, dc), lambda i: (seq_of(i), 0, 0)),
            pl.BlockSpec((1, GLA_HEADS, hk, hv), lambda i: (seq_of(i), 0, 0, 0)),
        ] + c_out,
        out_shape=[
            jax.ShapeDtypeStruct(((n_prompt + n_tail) * tl, d), F32),
            jax.ShapeDtypeStruct((bsz, hl, dc), F32),
            jax.ShapeDtypeStruct((bsz, GLA_HEADS, hk, hv), F32),
        ] + c_shapes,
        scratch_shapes=[
            pltpu.VMEM((HIST_PAD + tl, dc), F32),
            pltpu.VMEM((tl, dk), BF16), pltpu.VMEM((tl, dk), BF16), pltpu.VMEM((tl, dk), BF16),
            pltpu.VMEM((tl, dv), BF16),
            pltpu.VMEM((tl, dv), F32), pltpu.VMEM((tl, 2 * d), F32),
            pltpu.VMEM((tl, dv), F32), pltpu.VMEM((tl, dc), F32),
            pltpu.VMEM((GLA_HEADS, hk, hv), F32), pltpu.VMEM((dk, 128), F32),
        ],
        compiler_params=pltpu.CompilerParams(
            dimension_semantics=("arbitrary",), vmem_limit_bytes=VMEM_LIMIT),
        name="mixer_prompt",
    )(x, x1_tail, *wts[0], *to_cast)


def _mixer_sample(x, in_blk, hist, state, layer, wts, dims, ns, ls):
    d = x.shape[1]
    dc, dk, dv, hk, hv = dims["dc"], dims["dk"], dims["dv"], dims["hk"], dims["hv"]
    hl = CONV_W - 1
    t = ns * ls
    kern = functools.partial(_mixer_sample_kernel, dims=dims, ns=ns, ls=ls)
    x1, hist_o, state_o = pl.pallas_call(
        kern,
        grid=(ns // SEQ_PER_STEP,),
        in_specs=[
            pl.BlockSpec((t, d), lambda i: (in_blk, 0)),
            pl.BlockSpec((1, ns, hl, dc), lambda i: (layer, 0, 0, 0)),
            pl.BlockSpec((1, SEQ_PER_STEP, GLA_HEADS, hk, hv), lambda i: (layer, i, 0, 0, 0)),
        ] + wts[1],
        out_specs=[
            pl.BlockSpec((t, d), lambda i: (0, 0)),
            pl.BlockSpec((ns, hl, dc), lambda i: (0, 0, 0)),
            pl.BlockSpec((SEQ_PER_STEP, GLA_HEADS, hk, hv), lambda i: (i, 0, 0, 0)),
        ],
        out_shape=[
            jax.ShapeDtypeStruct((t, d), F32),
            jax.ShapeDtypeStruct((ns, hl, dc), F32),
            jax.ShapeDtypeStruct((ns, GLA_HEADS, hk, hv), F32),
        ],
        scratch_shapes=[
            pltpu.VMEM((ns, HIST_PAD + ls, dc), F32),
            pltpu.VMEM((t, dk), F32), pltpu.VMEM((t, dk), F32), pltpu.VMEM((t, dv), F32),
            pltpu.VMEM((t, dk), F32), pltpu.VMEM((t, dv), F32), pltpu.VMEM((t, dc), F32),
        ],
        compiler_params=pltpu.CompilerParams(
            dimension_semantics=("arbitrary",), vmem_limit_bytes=VMEM_LIMIT),
        name="mixer_sample",
    )(x, hist, state, *wts[0])
    return x1, hist_o, state_o


def _swiglu(xb, wg_ref, wu_ref, wd_ref):
    ff = wg_ref.shape[-1]
    n_tiles = -(-ff // MXU_TILE)
    edges = [min(ff, (n_tiles * c // FF_CHUNKS) * MXU_TILE) for c in range(FF_CHUNKS)] + [ff]
    y = None
    for a, b in zip(edges[:-1], edges[1:]):
        h = _silu(_mm(xb, wg_ref[:, a:b])) * _mm(xb, wu_ref[:, a:b])
        part = _mm(h.astype(BF16), wd_ref[a:b, :])
        y = part if y is None else y + part
    return y


def _token_out(t, d, tm, split):
    if split is None:
        return [pl.BlockSpec((tm, d), lambda i: (i, 0))], [jax.ShapeDtypeStruct((t, d), F32)]
    tp, ts = split
    npt = tp // tm
    specs = [pl.BlockSpec((tm, d), lambda i: (jnp.minimum(i, npt - 1), 0)),
             pl.BlockSpec((tm, d), lambda i: (jnp.maximum(i - npt, 0), 0))]
    return specs, [jax.ShapeDtypeStruct((tp, d), F32), jax.ShapeDtypeStruct((ts, d), F32)]


def _token_store(o_refs, val, npt):
    if len(o_refs) == 1:
        o_refs[0][...] = val
        return
    i = pl.program_id(0)

    @pl.when(i < npt)
    def _():
        o_refs[0][...] = val

    @pl.when(i >= npt)
    def _():
        o_refs[1][...] = val


def _ffn_dense_kernel(x_ref, wg_ref, wu_ref, wd_ref, g_ref, b_ref, *rest, alpha, npt, n_cast):
    cast_in, o_refs, cast_out = rest[:n_cast], rest[n_cast:len(rest) - n_cast], rest[len(rest) - n_cast:]
    _cast_blocks(cast_in, cast_out)
    x = x_ref[...]
    xb = x.astype(BF16)
    f = _swiglu(xb, wg_ref, wu_ref, wd_ref)
    _token_store(o_refs, _layer_norm(alpha * x + f, g_ref[...], b_ref[...]), npt)


def _ffn_dense(x, wg, wu, wd, g, b, alpha, tm, split, to_cast):
    t, d = x.shape
    ops = (wg.astype(BF16), wu.astype(BF16), wd.astype(BF16), g.reshape(1, d), b.reshape(1, d))
    out_specs, out_shape = _token_out(t, d, tm, split)
    npt = None if split is None else split[0] // tm
    c_in, c_out, c_shapes = _cast_plan(to_cast, t // tm) if to_cast else ([], [], [])
    res = pl.pallas_call(
        functools.partial(_ffn_dense_kernel, alpha=alpha, npt=npt, n_cast=len(to_cast)),
        grid=(t // tm,),
        in_specs=[pl.BlockSpec((tm, d), lambda i: (i, 0))] + [_full_spec(a.shape) for a in ops] + c_in,
        out_specs=out_specs + c_out,
        out_shape=out_shape + c_shapes,
        compiler_params=pltpu.CompilerParams(
            dimension_semantics=("arbitrary",), vmem_limit_bytes=VMEM_LIMIT),
        name="ffn_dense",
    )(x, *ops, *to_cast)
    return res[:len(out_shape)], res[len(out_shape):]


def _cast_plan(arrays, n_steps):
    n_blocks = max(n for n in range(1, n_steps + 1)
                   if all(a.shape[0] % n == 0 and (a.shape[0] // n) % 16 == 0 for a in arrays))
    spec = lambda a: pl.BlockSpec((a.shape[0] // n_blocks, a.shape[1]), lambda i: (jnp.minimum(i, n_blocks - 1), 0))
    return ([spec(a) for a in arrays], [spec(a) for a in arrays],
            [jax.ShapeDtypeStruct(a.shape, BF16) for a in arrays])


def _cast_blocks(src_refs, dst_refs):
    for src, dst in zip(src_refs, dst_refs):
        dst[...] = src[...].astype(BF16)


def _router_kernel(x_ref, wr_ref, *rest, ne, tr, n_cast):
    cast_in, (route_ref, cnt_ref), cast_out = rest[:n_cast], rest[n_cast:n_cast + 2], rest[n_cast + 2:-2]
    carry, upper_s = rest[-2:]
    i = pl.program_id(0)
    _cast_blocks(cast_in, cast_out)

    @pl.when(i == 0)
    def _():
        carry[...] = jnp.zeros(carry.shape, F32)
        ri = lax.broadcasted_iota(jnp.int32, (tr, tr), 0)
        ci = lax.broadcasted_iota(jnp.int32, (tr, tr), 1)
        upper_s[...] = (ri <= ci).astype(BF16)

    xh = x_ref[...]
    x_hi = xh.astype(BF16)
    x_lo = (xh - x_hi.astype(F32)).astype(BF16)
    wr = wr_ref[...]
    w_hi = wr.astype(BF16)
    w_lo = (wr - w_hi.astype(F32)).astype(BF16)
    nt = (((1,), (1,)), ((), ()))
    logits = (lax.dot_general(w_hi, x_hi, nt, preferred_element_type=F32)
              + lax.dot_general(w_hi, x_lo, nt, preferred_element_type=F32)
              + lax.dot_general(w_lo, x_hi, nt, preferred_element_type=F32))
    mx = jnp.max(logits, axis=0, keepdims=True)
    ex = jnp.exp(logits - mx)
    probs = ex / jnp.sum(ex, axis=0, keepdims=True)
    eid = lax.broadcasted_iota(jnp.int32, (ne, tr), 0)
    p1 = jnp.max(probs, axis=0, keepdims=True)
    i1 = jnp.min(jnp.where(probs == p1, eid, ne), axis=0, keepdims=True)
    rest = jnp.where(eid == i1, -1.0, probs)
    p2 = jnp.max(rest, axis=0, keepdims=True)
    i2 = jnp.min(jnp.where(rest == p2, eid, ne), axis=0, keepdims=True)
    den = p1 + p2
    oh1 = (eid == i1).astype(F32)
    oh2 = (eid == i2).astype(F32)
    oh = oh1 + oh2
    incl = _mm(oh.astype(BF16), upper_s[...])
    before = carry[:, 0:1] + incl - oh
    r1 = jnp.sum(oh1 * before, axis=0, keepdims=True)
    r2 = jnp.sum(oh2 * before, axis=0, keepdims=True)
    zero = jnp.zeros((1, tr), F32)
    route_ref[...] = jnp.concatenate(
        [i1.astype(F32), i2.astype(F32), p1 / den, p2 / den, r1, r2, zero, zero], axis=0)
    total = carry[:, 0:1] + incl[:, tr - 1:tr]
    carry[...] = jnp.broadcast_to(total, carry.shape)
    cnt_ref[...] = jnp.broadcast_to(total, cnt_ref.shape)


def _router(x, w_router, tr, to_cast):
    t, d = x.shape
    ne = w_router.shape[1]
    c_in, c_out, c_shapes = _cast_plan(to_cast, t // tr)
    res = pl.pallas_call(
        functools.partial(_router_kernel, ne=ne, tr=tr, n_cast=len(to_cast)),
        grid=(t // tr,),
        in_specs=[pl.BlockSpec((tr, d), lambda i: (i, 0)), _full_spec((ne, d))] + c_in,
        out_specs=[pl.BlockSpec((8, tr), lambda i: (0, i)), pl.BlockSpec((ne, 128), lambda i: (0, 0))] + c_out,
        out_shape=[jax.ShapeDtypeStruct((8, t), F32), jax.ShapeDtypeStruct((ne, 128), F32)] + c_shapes,
        scratch_shapes=[pltpu.VMEM((ne, 128), F32), pltpu.VMEM((tr, tr), BF16)],
        compiler_params=pltpu.CompilerParams(dimension_semantics=("arbitrary",), vmem_limit_bytes=VMEM_LIMIT),
        name="moe_router",
    )(x, w_router.T.astype(F32), *to_cast)
    return res[0], res[1], res[2:]


def _dispatch_kernel(pos_ref, grp_ref, x_ref, *rest, tm, te, ne, n_tiles, n_cast):
    cast_in, xs_ref, cast_out = rest[:n_cast], rest[n_cast], rest[n_cast + 1:-2]
    stage, sem = rest[-2:]
    i = pl.program_id(0)
    last = pl.num_programs(0) - 1
    def scatter_tile(s):
        stage[s] = x_ref[...].reshape(tm, 8, x_ref.shape[1] // 8)

        def row_copy(r, k):
            return pltpu.make_async_copy(
                stage.at[s, r], xs_ref.at[pos_ref[0, 0, 2 * r + k]], sem.at[s])

        def start(g, carry):
            for u in range(ROW_DMA_UNROLL):
                row_copy(g * ROW_DMA_UNROLL + u, 0).start(priority=0)
                row_copy(g * ROW_DMA_UNROLL + u, 1).start(priority=1)
            return carry

        lax.fori_loop(0, tm // ROW_DMA_UNROLL, start, 0)

    def wait_slot(s):
        for _ in range(TOP_K):
            pltpu.make_async_copy(stage.at[s], xs_ref.at[pl.ds(0, tm)], sem.at[s]).wait()

    for s in range(2):
        @pl.when(i % 2 == s)
        def _(s=s):
            scatter_tile(s)

            @pl.when(i > 0)
            def _():
                wait_slot(1 - s)

            @pl.when(i == last)
            def _():
                wait_slot(s)

    _cast_blocks(cast_in, cast_out)

    @pl.when(i == last)
    def _():
        for e in range(ne):
            lo, hi = grp_ref[e] + grp_ref[ne + e], grp_ref[e] + grp_ref[2 * ne + e]
            mid = jnp.minimum((lo + 7) // 8 * 8, hi)

            def pad_row(r):
                return pltpu.make_async_copy(stage.at[0, pl.ds(0, 1)], xs_ref.at[pl.ds(r, 1)], sem.at[0])

            def pad_rows8(q):
                r = pl.multiple_of(mid + q * 8, 8)
                return pltpu.make_async_copy(stage.at[0, pl.ds(0, 8)], xs_ref.at[pl.ds(r, 8)], sem.at[1])

            lax.fori_loop(lo, mid, lambda r, c: (pad_row(r).start(), c)[1], 0)
            lax.fori_loop(0, (hi - mid) // 8, lambda q, c: (pad_rows8(q).start(), c)[1], 0)
            lax.fori_loop(lo, mid, lambda r, c: (pad_row(r).wait(), c)[1], 0)
            lax.fori_loop(0, (hi - mid) // 8, lambda q, c: (pad_rows8(q).wait(), c)[1], 0)

        def tile_copy(j):
            return pltpu.make_async_copy(stage.at[0, pl.ds(0, te)], xs_ref.at[pl.ds(j * te, te)], sem.at[0])

        lax.fori_loop(grp_ref[3 * ne], n_tiles, lambda j, c: (tile_copy(j).start(), c)[1], 0)
        lax.fori_loop(grp_ref[3 * ne], n_tiles, lambda j, c: (tile_copy(j).wait(), c)[1], 0)


def _dispatch(x, pos, grp, n_tiles, tm, te, to_cast):
    t, d = x.shape
    ne = (grp.shape[0] - 1) // 3
    assert te <= tm
    c_in, c_out, c_shapes = _cast_plan(to_cast, t // tm)
    res = pl.pallas_call(
        functools.partial(_dispatch_kernel, tm=tm, te=te, ne=ne, n_tiles=n_tiles, n_cast=len(to_cast)),
        grid=(t // tm,),
        in_specs=[
            pl.BlockSpec((1, 1, 2 * tm), lambda i: (i, 0, 0), memory_space=pltpu.SMEM),
            pl.BlockSpec(memory_space=pltpu.SMEM),
            pl.BlockSpec((tm, d), lambda i: (i, 0)),
        ] + c_in,
        out_specs=[pl.BlockSpec(memory_space=pl.ANY)] + c_out,
        out_shape=[jax.ShapeDtypeStruct((n_tiles * te, 8, d // 8), F32)] + c_shapes,
        scratch_shapes=[pltpu.VMEM((2, tm, 8, d // 8), F32), pltpu.SemaphoreType.DMA((2,))],
        compiler_params=pltpu.CompilerParams(
            dimension_semantics=("arbitrary",), has_side_effects=True, vmem_limit_bytes=VMEM_LIMIT),
        name="moe_dispatch",
    )(pos.reshape(t // tm, 1, 2 * tm), grp, x, *to_cast)
    return res[0], res[1:]


def _experts_kernel(te_ref, nv_ref, xs_ref, wg_ref, wu_ref, wd_ref, ys_ref):
    i = pl.program_id(0)

    @pl.when(i < nv_ref[0])
    def _():
        tm = xs_ref.shape[0]
        xb = xs_ref[...].reshape(tm, wg_ref.shape[1]).astype(BF16)
        ys_ref[...] = _swiglu(xb, wg_ref.at[0], wu_ref.at[0], wd_ref.at[0]).reshape(ys_ref.shape)

    @pl.when(i >= nv_ref[0])
    def _():
        ys_ref[...] = jnp.zeros(ys_ref.shape, F32)


def _experts(xs, tile_expert, n_valid, wg, wu, wd, tm):
    n_rows, d = xs.shape[0], xs.shape[1] * xs.shape[2]
    ne, _, ff = wg.shape
    grid_spec = pltpu.PrefetchScalarGridSpec(
        num_scalar_prefetch=2,
        grid=(n_rows // tm,),
        in_specs=[
            pl.BlockSpec((tm, 8, d // 8), lambda i, te, nv: (i, 0, 0)),
            pl.BlockSpec((1, d, ff), lambda i, te, nv: (te[i], 0, 0)),
            pl.BlockSpec((1, d, ff), lambda i, te, nv: (te[i], 0, 0)),
            pl.BlockSpec((1, ff, d), lambda i, te, nv: (te[i], 0, 0)),
        ],
        out_specs=pl.BlockSpec((tm, 8, d // 8), lambda i, te, nv: (i, 0, 0)),
    )
    return pl.pallas_call(
        _experts_kernel,
        grid_spec=grid_spec,
        out_shape=jax.ShapeDtypeStruct((n_rows, 8, d // 8), F32),
        compiler_params=pltpu.CompilerParams(
            dimension_semantics=("arbitrary",), vmem_limit_bytes=VMEM_LIMIT),
        name="moe_experts",
    )(tile_expert, n_valid, xs, wg, wu, wd)


def _combine_kernel(pos_ref, pos_next_ref, x_ref, rt_ref, ys_ref, g_ref, b_ref, *rest, tm, alpha, npt):
    o_refs, (buf, sem) = rest[:-2], rest[-2:]
    i = pl.program_id(0)
    n = pl.num_programs(0)

    def gather(p_ref, s):
        def row_copy(r, k):
            return pltpu.make_async_copy(
                ys_ref.at[p_ref[0, 0, 2 * r + k]], buf.at[s, k, r], sem.at[s])

        def start(g, carry):
            for u in range(ROW_DMA_UNROLL):
                row_copy(g * ROW_DMA_UNROLL + u, 0).start(priority=0)
                row_copy(g * ROW_DMA_UNROLL + u, 1).start(priority=1)
            return carry

        lax.fori_loop(0, tm // ROW_DMA_UNROLL, start, 0)

    @pl.when(i == 0)
    def _():
        gather(pos_ref, 0)

    for s in range(2):
        @pl.when(i % 2 == s)
        def _(s=s):
            @pl.when(i + 1 < n)
            def _():
                gather(pos_next_ref, 1 - s)

            for k in range(TOP_K):
                pltpu.make_async_copy(ys_ref.at[pl.ds(0, tm)], buf.at[s, k], sem.at[s]).wait()
            rt = rt_ref[...]
            d = x_ref.shape[1]
            f = rt[:, 2:3] * buf[s, 0].reshape(tm, d) + rt[:, 3:4] * buf[s, 1].reshape(tm, d)
            _token_store(o_refs, _layer_norm(alpha * x_ref[...] + f, g_ref[...], b_ref[...]), npt)


def _combine(x, pos, route_t, ys, g, b, alpha, tm, split):
    t, d = x.shape
    out_specs, out_shape = _token_out(t, d, tm, split)
    npt = None if split is None else split[0] // tm
    n_steps = t // tm
    pos3 = pos.reshape(n_steps, 1, 2 * tm)
    return pl.pallas_call(
        functools.partial(_combine_kernel, tm=tm, alpha=alpha, npt=npt),
        grid=(t // tm,),
        in_specs=[
            pl.BlockSpec((1, 1, 2 * tm), lambda i: (i, 0, 0), memory_space=pltpu.SMEM),
            pl.BlockSpec((1, 1, 2 * tm), lambda i: (jnp.minimum(i + 1, n_steps - 1), 0, 0), memory_space=pltpu.SMEM),
            pl.BlockSpec((tm, d), lambda i: (i, 0)),
            pl.BlockSpec((tm, 8), lambda i: (i, 0)),
            pl.BlockSpec(memory_space=pl.ANY),
            _full_spec((1, d)), _full_spec((1, d)),
        ],
        out_specs=out_specs,
        out_shape=out_shape,
        scratch_shapes=[pltpu.VMEM((2, TOP_K, tm, 8, d // 8), F32), pltpu.SemaphoreType.DMA((2,))],
        compiler_params=pltpu.CompilerParams(dimension_semantics=("arbitrary",), vmem_limit_bytes=VMEM_LIMIT),
        name="moe_combine",
    )(pos3, pos3, x, route_t, ys, g.reshape(1, d), b.reshape(1, d))


def _ffn_moe(x, w_router, wg, wu, wd, g, b, alpha, tr, tm, split, pre):
    t, d = x.shape
    ne = w_router.shape[1]
    ff = wg.shape[2]
    flat = {"wg": wg.reshape(ne * d, ff), "wu": wu.reshape(ne * d, ff), "wd": wd.reshape(ne * ff, d)}
    conv = dict(pre)
    r_keys = [k for k in ("wd",) if k not in conv]
    d_keys = [k for k in ("wg", "wu") if k not in conv]
    route, counts, r_out = _router(x, w_router, tr, [flat[k] for k in r_keys])
    conv.update(zip(r_keys, r_out))
    cnt = counts[:, 0].astype(jnp.int32)
    gsz = ((cnt + tm - 1) // tm) * tm
    ends = jnp.cumsum(gsz)
    offs = ends - gsz
    n_tiles = (TOP_K * t) // tm + ne
    tile_start = jnp.arange(n_tiles, dtype=jnp.int32) * tm
    tile_e = jnp.sum((tile_start[:, None] >= ends[None, :]).astype(jnp.int32), axis=1)
    n_valid = (ends[ne - 1] // tm).astype(jnp.int32).reshape(1)
    last_e = jnp.sum((ends[ne - 1] - 1 >= ends).astype(jnp.int32))
    tile_e = jnp.minimum(tile_e, last_e).astype(jnp.int32)
    i12 = route[0:2].astype(jnp.int32)
    base = sum(jnp.where(i12 == e, offs[e], 0) for e in range(ne))
    pos = (base + route[4:6].astype(jnp.int32)).T.reshape(-1)
    grp = jnp.concatenate([offs, cnt, gsz, n_valid]).astype(jnp.int32)
    xs, d_out = _dispatch(x, pos, grp, n_tiles, tr, tm, [flat[k] for k in d_keys])
    conv.update(zip(d_keys, d_out))
    ys = _experts(xs, tile_e, n_valid, conv["wg"].reshape(ne, d, ff), conv["wu"].reshape(ne, d, ff),
                  conv["wd"].reshape(ne, ff, d), tm)
    return _combine(x, pos, route.T, ys, g, b, alpha, tr, split)


def kernel(x_prompt, x_sample, cache_conv, state_gla, w_in, b_in, conv_w, conv_b, conv_ln_g, conv_ln_b, w_conv_out, b_conv_out, w_gate_up, b_gate, gla_norm_g, w_gla_out, w_o, ln1_g, ln1_b, ln2_g, ln2_b, ff_w_gate, ff_w_up, ff_w_down, w_router, moe_w_gate, moe_w_up, moe_w_down):
    p = dict(w_in=w_in, b_in=b_in, conv_w=conv_w, conv_b=conv_b, conv_ln_g=conv_ln_g, conv_ln_b=conv_ln_b,
             w_conv_out=w_conv_out, b_conv_out=b_conv_out, w_gate_up=w_gate_up, b_gate=b_gate,
             gla_norm_g=gla_norm_g, w_gla_out=w_gla_out, w_o=w_o, ln1_g=ln1_g, ln1_b=ln1_b)
    depth = w_in.shape[0]
    bsz, seq, d = x_prompt.shape
    ns, ls, _ = x_sample.shape
    dc = conv_w.shape[-1]
    rank, dk = w_gate_up.shape[1], w_gate_up.shape[2]
    dv = w_gla_out.shape[1]
    dims = dict(d=d, dc=dc, dk=dk, dv=dv, rank=rank, hk=dk // GLA_HEADS, hv=dv // GLA_HEADS,
                alpha=(2.0 * depth) ** 0.25)
    alpha = dims["alpha"]
    tl = min(512, seq)
    tp = bsz * seq
    ts = ns * ls
    tm = min(512, ts)
    assert seq % tl == 0 and tp % ts == 0 and ts % tm == 0 and ts % tl == 0 and ns % SEQ_PER_STEP == 0

    x_p, x_s, s_blk = x_prompt.reshape(tp, d), x_sample.reshape(ts, d), 0
    hist_p, state_p, hist_s, state_s = [], [], [], []
    w_in_b = w_in.astype(BF16)
    for l in range(depth):
        wts = _mixer_weights(l, p, dims, w_in_b)
        nxt = {}
        if l % 2 == 0 and l + 1 < depth:
            nxt = {k: m[l // 2].reshape(-1, m.shape[-1])
                   for k, m in (("wg", moe_w_gate), ("wu", moe_w_up), ("wd", moe_w_down))}
        x1s, hs, ss = _mixer_sample(x_s, s_blk, cache_conv, state_gla, l, wts, dims, ns, ls)
        x1, hp, sp, *done_mix = _mixer_prompt(x_p, x1s, wts, dims, tl, bsz, seq, [nxt["wu"]] if nxt else [])
        hist_p.append(hp), state_p.append(sp), hist_s.append(hs), state_s.append(ss)
        split = (tp, ts) if l == depth - 1 else None
        if l % 2 == 0:
            x2, done = _ffn_dense(x1, ff_w_gate[l // 2], ff_w_up[l // 2], ff_w_down[l // 2], ln2_g[l], ln2_b[l],
                                  alpha, tm, split, [nxt["wg"], nxt["wd"]] if nxt else [])
            pre_cast = dict(zip(("wg", "wd", "wu"), list(done) + done_mix)) if nxt else {}
        else:
            x2 = _ffn_moe(x1, w_router[l // 2], moe_w_gate[l // 2], moe_w_up[l // 2], moe_w_down[l // 2],
                          ln2_g[l], ln2_b[l], alpha, tm, min(EXPERT_TILE, tm), split, pre_cast)
        if split is None:
            x_p, x_s, s_blk = x2[0], x2[0], tp // ts
    y_p, y_s = x2
    return (y_p.reshape(bsz, seq, d), y_s.reshape(ns, ls, d), jnp.stack(hist_p),
            jnp.stack(state_p).astype(state_gla.dtype), jnp.stack(hist_s), jnp.stack(state_s).astype(state_gla.dtype))
```

```python
import functools

import jax
import jax.numpy as jnp
from jax import lax
from jax.experimental import pallas as pl
from jax.experimental.pallas import tpu as pltpu

CHUNK = 64
CONV_W = 31
GLA_HEADS = 4
GATE_TAU = 16.0
LN_EPS = 1e-5
RMS_EPS = 1e-6
TOP_K = 2

HIST_PAD = 32
CONV_ROWS = 64
ROW_DMA_UNROLL = 8
SEQ_PER_STEP = 4
EXPERT_TILE = 512
MXU_TILE = 256
FF_CHUNKS = 11
VMEM_LIMIT = 56 * 1024 * 1024

BF16 = jnp.bfloat16
F32 = jnp.float32


def _mm(a, b):
    return jnp.dot(a, b, preferred_element_type=F32)


def _sigmoid(x):
    return 0.5 * jnp.tanh(0.5 * x) + 0.5


def _silu(x):
    return x * _sigmoid(x)


def _log_sigmoid(z):
    return -(jnp.maximum(-z, 0.0) + jnp.log(1.0 + jnp.exp(-jnp.abs(z))))


def _layer_norm(x, g, b):
    mu = jnp.mean(x, axis=-1, keepdims=True)
    xc = x - mu
    var = jnp.mean(xc * xc, axis=-1, keepdims=True)
    return xc * lax.rsqrt(var + LN_EPS) * g + b


def _split3(x):
    hi = x.astype(BF16)
    r1 = x - hi.astype(F32)
    mid = r1.astype(BF16)
    lo = (r1 - mid.astype(F32)).astype(BF16)
    return hi, mid, lo


def _full_spec(shape):
    zeros = (0,) * len(shape)
    return pl.BlockSpec(shape, lambda *_: zeros, pipeline_mode=pl.Buffered(1))


def _project(xb, w, dims, u_store, q_s, k_s, v_s, lf_s):
    dc, dk, dv = dims["dc"], dims["dk"], dims["dv"]
    off_q, off_k, off_v = 2 * dc, 2 * dc + dk, 2 * dc + 2 * dk
    off_g = off_v + dv
    glu = _mm(xb, w["wmain"][:, 0:off_q]) + w["bmain"][:, 0:off_q]
    u_store(glu[:, 0:dc] * _sigmoid(glu[:, dc:off_q]))
    q_s[...] = (_mm(xb, w["wmain"][:, off_q:off_k]) + w["bmain"][:, off_q:off_k]) * (dims["hk"] ** -0.5)
    k_s[...] = _mm(xb, w["wmain"][:, off_k:off_v]) + w["bmain"][:, off_k:off_v]
    v_s[...] = _mm(xb, w["wmain"][:, off_v:off_g]) + w["bmain"][:, off_v:off_g]
    lr = _mm(xb, w["wlr"][:, 0:dims["rank"]]) + w["blr"][...]
    z = _mm(lr.astype(BF16), w["wgu"][...]) + w["bgu"][...]
    lf_s[...] = _log_sigmoid(z) * (1.0 / GATE_TAU)


def _gla_chunk(q_s, k_s, v_s, lf_s, o_s, rows, c, get_state, set_state, dims):
    hk, hv, dk = dims["hk"], dims["hv"], dims["dk"]
    lf = lf_s[rows, :]
    ri = lax.broadcasted_iota(jnp.int32, (c, c), 0)
    ci = lax.broadcasted_iota(jnp.int32, (c, c), 1)
    causal = ri >= ci
    tri = causal.astype(BF16)
    hi, mid, lo = _split3(lf)
    b = _mm(tri, hi) + _mm(tri, mid) + _mm(tri, lo)
    b_last = b[c - 1:c, :]
    q_t = q_s[rows, :] * jnp.exp(b)
    kk = k_s[rows, :]
    k_t = kk * jnp.exp(-b)
    k_e = kk * jnp.exp(b_last - b)
    d_t = jnp.transpose(jnp.broadcast_to(jnp.exp(b_last), (128, dk)))
    vv = v_s[rows, :]
    for h in range(GLA_HEADS):
        ks = slice(h * hk, (h + 1) * hk)
        vs = slice(h * hv, (h + 1) * hv)
        qh = q_t[:, ks].astype(BF16)
        kh = k_t[:, ks].astype(BF16)
        keh = k_e[:, ks].astype(BF16)
        vh = vv[:, vs].astype(BF16)
        att = lax.dot_general(qh, kh, (((1,), (1,)), ((), ())), preferred_element_type=F32)
        att = jnp.where(causal, att, 0.0)
        s_old = get_state(h)
        o_s[rows, vs] = _mm(qh, s_old.astype(BF16)) + _mm(att.astype(BF16), vh)
        upd = lax.dot_general(keh, vh, (((0,), (0,)), ((), ())), preferred_element_type=F32)
        dcol = d_t[ks, :]
        decay = jnp.concatenate([dcol] * (hv // 128), axis=1)
        set_state(h, decay * s_old + upd)


def _conv_block(win, w8_ref, n):
    wn, ch = win.shape
    base = HIST_PAD - (CONV_W - 1)
    acc = None
    for b in range(8):
        wb = win if b == 0 else pltpu.roll(win, wn - b, axis=0)
        for a in range((base + CONV_W + 7) // 8):
            j = 8 * a + b - base
            if 0 <= j < CONV_W:
                term = w8_ref[8 * j:8 * j + 8, :][None] * wb[8 * a:8 * a + n, :].reshape(n // 8, 8, ch)
                acc = term if acc is None else acc + term
    return acc.reshape(n, ch)


def _tail(x, xb, c, w, dims, o_s):
    d, dc, dk, dv, hv = dims["d"], dims["dc"], dims["dk"], dims["dv"], dims["hv"]
    off_g = 2 * dc + 2 * dk + dv
    off_lr = off_g + dv
    c = _silu(_layer_norm(c + w["convb"][...], w["clng"][...], w["clnb"][...]))
    y_a = _mm(c.astype(BF16), w["wco"][...]) + w["bco"][...]
    g_out = _mm(xb, w["wmain"][:, off_g:off_lr]) + w["bmain"][:, off_g:off_lr]
    heads = []
    for h in range(GLA_HEADS):
        oh = o_s[:, h * hv:(h + 1) * hv]
        ms = jnp.mean(oh * oh, axis=-1, keepdims=True)
        heads.append(oh * lax.rsqrt(ms + RMS_EPS) * w["gng"][...])
    o = jnp.concatenate(heads, axis=1) * _silu(g_out)
    y_b = _mm(o.astype(BF16), w["wgo"][...])
    gates = _sigmoid(_mm(xb, w["wgates"][...]) + w["bgates"][...])
    merged = gates[:, 0:d] * y_a + gates[:, d:2 * d] * y_b
    m = _mm(merged.astype(BF16), w["wo"][...])
    return _layer_norm(dims["alpha"] * x + m, w["ln1g"][...], w["ln1b"][...])


_W_NAMES = ("wmain", "bmain", "wlr", "blr", "wgates", "bgates", "convw", "convb", "clng", "clnb",
            "wco", "bco", "wgu", "bgu", "gng", "wgo", "wo", "ln1g", "ln1b")


def _weight_refs(refs):
    w = dict(zip(_W_NAMES, refs))
    w["wmain"], w["wlr"] = w["wmain"].at[0], w["wlr"].at[0]
    return w


def _mixer_prompt_kernel(*refs, dims, tl, nt, n_prompt, n_cast):
    nw = len(_W_NAMES)
    x_ref, tail_ref = refs[0:2]
    w = _weight_refs(refs[2:2 + nw])
    cast_in = refs[2 + nw:2 + nw + n_cast]
    x1_ref, hist_ref, state_ref = refs[2 + nw + n_cast:5 + nw + n_cast]
    cast_out = refs[5 + nw + n_cast:5 + nw + 2 * n_cast]
    scratch = refs[5 + nw + 2 * n_cast:]
    i = pl.program_id(0)
    _cast_blocks(cast_in, cast_out)

    @pl.when(i < n_prompt)
    def _():
        _mixer_prompt_tile(i % nt, x_ref, w, x1_ref, hist_ref, state_ref, scratch, dims, tl, nt)

    @pl.when(i >= n_prompt)
    def _():
        x1_ref[...] = tail_ref[...]


def _mixer_prompt_tile(j, x_ref, w, x1_ref, hist_ref, state_ref, scratch, dims, tl, nt):
    ubuf, qt_s, kt_s, ke_s, vb_s, g_s, gt_s, o_s, c_s, s_s, dt_s = scratch
    d, dc, dk, dv, hk, hv = dims["d"], dims["dc"], dims["dk"], dims["dv"], dims["hk"], dims["hv"]
    off_q, off_k, off_v = 2 * dc, 2 * dc + dk, 2 * dc + 2 * dk
    off_g = off_v + dv
    off_lr = off_g + dv
    nch = tl // CHUNK
    rep = 128 // nch
    wm, bm = w["wmain"], w["bmain"]

    @pl.when(j == 0)
    def _():
        ubuf[0:HIST_PAD, :] = jnp.zeros((HIST_PAD, dc), F32)
        s_s[...] = jnp.zeros(s_s.shape, F32)

    x = x_ref[...]
    xb = x.astype(BF16)

    glu = _mm(xb, wm[:, 0:off_q]) + bm[:, 0:off_q]
    ubuf[HIST_PAD:HIST_PAD + tl, :] = glu[:, 0:dc] * _sigmoid(glu[:, dc:off_q])
    lr = _mm(xb, w["wlr"][:, 0:dims["rank"]]) + w["blr"][...]
    z = _mm(lr.astype(BF16), w["wgu"][...]) + w["bgu"][...]
    lf = _log_sigmoid(z) * (1.0 / GATE_TAU)
    ri = lax.broadcasted_iota(jnp.int32, (CHUNK, CHUNK), 0)
    ci = lax.broadcasted_iota(jnp.int32, (CHUNK, CHUNK), 1)
    causal = ri >= ci
    tri = causal.astype(BF16)
    parts = _split3(lf)
    b = jnp.concatenate(
        [sum(_mm(tri, p[c * CHUNK:(c + 1) * CHUNK, :]) for p in parts) for c in range(nch)], axis=0)
    b3 = b.reshape(nch, CHUNK, dk)
    b_last = b3[:, CHUNK - 1:CHUNK, :]
    q = (_mm(xb, wm[:, off_q:off_k]) + bm[:, off_q:off_k]) * (hk ** -0.5)
    qt_s[...] = (q * jnp.exp(b)).astype(BF16)
    kk = _mm(xb, wm[:, off_k:off_v]) + bm[:, off_k:off_v]
    kt_s[...] = (kk * jnp.exp(-b)).astype(BF16)
    ke_s[...] = (kk.reshape(nch, CHUNK, dk) * jnp.exp(b_last - b3)).reshape(tl, dk).astype(BF16)
    vb_s[...] = (_mm(xb, wm[:, off_v:off_g]) + bm[:, off_v:off_g]).astype(BF16)
    d_rows = jnp.broadcast_to(jnp.exp(b_last), (nch, rep, dk)).reshape(nch * rep, dk)
    if nch * rep < 128:
        d_rows = jnp.concatenate([d_rows, jnp.zeros((128 - nch * rep, dk), F32)], axis=0)
    dt_s[...] = jnp.transpose(d_rows)
    g_s[...] = _silu(_mm(xb, wm[:, off_g:off_lr]) + bm[:, off_g:off_lr])

    states = [s_s[h] for h in range(GLA_HEADS)]
    conv_per = tl // CONV_ROWS // nch
    gw = 2 * d // nch
    for c in range(nch):
        rows = slice(c * CHUNK, (c + 1) * CHUNK)
        for h in range(GLA_HEADS):
            ks = slice(h * hk, (h + 1) * hk)
            vs = slice(h * hv, (h + 1) * hv)
            qh, kh, keh, vh = qt_s[rows, ks], kt_s[rows, ks], ke_s[rows, ks], vb_s[rows, vs]
            att = lax.dot_general(qh, kh, (((1,), (1,)), ((), ())), preferred_element_type=F32)
            att = jnp.where(causal, att, 0.0)
            s_old = states[h]
            o_s[rows, vs] = _mm(qh, s_old.astype(BF16)) + _mm(att.astype(BF16), vh)
            upd = lax.dot_general(keh, vh, (((0,), (0,)), ((), ())), preferred_element_type=F32)
            decay = jnp.broadcast_to(dt_s[ks, c * rep:c * rep + 1], (hk, hv))
            states[h] = decay * s_old + upd
        for bi in range(c * conv_per, (c + 1) * conv_per):
            r0 = bi * CONV_ROWS
            c_s[r0:r0 + CONV_ROWS, :] = _conv_block(
                ubuf[r0:r0 + CONV_ROWS + HIST_PAD, :], w["convw"], CONV_ROWS)
        cols = slice(c * gw, (c + 1) * gw)
        gt_s[:, cols] = _sigmoid(_mm(xb, w["wgates"][:, cols]) + w["bgates"][:, cols])
    for h in range(GLA_HEADS):
        s_s[h] = states[h]

    cv = _silu(_layer_norm(c_s[...] + w["convb"][...], w["clng"][...], w["clnb"][...]))
    y_a = _mm(cv.astype(BF16), w["wco"][...]) + w["bco"][...]
    heads = []
    for h in range(GLA_HEADS):
        oh = o_s[:, h * hv:(h + 1) * hv]
        ms = jnp.mean(oh * oh, axis=-1, keepdims=True)
        heads.append(oh * lax.rsqrt(ms + RMS_EPS) * w["gng"][...])
    o = jnp.concatenate(heads, axis=1) * g_s[...]
    y_b = _mm(o.astype(BF16), w["wgo"][...])
    merged = gt_s[:, 0:d] * y_a + gt_s[:, d:2 * d] * y_b
    m = _mm(merged.astype(BF16), w["wo"][...])
    x1_ref[...] = _layer_norm(dims["alpha"] * x + m, w["ln1g"][...], w["ln1b"][...])

    @pl.when(j == nt - 1)
    def _():
        hist_ref[0] = ubuf[tl + HIST_PAD - (CONV_W - 1):tl + HIST_PAD, :]
        state_ref[0] = s_s[...]

    ubuf[0:HIST_PAD, :] = ubuf[tl:tl + HIST_PAD, :]


def _mixer_sample_kernel(*refs, dims, ns, ls):
    nw = len(_W_NAMES)
    x_ref, hist_in_ref, state_in_ref = refs[0:3]
    w = _weight_refs(refs[3:3 + nw])
    x1_ref, hist_ref, state_ref = refs[3 + nw:6 + nw]
    ubuf, q_s, k_s, v_s, lf_s, o_s, c_s = refs[6 + nw:]
    i = pl.program_id(0)
    dc = dims["dc"]
    hl = CONV_W - 1

    @pl.when(i == 0)
    def _():
        xb = x_ref[...].astype(BF16)
        ubuf[:, 0:8, :] = jnp.zeros((ns, 8, dc), F32)
        ubuf[:, HIST_PAD - hl:HIST_PAD, :] = hist_in_ref[0]

        def u_store(u):
            ubuf[:, HIST_PAD:HIST_PAD + ls, :] = u.reshape(ns, ls, dc)

        _project(xb, w, dims, u_store, q_s, k_s, v_s, lf_s)

    for s in range(SEQ_PER_STEP):
        seq = i * SEQ_PER_STEP + s
        rows = pl.ds(pl.multiple_of(seq * ls, ls), ls)

        def set_state(h, val, s=s):
            state_ref[s, h] = val

        _gla_chunk(q_s, k_s, v_s, lf_s, o_s, rows, ls, lambda h, s=s: state_in_ref[0, s, h], set_state, dims)
        win = ubuf[seq]
        c_s[rows, :] = _conv_block(win, w["convw"], ls)
        hist_ref[seq] = win[HIST_PAD + ls - hl:HIST_PAD + ls, :]

    @pl.when(i == ns // SEQ_PER_STEP - 1)
    def _():
        x = x_ref[...]
        x1_ref[...] = _tail(x, x.astype(BF16), c_s[...], w, dims, o_s)


def _mixer_weights(l, p, dims, w_in_b):
    dc, dk, dv, rank = dims["dc"], dims["dk"], dims["dv"], dims["rank"]
    off_lr = 2 * dc + 2 * dk + 2 * dv
    off_gates = off_lr + rank
    assert off_lr % 128 == 0 and rank <= 128
    b_in = p["b_in"][l]
    d = w_in_b.shape[1]
    row = lambda v: v.reshape(1, -1).astype(F32)
    ops = (
        w_in_b, row(b_in[:off_lr]),
        w_in_b, row(b_in[off_lr:off_gates]),
        w_in_b[l][:, off_gates:], row(b_in[off_gates:]),
        jnp.repeat(p["conv_w"][l].astype(F32), 8, axis=0),
        row(p["conv_b"][l]), row(p["conv_ln_g"][l]), row(p["conv_ln_b"][l]),
        p["w_conv_out"][l].astype(BF16), row(p["b_conv_out"][l]),
        p["w_gate_up"][l].astype(BF16), row(p["b_gate"][l]),
        row(p["gla_norm_g"][l]),
        p["w_gla_out"][l].astype(BF16), p["w_o"][l].astype(BF16),
        row(p["ln1_g"][l]), row(p["ln1_b"][l]),
    )
    specs = [_full_spec(a.shape) for a in ops]
    specs[0] = pl.BlockSpec((1, d, off_lr), lambda *_: (l, 0, 0), pipeline_mode=pl.Buffered(1))
    specs[2] = pl.BlockSpec((1, d, 128), lambda *_: (l, 0, off_lr // 128), pipeline_mode=pl.Buffered(1))
    return ops, specs


def _mixer_prompt(x, x1_tail, wts, dims, tl, bsz, seq, to_cast):
    d = x.shape[1]
    dc, dk, dv, hk, hv = dims["dc"], dims["dk"], dims["dv"], dims["hk"], dims["hv"]
    hl = CONV_W - 1
    nt = seq // tl
    n_prompt = bsz * nt
    n_tail = x1_tail.shape[0] // tl
    kern = functools.partial(_mixer_prompt_kernel, dims=dims, tl=tl, nt=nt, n_prompt=n_prompt,
                             n_cast=len(to_cast))
    seq_of = lambda i: jnp.minimum(i // nt, bsz - 1)
    c_in, c_out, c_shapes = _cast_plan(to_cast, n_prompt + n_tail) if to_cast else ([], [], [])
    return pl.pallas_call(
        kern,
        grid=(n_prompt + n_tail,),
        in_specs=[pl.BlockSpec((tl, d), lambda i: (jnp.minimum(i, n_prompt - 1), 0)),
                  pl.BlockSpec((tl, d), lambda i: (jnp.maximum(i - n_prompt, 0), 0))]
        + wts[1] + c_in,
        out_specs=[
            pl.BlockSpec((tl, d), lambda i: (i, 0)),
            pl.BlockSpec((1, hl, dc), lambda i: (seq_of(i), 0, 0)),
            pl.BlockSpec((1, GLA_HEADS, hk, hv), lambda i: (seq_of(i), 0, 0, 0)),
        ] + c_out,
        out_shape=[
            jax.ShapeDtypeStruct(((n_prompt + n_tail) * tl, d), F32),
            jax.ShapeDtypeStruct((bsz, hl, dc), F32),
            jax.ShapeDtypeStruct((bsz, GLA_HEADS, hk, hv), F32),
        ] + c_shapes,
        scratch_shapes=[
            pltpu.VMEM((HIST_PAD + tl, dc), F32),
            pltpu.VMEM((tl, dk), BF16), pltpu.VMEM((tl, dk), BF16), pltpu.VMEM((tl, dk), BF16),
            pltpu.VMEM((tl, dv), BF16),
            pltpu.VMEM((tl, dv), F32), pltpu.VMEM((tl, 2 * d), F32),
            pltpu.VMEM((tl, dv), F32), pltpu.VMEM((tl, dc), F32),
            pltpu.VMEM((GLA_HEADS, hk, hv), F32), pltpu.VMEM((dk, 128), F32),
        ],
        compiler_params=pltpu.CompilerParams(
            dimension_semantics=("arbitrary",), vmem_limit_bytes=VMEM_LIMIT),
        name="mixer_prompt",
    )(x, x1_tail, *wts[0], *to_cast)


def _mixer_sample(x, in_blk, hist, state, layer, wts, dims, ns, ls):
    d = x.shape[1]
    dc, dk, dv, hk, hv = dims["dc"], dims["dk"], dims["dv"], dims["hk"], dims["hv"]
    hl = CONV_W - 1
    t = ns * ls
    kern = functools.partial(_mixer_sample_kernel, dims=dims, ns=ns, ls=ls)
    x1, hist_o, state_o = pl.pallas_call(
        kern,
        grid=(ns // SEQ_PER_STEP,),
        in_specs=[
            pl.BlockSpec((t, d), lambda i: (in_blk, 0)),
            pl.BlockSpec((1, ns, hl, dc), lambda i: (layer, 0, 0, 0)),
            pl.BlockSpec((1, SEQ_PER_STEP, GLA_HEADS, hk, hv), lambda i: (layer, i, 0, 0, 0)),
        ] + wts[1],
        out_specs=[
            pl.BlockSpec((t, d), lambda i: (0, 0)),
            pl.BlockSpec((ns, hl, dc), lambda i: (0, 0, 0)),
            pl.BlockSpec((SEQ_PER_STEP, GLA_HEADS, hk, hv), lambda i: (i, 0, 0, 0)),
        ],
        out_shape=[
            jax.ShapeDtypeStruct((t, d), F32),
            jax.ShapeDtypeStruct((ns, hl, dc), F32),
            jax.ShapeDtypeStruct((ns, GLA_HEADS, hk, hv), F32),
        ],
        scratch_shapes=[
            pltpu.VMEM((ns, HIST_PAD + ls, dc), F32),
            pltpu.VMEM((t, dk), F32), pltpu.VMEM((t, dk), F32), pltpu.VMEM((t, dv), F32),
            pltpu.VMEM((t, dk), F32), pltpu.VMEM((t, dv), F32), pltpu.VMEM((t, dc), F32),
        ],
        compiler_params=pltpu.CompilerParams(
            dimension_semantics=("arbitrary",), vmem_limit_bytes=VMEM_LIMIT),
        name="mixer_sample",
    )(x, hist, state, *wts[0])
    return x1, hist_o, state_o


def _swiglu(xb, wg_ref, wu_ref, wd_ref):
    ff = wg_ref.shape[-1]
    n_tiles = -(-ff // MXU_TILE)
    edges = [min(ff, (n_tiles * c // FF_CHUNKS) * MXU_TILE) for c in range(FF_CHUNKS)] + [ff]
    y = None
    for a, b in zip(edges[:-1], edges[1:]):
        h = _silu(_mm(xb, wg_ref[:, a:b])) * _mm(xb, wu_ref[:, a:b])
        part = _mm(h.astype(BF16), wd_ref[a:b, :])
        y = part if y is None else y + part
    return y


def _token_out(t, d, tm, split):
    if split is None:
        return [pl.BlockSpec((tm, d), lambda i: (i, 0))], [jax.ShapeDtypeStruct((t, d), F32)]
    tp, ts = split
    npt = tp // tm
    specs = [pl.BlockSpec((tm, d), lambda i: (jnp.minimum(i, npt - 1), 0)),
             pl.BlockSpec((tm, d), lambda i: (jnp.maximum(i - npt, 0), 0))]
    return specs, [jax.ShapeDtypeStruct((tp, d), F32), jax.ShapeDtypeStruct((ts, d), F32)]


def _token_store(o_refs, val, npt):
    if len(o_refs) == 1:
        o_refs[0][...] = val
        return
    i = pl.program_id(0)

    @pl.when(i < npt)
    def _():
        o_refs[0][...] = val

    @pl.when(i >= npt)
    def _():
        o_refs[1][...] = val


def _ffn_dense_kernel(x_ref, wg_ref, wu_ref, wd_ref, g_ref, b_ref, *rest, alpha, npt, n_cast):
    cast_in, o_refs, cast_out = rest[:n_cast], rest[n_cast:len(rest) - n_cast], rest[len(rest) - n_cast:]
    _cast_blocks(cast_in, cast_out)
    x = x_ref[...]
    xb = x.astype(BF16)
    f = _swiglu(xb, wg_ref, wu_ref, wd_ref)
    _token_store(o_refs, _layer_norm(alpha * x + f, g_ref[...], b_ref[...]), npt)


def _ffn_dense(x, wg, wu, wd, g, b, alpha, tm, split, to_cast):
    t, d = x.shape
    ops = (wg.astype(BF16), wu.astype(BF16), wd.astype(BF16), g.reshape(1, d), b.reshape(1, d))
    out_specs, out_shape = _token_out(t, d, tm, split)
    npt = None if split is None else split[0] // tm
    c_in, c_out, c_shapes = _cast_plan(to_cast, t // tm) if to_cast else ([], [], [])
    res = pl.pallas_call(
        functools.partial(_ffn_dense_kernel, alpha=alpha, npt=npt, n_cast=len(to_cast)),
        grid=(t // tm,),
        in_specs=[pl.BlockSpec((tm, d), lambda i: (i, 0))] + [_full_spec(a.shape) for a in ops] + c_in,
        out_specs=out_specs + c_out,
        out_shape=out_shape + c_shapes,
        compiler_params=pltpu.CompilerParams(
            dimension_semantics=("arbitrary",), vmem_limit_bytes=VMEM_LIMIT),
        name="ffn_dense",
    )(x, *ops, *to_cast)
    return res[:len(out_shape)], res[len(out_shape):]


def _cast_plan(arrays, n_steps):
    n_blocks = max(n for n in range(1, n_steps + 1)
                   if all(a.shape[0] % n == 0 and (a.shape[0] // n) % 16 == 0 for a in arrays))
    spec = lambda a: pl.BlockSpec((a.shape[0] // n_blocks, a.shape[1]), lambda i: (jnp.minimum(i, n_blocks - 1), 0))
    return ([spec(a) for a in arrays], [spec(a) for a in arrays],
            [jax.ShapeDtypeStruct(a.shape, BF16) for a in arrays])


def _cast_blocks(src_refs, dst_refs):
    for src, dst in zip(src_refs, dst_refs):
        dst[...] = src[...].astype(BF16)


def _router_kernel(x_ref, wr_ref, *rest, ne, tr, n_cast):
    cast_in, (route_ref, cnt_ref), cast_out = rest[:n_cast], rest[n_cast:n_cast + 2], rest[n_cast + 2:-2]
    carry, upper_s = rest[-2:]
    i = pl.program_id(0)
    _cast_blocks(cast_in, cast_out)

    @pl.when(i == 0)
    def _():
        carry[...] = jnp.zeros(carry.shape, F32)
        ri = lax.broadcasted_iota(jnp.int32, (tr, tr), 0)
        ci = lax.broadcasted_iota(jnp.int32, (tr, tr), 1)
        upper_s[...] = (ri <= ci).astype(BF16)

    xh = x_ref[...]
    x_hi = xh.astype(BF16)
    x_lo = (xh - x_hi.astype(F32)).astype(BF16)
    wr = wr_ref[...]
    w_hi = wr.astype(BF16)
    w_lo = (wr - w_hi.astype(F32)).astype(BF16)
    nt = (((1,), (1,)), ((), ()))
    logits = (lax.dot_general(w_hi, x_hi, nt, preferred_element_type=F32)
              + lax.dot_general(w_hi, x_lo, nt, preferred_element_type=F32)
              + lax.dot_general(w_lo, x_hi, nt, preferred_element_type=F32))
    mx = jnp.max(logits, axis=0, keepdims=True)
    ex = jnp.exp(logits - mx)
    probs = ex / jnp.sum(ex, axis=0, keepdims=True)
    eid = lax.broadcasted_iota(jnp.int32, (ne, tr), 0)
    p1 = jnp.max(probs, axis=0, keepdims=True)
    i1 = jnp.min(jnp.where(probs == p1, eid, ne), axis=0, keepdims=True)
    rest = jnp.where(eid == i1, -1.0, probs)
    p2 = jnp.max(rest, axis=0, keepdims=True)
    i2 = jnp.min(jnp.where(rest == p2, eid, ne), axis=0, keepdims=True)
    den = p1 + p2
    oh1 = (eid == i1).astype(F32)
    oh2 = (eid == i2).astype(F32)
    oh = oh1 + oh2
    incl = _mm(oh.astype(BF16), upper_s[...])
    before = carry[:, 0:1] + incl - oh
    r1 = jnp.sum(oh1 * before, axis=0, keepdims=True)
    r2 = jnp.sum(oh2 * before, axis=0, keepdims=True)
    zero = jnp.zeros((1, tr), F32)
    route_ref[...] = jnp.concatenate(
        [i1.astype(F32), i2.astype(F32), p1 / den, p2 / den, r1, r2, zero, zero], axis=0)
    total = carry[:, 0:1] + incl[:, tr - 1:tr]
    carry[...] = jnp.broadcast_to(total, carry.shape)
    cnt_ref[...] = jnp.broadcast_to(total, cnt_ref.shape)


def _router(x, w_router, tr, to_cast):
    t, d = x.shape
    ne = w_router.shape[1]
    c_in, c_out, c_shapes = _cast_plan(to_cast, t // tr)
    res = pl.pallas_call(
        functools.partial(_router_kernel, ne=ne, tr=tr, n_cast=len(to_cast)),
        grid=(t // tr,),
        in_specs=[pl.BlockSpec((tr, d), lambda i: (i, 0)), _full_spec((ne, d))] + c_in,
        out_specs=[pl.BlockSpec((8, tr), lambda i: (0, i)), pl.BlockSpec((ne, 128), lambda i: (0, 0))] + c_out,
        out_shape=[jax.ShapeDtypeStruct((8, t), F32), jax.ShapeDtypeStruct((ne, 128), F32)] + c_shapes,
        scratch_shapes=[pltpu.VMEM((ne, 128), F32), pltpu.VMEM((tr, tr), BF16)],
        compiler_params=pltpu.CompilerParams(dimension_semantics=("arbitrary",), vmem_limit_bytes=VMEM_LIMIT),
        name="moe_router",
    )(x, w_router.T.astype(F32), *to_cast)
    return res[0], res[1], res[2:]


def _dispatch_kernel(pos_ref, grp_ref, x_ref, *rest, tm, te, ne, n_tiles, n_cast):
    cast_in, xs_ref, cast_out = rest[:n_cast], rest[n_cast], rest[n_cast + 1:-2]
    stage, sem = rest[-2:]
    i = pl.program_id(0)
    last = pl.num_programs(0) - 1
    def scatter_tile(s):
        stage[s] = x_ref[...].reshape(tm, 8, x_ref.shape[1] // 8)

        def row_copy(r, k):
            return pltpu.make_async_copy(
                stage.at[s, r], xs_ref.at[pos_ref[0, 0, 2 * r + k]], sem.at[s])

        def start(g, carry):
            for u in range(ROW_DMA_UNROLL):
                row_copy(g * ROW_DMA_UNROLL + u, 0).start(priority=0)
                row_copy(g * ROW_DMA_UNROLL + u, 1).start(priority=1)
            return carry

        lax.fori_loop(0, tm // ROW_DMA_UNROLL, start, 0)

    def wait_slot(s):
        for _ in range(TOP_K):
            pltpu.make_async_copy(stage.at[s], xs_ref.at[pl.ds(0, tm)], sem.at[s]).wait()

    for s in range(2):
        @pl.when(i % 2 == s)
        def _(s=s):
            scatter_tile(s)

            @pl.when(i > 0)
            def _():
                wait_slot(1 - s)

            @pl.when(i == last)
            def _():
                wait_slot(s)

    _cast_blocks(cast_in, cast_out)

    @pl.when(i == last)
    def _():
        for e in range(ne):
            lo, hi = grp_ref[e] + grp_ref[ne + e], grp_ref[e] + grp_ref[2 * ne + e]
            mid = jnp.minimum((lo + 7) // 8 * 8, hi)

            def pad_row(r):
                return pltpu.make_async_copy(stage.at[0, pl.ds(0, 1)], xs_ref.at[pl.ds(r, 1)], sem.at[0])

            def pad_rows8(q):
                r = pl.multiple_of(mid + q * 8, 8)
                return pltpu.make_async_copy(stage.at[0, pl.ds(0, 8)], xs_ref.at[pl.ds(r, 8)], sem.at[1])

            lax.fori_loop(lo, mid, lambda r, c: (pad_row(r).start(), c)[1], 0)
            lax.fori_loop(0, (hi - mid) // 8, lambda q, c: (pad_rows8(q).start(), c)[1], 0)
            lax.fori_loop(lo, mid, lambda r, c: (pad_row(r).wait(), c)[1], 0)
            lax.fori_loop(0, (hi - mid) // 8, lambda q, c: (pad_rows8(q).wait(), c)[1], 0)

        def tile_copy(j):
            return pltpu.make_async_copy(stage.at[0, pl.ds(0, te)], xs_ref.at[pl.ds(j * te, te)], sem.at[0])

        lax.fori_loop(grp_ref[3 * ne], n_tiles, lambda j, c: (tile_copy(j).start(), c)[1], 0)
        lax.fori_loop(grp_ref[3 * ne], n_tiles, lambda j, c: (tile_copy(j).wait(), c)[1], 0)


def _dispatch(x, pos, grp, n_tiles, tm, te, to_cast):
    t, d = x.shape
    ne = (grp.shape[0] - 1) // 3
    assert te <= tm
    c_in, c_out, c_shapes = _cast_plan(to_cast, t // tm)
    res = pl.pallas_call(
        functools.partial(_dispatch_kernel, tm=tm, te=te, ne=ne, n_tiles=n_tiles, n_cast=len(to_cast)),
        grid=(t // tm,),
        in_specs=[
            pl.BlockSpec((1, 1, 2 * tm), lambda i: (i, 0, 0), memory_space=pltpu.SMEM),
            pl.BlockSpec(memory_space=pltpu.SMEM),
            pl.BlockSpec((tm, d), lambda i: (i, 0)),
        ] + c_in,
        out_specs=[pl.BlockSpec(memory_space=pl.ANY)] + c_out,
        out_shape=[jax.ShapeDtypeStruct((n_tiles * te, 8, d // 8), F32)] + c_shapes,
        scratch_shapes=[pltpu.VMEM((2, tm, 8, d // 8), F32), pltpu.SemaphoreType.DMA((2,))],
        compiler_params=pltpu.CompilerParams(
            dimension_semantics=("arbitrary",), has_side_effects=True, vmem_limit_bytes=VMEM_LIMIT),
        name="moe_dispatch",
    )(pos.reshape(t // tm, 1, 2 * tm), grp, x, *to_cast)
    return res[0], res[1:]


def _experts_kernel(te_ref, nv_ref, xs_ref, wg_ref, wu_ref, wd_ref, ys_ref):
    i = pl.program_id(0)

    @pl.when(i < nv_ref[0])
    def _():
        tm = xs_ref.shape[0]
        xb = xs_ref[...].reshape(tm, wg_ref.shape[1]).astype(BF16)
        ys_ref[...] = _swiglu(xb, wg_ref.at[0], wu_ref.at[0], wd_ref.at[0]).reshape(ys_ref.shape)

    @pl.when(i >= nv_ref[0])
    def _():
        ys_ref[...] = jnp.zeros(ys_ref.shape, F32)


def _experts(xs, tile_expert, n_valid, wg, wu, wd, tm):
    n_rows, d = xs.shape[0], xs.shape[1] * xs.shape[2]
    ne, _, ff = wg.shape
    grid_spec = pltpu.PrefetchScalarGridSpec(
        num_scalar_prefetch=2,
        grid=(n_rows // tm,),
        in_specs=[
            pl.BlockSpec((tm, 8, d // 8), lambda i, te, nv: (i, 0, 0)),
            pl.BlockSpec((1, d, ff), lambda i, te, nv: (te[i], 0, 0)),
            pl.BlockSpec((1, d, ff), lambda i, te, nv: (te[i], 0, 0)),
            pl.BlockSpec((1, ff, d), lambda i, te, nv: (te[i], 0, 0)),
        ],
        out_specs=pl.BlockSpec((tm, 8, d // 8), lambda i, te, nv: (i, 0, 0)),
    )
    return pl.pallas_call(
        _experts_kernel,
        grid_spec=grid_spec,
        out_shape=jax.ShapeDtypeStruct((n_rows, 8, d // 8), F32),
        compiler_params=pltpu.CompilerParams(
            dimension_semantics=("arbitrary",), vmem_limit_bytes=VMEM_LIMIT),
        name="moe_experts",
    )(tile_expert, n_valid, xs, wg, wu, wd)


def _combine_kernel(pos_ref, pos_next_ref, x_ref, rt_ref, ys_ref, g_ref, b_ref, *rest, tm, alpha, npt):
    o_refs, (buf, sem) = rest[:-2], rest[-2:]
    i = pl.program_id(0)
    n = pl.num_programs(0)

    def gather(p_ref, s):
        def row_copy(r, k):
            return pltpu.make_async_copy(
                ys_ref.at[p_ref[0, 0, 2 * r + k]], buf.at[s, k, r], sem.at[s])

        def start(g, carry):
            for u in range(ROW_DMA_UNROLL):
                row_copy(g * ROW_DMA_UNROLL + u, 0).start(priority=0)
                row_copy(g * ROW_DMA_UNROLL + u, 1).start(priority=1)
            return carry

        lax.fori_loop(0, tm // ROW_DMA_UNROLL, start, 0)

    @pl.when(i == 0)
    def _():
        gather(pos_ref, 0)

    for s in range(2):
        @pl.when(i % 2 == s)
        def _(s=s):
            @pl.when(i + 1 < n)
            def _():
                gather(pos_next_ref, 1 - s)

            for k in range(TOP_K):
                pltpu.make_async_copy(ys_ref.at[pl.ds(0, tm)], buf.at[s, k], sem.at[s]).wait()
            rt = rt_ref[...]
            d = x_ref.shape[1]
            f = rt[:, 2:3] * buf[s, 0].reshape(tm, d) + rt[:, 3:4] * buf[s, 1].reshape(tm, d)
            _token_store(o_refs, _layer_norm(alpha * x_ref[...] + f, g_ref[...], b_ref[...]), npt)


def _combine(x, pos, route_t, ys, g, b, alpha, tm, split):
    t, d = x.shape
    out_specs, out_shape = _token_out(t, d, tm, split)
    npt = None if split is None else split[0] // tm
    n_steps = t // tm
    pos3 = pos.reshape(n_steps, 1, 2 * tm)
    return pl.pallas_call(
        functools.partial(_combine_kernel, tm=tm, alpha=alpha, npt=npt),
        grid=(t // tm,),
        in_specs=[
            pl.BlockSpec((1, 1, 2 * tm), lambda i: (i, 0, 0), memory_space=pltpu.SMEM),
            pl.BlockSpec((1, 1, 2 * tm), lambda i: (jnp.minimum(i + 1, n_steps - 1), 0, 0), memory_space=pltpu.SMEM),
            pl.BlockSpec((tm, d), lambda i: (i, 0)),
            pl.BlockSpec((tm, 8), lambda i: (i, 0)),
            pl.BlockSpec(memory_space=pl.ANY),
            _full_spec((1, d)), _full_spec((1, d)),
        ],
        out_specs=out_specs,
        out_shape=out_shape,
        scratch_shapes=[pltpu.VMEM((2, TOP_K, tm, 8, d // 8), F32), pltpu.SemaphoreType.DMA((2,))],
        compiler_params=pltpu.CompilerParams(dimension_semantics=("arbitrary",), vmem_limit_bytes=VMEM_LIMIT),
        name="moe_combine",
    )(pos3, pos3, x, route_t, ys, g.reshape(1, d), b.reshape(1, d))


def _ffn_moe(x, w_router, wg, wu, wd, g, b, alpha, tr, tm, split, pre):
    t, d = x.shape
    ne = w_router.shape[1]
    ff = wg.shape[2]
    flat = {"wg": wg.reshape(ne * d, ff), "wu": wu.reshape(ne * d, ff), "wd": wd.reshape(ne * ff, d)}
    conv = dict(pre)
    r_keys = [k for k in ("wd",) if k not in conv]
    d_keys = [k for k in ("wg", "wu") if k not in conv]
    route, counts, r_out = _router(x, w_router, tr, [flat[k] for k in r_keys])
    conv.update(zip(r_keys, r_out))
    cnt = counts[:, 0].astype(jnp.int32)
    gsz = ((cnt + tm - 1) // tm) * tm
    ends = jnp.cumsum(gsz)
    offs = ends - gsz
    n_tiles = (TOP_K * t) // tm + ne
    tile_start = jnp.arange(n_tiles, dtype=jnp.int32) * tm
    tile_e = jnp.sum((tile_start[:, None] >= ends[None, :]).astype(jnp.int32), axis=1)
    n_valid = (ends[ne - 1] // tm).astype(jnp.int32).reshape(1)
    last_e = jnp.sum((ends[ne - 1] - 1 >= ends).astype(jnp.int32))
    tile_e = jnp.minimum(tile_e, last_e).astype(jnp.int32)
    i12 = route[0:2].astype(jnp.int32)
    base = sum(jnp.where(i12 == e, offs[e], 0) for e in range(ne))
    pos = (base + route[4:6].astype(jnp.int32)).T.reshape(-1)
    grp = jnp.concatenate([offs, cnt, gsz, n_valid]).astype(jnp.int32)
    xs, d_out = _dispatch(x, pos, grp, n_tiles, tr, tm, [flat[k] for k in d_keys])
    conv.update(zip(d_keys, d_out))
    ys = _experts(xs, tile_e, n_valid, conv["wg"].reshape(ne, d, ff), conv["wu"].reshape(ne, d, ff),
                  conv["wd"].reshape(ne, ff, d), tm)
    return _combine(x, pos, route.T, ys, g, b, alpha, tr, split)


def kernel(x_prompt, x_sample, cache_conv, state_gla, w_in, b_in, conv_w, conv_b, conv_ln_g, conv_ln_b, w_conv_out, b_conv_out, w_gate_up, b_gate, gla_norm_g, w_gla_out, w_o, ln1_g, ln1_b, ln2_g, ln2_b, ff_w_gate, ff_w_up, ff_w_down, w_router, moe_w_gate, moe_w_up, moe_w_down):
    p = dict(w_in=w_in, b_in=b_in, conv_w=conv_w, conv_b=conv_b, conv_ln_g=conv_ln_g, conv_ln_b=conv_ln_b,
             w_conv_out=w_conv_out, b_conv_out=b_conv_out, w_gate_up=w_gate_up, b_gate=b_gate,
             gla_norm_g=gla_norm_g, w_gla_out=w_gla_out, w_o=w_o, ln1_g=ln1_g, ln1_b=ln1_b)
    depth = w_in.shape[0]
    bsz, seq, d = x_prompt.shape
    ns, ls, _ = x_sample.shape
    dc = conv_w.shape[-1]
    rank, dk = w_gate_up.shape[1], w_gate_up.shape[2]
    dv = w_gla_out.shape[1]
    dims = dict(d=d, dc=dc, dk=dk, dv=dv, rank=rank, hk=dk // GLA_HEADS, hv=dv // GLA_HEADS,
                alpha=(2.0 * depth) ** 0.25)
    alpha = dims["alpha"]
    tl = min(512, seq)
    tp = bsz * seq
    ts = ns * ls
    tm = min(512, ts)
    assert seq % tl == 0 and tp % ts == 0 and ts % tm == 0 and ts % tl == 0 and ns % SEQ_PER_STEP == 0

    x_p, x_s, s_blk = x_prompt.reshape(tp, d), x_sample.reshape(ts, d), 0
    hist_p, state_p, hist_s, state_s = [], [], [], []
    w_in_b = w_in.astype(BF16)
    for l in range(depth):
        wts = _mixer_weights(l, p, dims, w_in_b)
        nxt = {}
        if l % 2 == 0 and l + 1 < depth:
            nxt = {k: m[l // 2].reshape(-1, m.shape[-1])
                   for k, m in (("wg", moe_w_gate), ("wu", moe_w_up), ("wd", moe_w_down))}
        x1s, hs, ss = _mixer_sample(x_s, s_blk, cache_conv, state_gla, l, wts, dims, ns, ls)
        x1, hp, sp, *done_mix = _mixer_prompt(x_p, x1s, wts, dims, tl, bsz, seq, [nxt["wu"]] if nxt else [])
        hist_p.append(hp), state_p.append(sp), hist_s.append(hs), state_s.append(ss)
        split = (tp, ts) if l == depth - 1 else None
        if l % 2 == 0:
            x2, done = _ffn_dense(x1, ff_w_gate[l // 2], ff_w_up[l // 2], ff_w_down[l // 2], ln2_g[l], ln2_b[l],
                                  alpha, tm, split, [nxt["wg"], nxt["wd"]] if nxt else [])
            pre_cast = dict(zip(("wg", "wd", "wu"), list(done) + done_mix)) if nxt else {}
        else:
            x2 = _ffn_moe(x1, w_router[l // 2], moe_w_gate[l // 2], moe_w_up[l // 2], moe_w_down[l // 2],
                          ln2_g[l], ln2_b[l], alpha, tm, min(EXPERT_TILE, tm), split, pre_cast)
        if split is None:
            x_p, x_s, s_blk = x2[0], x2[0], tp // ts
    y_p, y_s = x2
    return (y_p.reshape(bsz, seq, d), y_s.reshape(ns, ls, d), jnp.stack(hist_p),
            jnp.stack(state_p).astype(state_gla.dtype), jnp.stack(hist_s), jnp.stack(state_s).astype(state_gla.dtype))
```
